```python
import math
import jax, jax.numpy as jnp
from jax import lax
import numpy as np

D_MODEL = 1024
BATCH = 1
SEQ = 16384
DEPTH = 2

GRID_W = 64
CTX_LEN = 256

CONV_W = 256
CONV_K = 31
RET_HEADS = 6
RET_DK = 32
RET_DV = 64
RET_W = RET_HEADS * RET_DV
RET_CHUNK = 128
RET_GN_EPS = 1e-5
MLA_HEADS = 6
MLA_Q_RANK = 192
MLA_KV_RANK = 128
MLA_D_NOPE = 64
MLA_D_ROPE = 32
MLA_D_V = 64
MLA_W = MLA_HEADS * MLA_D_V
MLA_SCALE = (MLA_D_NOPE + MLA_D_ROPE) ** -0.5
Q_BLOCK = 128
MIX_W = CONV_W + RET_W + MLA_W
ROPE_DIM = 32
ROPE_PAIRS = ROPE_DIM // 4
ROPE_BASE = 10000.0
N_EXPERTS = 16
N_GROUPS = 4
EXPERTS_PER_GROUP = N_EXPERTS // N_GROUPS
TOP_K = 2
D_EXPERT = 256
EPS = 1e-6

IN_SIZES = (2 * CONV_W, RET_HEADS * RET_DK, RET_HEADS * RET_DK, RET_W, RET_W, RET_W,
            MLA_Q_RANK, MLA_KV_RANK, MLA_D_ROPE)
IN_COLS = 2 * CONV_W + 2 * RET_HEADS * RET_DK + 3 * RET_W + MLA_Q_RANK + MLA_KV_RANK + MLA_D_ROPE

kernel_name = "hymba_conv_retnet_mla_groupmoe_dit"


def rmsnorm(x, w):
    xf = x.astype(jnp.float32)
    y = xf * lax.rsqrt(jnp.mean(xf * xf, axis=-1, keepdims=True) + EPS)
    return (y * w.astype(jnp.float32)).astype(x.dtype)


def layernorm(x, w, b):
    xf = x.astype(jnp.float32)
    mu = jnp.mean(xf, axis=-1, keepdims=True)
    var = jnp.mean(jnp.square(xf - mu), axis=-1, keepdims=True)
    y = (xf - mu) * lax.rsqrt(var + EPS)
    return (y * w.astype(jnp.float32) + b.astype(jnp.float32)).astype(x.dtype)


def modulate(h, shift, scale):
    return h * (1.0 + scale) + shift


def split_cols(p, sizes):
    offs = np.cumsum(sizes)[:-1].tolist()
    return jnp.split(p, offs, axis=-1)


def rope_tables(rows):
    r, cl = jnp.meshgrid(jnp.arange(rows, dtype=jnp.float32),
                         jnp.arange(GRID_W, dtype=jnp.float32), indexing="ij")
    r = r.reshape(-1)
    cl = cl.reshape(-1)
    inv = ROPE_BASE ** (-jnp.arange(ROPE_PAIRS, dtype=jnp.float32) / ROPE_PAIRS)
    ar = r[:, None, None] * inv
    ac = cl[:, None, None] * inv
    return (jnp.cos(ar), jnp.sin(ar), jnp.cos(ac), jnp.sin(ac))


def apply_rope(x, tabs):
    cos_r, sin_r, cos_c, sin_c = tabs
    m = x.shape[-1] // 4

    def rot(seg, cos, sin):
        a, b = seg[..., :m], seg[..., m:]
        return jnp.concatenate([a * cos - b * sin, a * sin + b * cos], axis=-1)

    y = jnp.concatenate([rot(x[..., :2 * m], cos_r, sin_r), rot(x[..., 2 * m:], cos_c, sin_c)], axis=-1)
    return y.astype(x.dtype)


def conformer_conv(u2, dw, db, ln_w, ln_b):
    a, g = jnp.split(u2, 2, axis=-1)
    u = a * jax.nn.sigmoid(g)
    y = lax.conv_general_dilated(
        u, dw[:, None, :], window_strides=(1,),
        padding=[(CONV_K // 2, CONV_K // 2)],
        dimension_numbers=("NWC", "WIO", "NWC"),
        feature_group_count=CONV_W) + db
    y = layernorm(y, ln_w, ln_b)
    return jax.nn.silu(y)


def retention_chunked(q, k, v, gamma, r0, with_output):
    B_, L, H, dk = q.shape
    dv = v.shape[-1]
    n = L // RET_CHUNK
    qb = q.reshape(B_, n, RET_CHUNK, H, dk)
    kb = k.reshape(B_, n, RET_CHUNK, H, dk)
    vb = v.reshape(B_, n, RET_CHUNK, H, dv)
    log_g = jnp.log(gamma.astype(jnp.float32))
    idx = jnp.arange(RET_CHUNK, dtype=jnp.float32)
    zeta = jnp.exp((RET_CHUNK - 1 - idx)[:, None] * log_g[None, :])
    d_state = jnp.einsum("bnjhd,bnjhe,jh->bnhde", kb, vb, zeta).astype(jnp.float32)
    g_chunk = jnp.exp(RET_CHUNK * log_g)[None, :, None, None]

    def step(r, dr):
        return g_chunk * r + dr, r

    r_final, r_prev = lax.scan(step, r0, jnp.moveaxis(d_state, 1, 0))
    if not with_output:
        return None, r_final
    r_prev = jnp.moveaxis(r_prev, 0, 1)
    diff = idx[:, None] - idx[None, :]
    dmask = jnp.where(diff[None] >= 0,
                      jnp.exp(jnp.maximum(diff, 0.0)[None] * log_g[:, None, None]), 0.0)
    scores = jnp.einsum("bnihd,bnjhd->bnhij", qb, kb) * dmask
    inner = jnp.einsum("bnhij,bnjhe->bnihe", scores, vb)
    xi = jnp.exp((idx + 1.0)[:, None] * log_g[None, :])
    cross = jnp.einsum("bnihd,bnhde->bnihe", qb, r_prev) * xi[None, None, :, :, None]
    return (inner + cross).reshape(B_, L, H, dv), r_final


def retention_dir(q, k, v, gamma, r0, reverse, with_output):
    if reverse:
        q, k, v = jnp.flip(q, 1), jnp.flip(k, 1), jnp.flip(v, 1)
    out, r = retention_chunked(q, k, v, gamma, r0, with_output)
    if reverse and out is not None:
        out = jnp.flip(out, 1)
    return out, r


def head_groupnorm(y):
    yf = y.astype(jnp.float32)
    mu = jnp.mean(yf, axis=-1, keepdims=True)
    var = jnp.mean(jnp.square(yf - mu), axis=-1, keepdims=True)
    return (yf - mu) * lax.rsqrt(var + RET_GN_EPS)


def retention_mixer(pc, pl, tabs, need_ctx):
    def heads(parts, rotate):
        q, k, v, gf, gb = parts
        B_, L, _ = q.shape
        q = q.reshape(B_, L, RET_HEADS, RET_DK)
        k = k.reshape(B_, L, RET_HEADS, RET_DK) * (RET_DK ** -0.5)
        v = v.reshape(B_, L, RET_HEADS, RET_DV)
        gf = gf.reshape(B_, L, RET_HEADS, RET_DV)
        gb = gb.reshape(B_, L, RET_HEADS, RET_DV)
        if rotate:
            q = apply_rope(q, tabs)
            k = apply_rope(k, tabs)
        return q, k, v, gf, gb

    qc, kc, vc, gfc, gbc = heads(pc, False)
    ql, kl, vl, gfl, gbl = heads(pl, True)
    B_, L = ql.shape[0], ql.shape[1]
    gamma_f = 1.0 - 2.0 ** (-5.0 - jnp.arange(RET_HEADS, dtype=jnp.float32))
    gamma_b = gamma_f[::-1]
    r0 = jnp.zeros((B_, RET_HEADS, RET_DK, RET_DV), jnp.float32)
    outs_l, outs_c = [], []
    for reverse, gamma, gate_l, gate_c in ((False, gamma_f, gfl, gfc), (True, gamma_b, gbl, gbc)):
        oc, rc = retention_dir(qc, kc, vc, gamma, r0, reverse, need_ctx)
        ol, _ = retention_dir(ql, kl, vl, gamma, rc, reverse, True)
        outs_l.append(jax.nn.silu(gate_l) * head_groupnorm(ol).astype(gate_l.dtype))
        if need_ctx:
            outs_c.append(jax.nn.silu(gate_c) * head_groupnorm(oc).astype(gate_c.dtype))
    out_l = (outs_l[0] + outs_l[1]).reshape(B_, L, RET_W)
    if not need_ctx:
        return None, out_l
    out_c = (outs_c[0] + outs_c[1]).reshape(B_, qc.shape[1], RET_W)
    return out_c, out_l


def mla_attend(qn, qr, kn, kr, v):
    s = jnp.einsum("bqhd,bkhd->bhqk", qn, kn) + jnp.einsum("bqhd,bkd->bhqk", qr, kr)
    p = jax.nn.softmax(s.astype(jnp.float32) * MLA_SCALE, axis=-1).astype(v.dtype)
    return jnp.einsum("bhqk,bkhd->bqhd", p, v)


def mla_mixer(pc, pl, q_norm_w, w_uq, kv_norm_w, w_ukv, tabs, need_ctx):
    def project(parts, rotate, with_q):
        cq, ckv, kr = parts
        B_, L, _ = ckv.shape
        kv = (rmsnorm(ckv, kv_norm_w) @ w_ukv).reshape(B_, L, MLA_HEADS, MLA_D_NOPE + MLA_D_V)
        kn, v = kv[..., :MLA_D_NOPE], kv[..., MLA_D_NOPE:]
        kr = kr[:, :, None, :]
        if rotate:
            kr = apply_rope(kr, tabs)
        kr = kr[:, :, 0, :]
        if not with_q:
            return None, None, kn, kr, v
        q = (rmsnorm(cq, q_norm_w) @ w_uq).reshape(B_, L, MLA_HEADS, MLA_D_NOPE + MLA_D_ROPE)
        qn, qr = q[..., :MLA_D_NOPE], q[..., MLA_D_NOPE:]
        if rotate:
            qr = apply_rope(qr, tabs)
        return qn, qr, kn, kr, v

    qn_c, qr_c, kn_c, kr_c, v_c = project(pc, False, need_ctx)
    qn_l, qr_l, kn_l, kr_l, v_l = project(pl, True, True)
    kn_all = jnp.concatenate([kn_c, kn_l], axis=1)
    kr_all = jnp.concatenate([kr_c, kr_l], axis=1)
    v_all = jnp.concatenate([v_c, v_l], axis=1)
    B_, L = qn_l.shape[0], qn_l.shape[1]
    nb = L // Q_BLOCK

    def to_blocks(t):
        return jnp.moveaxis(t.reshape(B_, nb, Q_BLOCK, *t.shape[2:]), 1, 0)

    out_l = lax.map(lambda qb: mla_attend(qb[0], qb[1], kn_all, kr_all, v_all),
                    (to_blocks(qn_l), to_blocks(qr_l)))
    out_l = jnp.moveaxis(out_l, 0, 1).reshape(B_, L, MLA_W)
    if not need_ctx:
        return None, out_l
    out_c = mla_attend(qn_c, qr_c, kn_c, kr_c, v_c).reshape(B_, qn_c.shape[1], MLA_W)
    return out_c, out_l


def mixing_sublayer(h_c, h_l, w_in, conv_dw, conv_b, conv_ln_w, conv_ln_b,
                    q_norm_w, w_uq, kv_norm_w, w_ukv, w_out, tabs, need_ctx):
    parts_l = split_cols(h_l @ w_in, IN_SIZES)
    parts_c = split_cols(h_c @ w_in, IN_SIZES)
    conv_l = conformer_conv(parts_l[0], conv_dw, conv_b, conv_ln_w, conv_ln_b)
    ret_c, ret_l = retention_mixer(parts_c[1:6], parts_l[1:6], tabs, need_ctx)
    mla_c, mla_l = mla_mixer(parts_c[6:], parts_l[6:], q_norm_w, w_uq, kv_norm_w, w_ukv, tabs, need_ctx)
    o_l = jnp.concatenate([conv_l, ret_l.astype(h_l.dtype), mla_l], axis=-1) @ w_out
    if not need_ctx:
        return None, o_l
    conv_c = conformer_conv(parts_c[0], conv_dw, conv_b, conv_ln_w, conv_ln_b)
    o_c = jnp.concatenate([conv_c, ret_c.astype(h_c.dtype), mla_c], axis=-1) @ w_out
    return o_c, o_l


def grouped_moe(h, router_w, router_bias, w_gate, w_up, w_down):
    B_, L, D = h.shape
    t = h.reshape(B_ * L, D)
    aff = jax.nn.sigmoid((t @ router_w).astype(jnp.float32))
    sel = aff + router_bias.astype(jnp.float32)
    grp = sel.reshape(-1, N_GROUPS, EXPERTS_PER_GROUP)
    group_score = jnp.sum(lax.top_k(grp, 2)[0], axis=-1)
    best_group = jnp.argmax(group_score, axis=-1)
    in_group = (jnp.arange(N_EXPERTS) // EXPERTS_PER_GROUP)[None, :] == best_group[:, None]
    _, idx = lax.top_k(jnp.where(in_group, sel, -jnp.inf), TOP_K)
    w = jnp.take_along_axis(aff, idx, axis=-1)
    w = w / jnp.sum(w, axis=-1, keepdims=True)
    gates = jnp.sum(jax.nn.one_hot(idx, N_EXPERTS, dtype=jnp.float32) * w[..., None], axis=1)
    out = jnp.zeros((t.shape[0], D), jnp.float32)
    for e in range(N_EXPERTS):
        he = jax.nn.silu(t @ w_gate[e]) * (t @ w_up[e])
        out = out + gates[:, e:e + 1] * (he @ w_down[e])
    return out.astype(h.dtype).reshape(B_, L, D)


def setup_inputs(seed: int = 0) -> dict:
    key = jax.random.key(seed)
    ks = jax.random.split(key, 24)
    f32 = jnp.float32

    def nrm(k, shape, scale):
        return jax.random.normal(k, shape, f32) * scale

    D = D_MODEL
    return {
        "x": nrm(ks[0], (BATCH, SEQ, D), 1.0),
        "c": nrm(ks[1], (BATCH, D), 1.0),
        "ctx": nrm(ks[2], (BATCH, CTX_LEN, D), 1.0),
        "c_ctx": nrm(ks[3], (D,), 1.0),
        "ada_w": nrm(ks[4], (DEPTH, D, 6 * D), 0.5 * D ** -0.5),
        "ada_b": nrm(ks[5], (DEPTH, 6 * D), 0.02),
        "norm1_w": 1.0 + nrm(ks[6], (DEPTH, D), 0.05),
        "norm2_w": 1.0 + nrm(ks[7], (DEPTH, D), 0.05),
        "w_in": nrm(ks[8], (DEPTH, D, IN_COLS), D ** -0.5),
        "conv_dw": nrm(ks[9], (DEPTH, CONV_K, CONV_W), CONV_K ** -0.5),
        "conv_b": nrm(ks[10], (DEPTH, CONV_W), 0.02),
        "conv_ln_w": 1.0 + nrm(ks[11], (DEPTH, CONV_W), 0.05),
        "conv_ln_b": nrm(ks[12], (DEPTH, CONV_W), 0.02),
        "mla_q_norm_w": 1.0 + nrm(ks[13], (DEPTH, MLA_Q_RANK), 0.05),
        "mla_w_uq": nrm(ks[14], (DEPTH, MLA_Q_RANK, MLA_HEADS * (MLA_D_NOPE + MLA_D_ROPE)), MLA_Q_RANK ** -0.5),
        "mla_kv_norm_w": 1.0 + nrm(ks[15], (DEPTH, MLA_KV_RANK), 0.05),
        "mla_w_ukv": nrm(ks[16], (DEPTH, MLA_KV_RANK, MLA_HEADS * (MLA_D_NOPE + MLA_D_V)), MLA_KV_RANK ** -0.5),
        "w_out": nrm(ks[17], (DEPTH, MIX_W, D), MIX_W ** -0.5),
        "router_w": nrm(ks[18], (D, N_EXPERTS), D ** -0.5),
        "router_bias": nrm(ks[19], (N_EXPERTS,), 0.01),
        "moe_w_gate": nrm(ks[20], (DEPTH, N_EXPERTS, D, D_EXPERT), D ** -0.5),
        "moe_w_up": nrm(ks[21], (DEPTH, N_EXPERTS, D, D_EXPERT), D ** -0.5),
        "moe_w_down": nrm(ks[22], (DEPTH, N_EXPERTS, D_EXPERT, D), D_EXPERT ** -0.5),
        "final_norm_w": 1.0 + nrm(ks[23], (D,), 0.05),
    }


def reference(x, c, ctx, c_ctx, ada_w, ada_b, norm1_w, norm2_w, w_in, conv_dw, conv_b,
              conv_ln_w, conv_ln_b, mla_q_norm_w, mla_w_uq, mla_kv_norm_w, mla_w_ukv, w_out,
              router_w, router_bias, moe_w_gate, moe_w_up, moe_w_down, final_norm_w):
    L = x.shape[1]
    ROWS = L // GRID_W
    tabs = rope_tables(ROWS)
    silu_c = jax.nn.silu(c)
    silu_cc = jax.nn.silu(c_ctx)
    xc = ctx
    for l in range(DEPTH):
        need_ctx = l < DEPTH - 1
        mod_l = (silu_c @ ada_w[l] + ada_b[l])[:, None, :]
        mod_c = (silu_cc @ ada_w[l] + ada_b[l])[None, None, :]
        sh1, sc1, g1, sh2, sc2, g2 = jnp.split(mod_l, 6, axis=-1)
        csh1, csc1, cg1, csh2, csc2, cg2 = jnp.split(mod_c, 6, axis=-1)

        h_l = modulate(rmsnorm(x, norm1_w[l]), sh1, sc1)
        h_c = modulate(rmsnorm(xc, norm1_w[l]), csh1, csc1)
        o_c, o_l = mixing_sublayer(h_c, h_l, w_in[l], conv_dw[l], conv_b[l], conv_ln_w[l], conv_ln_b[l],
                                   mla_q_norm_w[l], mla_w_uq[l], mla_kv_norm_w[l], mla_w_ukv[l],
                                   w_out[l], tabs, need_ctx)
        x = x + g1 * o_l
        h_l = modulate(rmsnorm(x, norm2_w[l]), sh2, sc2)
        x = x + g2 * grouped_moe(h_l, router_w, router_bias, moe_w_gate[l], moe_w_up[l], moe_w_down[l])
        if need_ctx:
            xc = xc + cg1 * o_c
            h_c = modulate(rmsnorm(xc, norm2_w[l]), csh2, csc2)
            xc = xc + cg2 * grouped_moe(h_c, router_w, router_bias, moe_w_gate[l], moe_w_up[l], moe_w_down[l])
    return rmsnorm(x, final_norm_w)
```

```python
import functools
import math

import numpy as np
import jax
import jax.numpy as jnp
from jax import lax
from jax.experimental import pallas as pl
from jax.experimental.pallas import tpu as pltpu

F32 = jnp.float32
BF16 = jnp.bfloat16

D_MODEL = 1024
GRID_W = 64
CONV_W = 256
CONV_K = 31
RET_HEADS = 6
RET_DK = 32
RET_DV = 64
RET_QK_W = RET_HEADS * RET_DK
RET_W = RET_HEADS * RET_DV
RET_CHUNK = 128
RET_GN_EPS = 1e-5
MLA_HEADS = 6
MLA_Q_RANK = 192
MLA_KV_RANK = 128
MLA_D_NOPE = 64
MLA_D_ROPE = 32
MLA_D_V = 64
MLA_SCALE = (MLA_D_NOPE + MLA_D_ROPE) ** -0.5
ROPE_DIM = 32
ROPE_PAIRS = ROPE_DIM // 4
ROPE_BASE = 10000.0
N_EXPERTS = 16
N_GROUPS = 4
EXPERTS_PER_GROUP = N_EXPERTS // N_GROUPS
D_EXPERT = 256
EPS = 1e-6

LANES = 128
HEAD_PAD = LANES
ATT_W = MLA_HEADS * HEAD_PAD
LOG2E = math.log2(math.e)

C_CONV = 0
C_RQ = 512
C_RQS = 768
C_RK = 1024
C_RKS = 1280
C_RV = 1536
C_RGF = 1920
C_RGB = 2304
C_CQ = 2688
C_CKV = 2944
C_KR = 3072
C_KRS = 3200
IN_EXT = 3328

ROW_TILE = 256
MOE_TILE = 640
ATT_TQ = 256
ATT_TK = 1280
VMEM_LIMIT = 48 * 1024 * 1024


def _cparams(sem):
    return pltpu.CompilerParams(dimension_semantics=sem, vmem_limit_bytes=VMEM_LIMIT)


def _silu(x):
    return x * jax.nn.sigmoid(x)


def _ada_kernel(c_ref, w_ref, b_ref, o_ref):
    s = _silu(c_ref[...])
    o_ref[...] = jnp.dot(s, w_ref[...], preferred_element_type=F32,
                         precision=lax.Precision.HIGHEST) + b_ref[...]


def _ada_call(cond, ada_w, ada_b):
    depth, d, n = ada_w.shape
    tn = 1536
    return pl.pallas_call(
        _ada_kernel,
        grid=(depth, n // tn),
        in_specs=[pl.BlockSpec((8, d), lambda l, j: (0, 0)),
                  pl.BlockSpec((None, d, tn), lambda l, j: (l, 0, j)),
                  pl.BlockSpec((None, 1, tn), lambda l, j: (l, 0, j))],
        out_specs=pl.BlockSpec((None, 8, tn), lambda l, j: (l, 0, j)),
        out_shape=jax.ShapeDtypeStruct((depth, 8, n), F32),
        compiler_params=_cparams(("arbitrary", "arbitrary")),
        name="adaln",
    )(cond, ada_w, ada_b.reshape(depth, 1, n))


def _inproj_kernel(x_ref, nw_ref, mod_ref, w_ref, cs_ref, sn_ref, qnw_ref, wuq_ref, kvnw_ref, wukv_ref,
                   u_ref, rq_ref, rk_ref, rv_ref, gf_ref, gb_ref, q_ref, k_ref, v_ref):
    x = x_ref[...]
    y = x * lax.rsqrt(jnp.mean(x * x, axis=-1, keepdims=True) + EPS) * nw_ref[...]
    h = (y * (1.0 + mod_ref[1:2, :]) + mod_ref[0:1, :]).astype(BF16)

    def proj(c0, width):
        return jnp.dot(h, w_ref[:, c0:c0 + width], preferred_element_type=F32)

    cs = cs_ref[...]
    sn = sn_ref[...]
    lane = lax.broadcasted_iota(jnp.int32, (1, LANES), 1)
    rope_lanes = (lane >= MLA_D_NOPE) & (lane < MLA_D_NOPE + MLA_D_ROPE)
    csq = jnp.where(rope_lanes, cs, 1.0)
    snq = jnp.where(rope_lanes, sn, 0.0)

    ag = proj(C_CONV, 2 * CONV_W)
    u_ref[...] = ag[:, :CONV_W] * jax.nn.sigmoid(ag[:, CONV_W:])

    cs192 = jnp.concatenate([cs, cs[:, :RET_QK_W - LANES]], axis=1)
    sn192 = jnp.concatenate([sn, sn[:, :RET_QK_W - LANES]], axis=1)
    rq = proj(C_RQ, 256)[:, :RET_QK_W] * cs192 + proj(C_RQS, 256)[:, :RET_QK_W] * sn192
    rq_ref[...] = rq.astype(BF16)
    rk = proj(C_RK, 256)[:, :RET_QK_W] * cs192 + proj(C_RKS, 256)[:, :RET_QK_W] * sn192
    rk_ref[...] = (rk * (RET_DK ** -0.5)).astype(BF16)
    rv_ref[...] = proj(C_RV, RET_W).astype(BF16)
    gf_ref[...] = _silu(proj(C_RGF, RET_W))
    gb_ref[...] = _silu(proj(C_RGB, RET_W))

    cq = proj(C_CQ, 256)[:, :MLA_Q_RANK]
    cqn = (cq * lax.rsqrt(jnp.mean(cq * cq, axis=-1, keepdims=True) + EPS) * qnw_ref[...]).astype(BF16)
    qa = jnp.dot(cqn, wuq_ref[:, :ATT_W], preferred_element_type=F32)
    qb = jnp.dot(cqn, wuq_ref[:, ATT_W:], preferred_element_type=F32)
    ckv = proj(C_CKV, MLA_KV_RANK)
    ckvn = (ckv * lax.rsqrt(jnp.mean(ckv * ckv, axis=-1, keepdims=True) + EPS) * kvnw_ref[...]).astype(BF16)
    ka = jnp.dot(ckvn, wukv_ref[:, :ATT_W], preferred_element_type=F32)
    va = jnp.dot(ckvn, wukv_ref[:, ATT_W:], preferred_element_type=F32)
    kr = proj(C_KR, LANES) * csq + proj(C_KRS, LANES) * snq
    ones_lane = (lane == MLA_D_V).astype(F32)
    for hd in range(MLA_HEADS):
        sl = slice(hd * HEAD_PAD, (hd + 1) * HEAD_PAD)
        q_ref[hd] = ((qa[:, sl] * csq + qb[:, sl] * snq) * (MLA_SCALE * LOG2E)).astype(BF16)
        k_ref[hd] = (ka[:, sl] + kr).astype(BF16)
        v_ref[hd] = (va[:, sl] + ones_lane).astype(BF16)


def _inproj_call(x_all, nw, mod, w_ext, cs, sn, qnw, wuq, kvnw, wukv, n_ctx_tiles):
    t_all, d = x_all.shape
    tm = ROW_TILE
    row = lambda i: (i, 0)
    const2 = lambda i: (0, 0)
    head = lambda i: (0, i, 0)
    cls = lambda i: (jnp.where(i < n_ctx_tiles, 0, 1), 0, 0)
    out_shape = (
        jax.ShapeDtypeStruct((t_all, CONV_W), F32),
        jax.ShapeDtypeStruct((t_all, RET_QK_W), BF16),
        jax.ShapeDtypeStruct((t_all, RET_QK_W), BF16),
        jax.ShapeDtypeStruct((t_all, RET_W), BF16),
        jax.ShapeDtypeStruct((t_all, RET_W), F32),
        jax.ShapeDtypeStruct((t_all, RET_W), F32),
        jax.ShapeDtypeStruct((MLA_HEADS, t_all, HEAD_PAD), BF16),
        jax.ShapeDtypeStruct((MLA_HEADS, t_all, HEAD_PAD), BF16),
        jax.ShapeDtypeStruct((MLA_HEADS, t_all, HEAD_PAD), BF16),
    )
    out_specs = (
        pl.BlockSpec((tm, CONV_W), row),
        pl.BlockSpec((tm, RET_QK_W), row),
        pl.BlockSpec((tm, RET_QK_W), row),
        pl.BlockSpec((tm, RET_W), row),
        pl.BlockSpec((tm, RET_W), row),
        pl.BlockSpec((tm, RET_W), row),
        pl.BlockSpec((MLA_HEADS, tm, HEAD_PAD), head),
        pl.BlockSpec((MLA_HEADS, tm, HEAD_PAD), head),
        pl.BlockSpec((MLA_HEADS, tm, HEAD_PAD), head),
    )
    return pl.pallas_call(
        _inproj_kernel,
        grid=(t_all // tm,),
        in_specs=[pl.BlockSpec((tm, d), row),
                  pl.BlockSpec((1, d), const2),
                  pl.BlockSpec((None, 2, d), cls),
                  pl.BlockSpec((d, IN_EXT), const2),
                  pl.BlockSpec((tm, LANES), row),
                  pl.BlockSpec((tm, LANES), row),
                  pl.BlockSpec((1, MLA_Q_RANK), const2),
                  pl.BlockSpec((MLA_Q_RANK, 2 * ATT_W), const2),
                  pl.BlockSpec((1, MLA_KV_RANK), const2),
                  pl.BlockSpec((MLA_KV_RANK, 2 * ATT_W), const2)],
        out_specs=out_specs,
        out_shape=out_shape,
        compiler_params=_cparams(("arbitrary",)),
        name="inproj",
    )(x_all, nw, mod, w_ext, cs, sn, qnw, wuq, kvnw, wukv)


CONV_HALO = 16
CONV_SUB = 64


def _conv_kernel(prev_ref, cur_ref, next_ref, dw_ref, b_ref, lnw_ref, lnb_ref, o_ref, ext_ref, *, n_ctx_tiles):
    i = pl.program_id(0)
    n = pl.num_programs(0)
    tm = cur_ref.shape[0]
    seq_start = (i == 0) | (i == n_ctx_tiles)
    seq_end = (i == n_ctx_tiles - 1) | (i == n - 1)
    ext_ref[0:CONV_HALO, :] = jnp.where(seq_start, 0.0, prev_ref[...])
    ext_ref[CONV_HALO:CONV_HALO + tm, :] = cur_ref[...]
    ext_ref[CONV_HALO + tm:, :] = jnp.where(seq_end, 0.0, next_ref[...])
    base = CONV_HALO - CONV_K // 2
    for r in range(tm // CONV_SUB):
        acc = jnp.zeros((CONV_SUB, CONV_W), F32) + b_ref[...]
        for k in range(CONV_K):
            acc = acc + ext_ref[pl.ds(r * CONV_SUB + base + k, CONV_SUB), :] * dw_ref[k:k + 1, :]
        mu = jnp.mean(acc, axis=-1, keepdims=True)
        dlt = acc - mu
        var = jnp.mean(dlt * dlt, axis=-1, keepdims=True)
        y = dlt * lax.rsqrt(var + EPS) * lnw_ref[...] + lnb_ref[...]
        o_ref[r * CONV_SUB:(r + 1) * CONV_SUB, :] = _silu(y).astype(BF16)


def _conv_call(u, dw, b, lnw, lnb, n_ctx_tiles):
    t_all = u.shape[0]
    tm = ROW_TILE
    hpt = tm // CONV_HALO
    n_halo = t_all // CONV_HALO
    const2 = lambda i: (0, 0)
    return pl.pallas_call(
        functools.partial(_conv_kernel, n_ctx_tiles=n_ctx_tiles),
        grid=(t_all // tm,),
        in_specs=[pl.BlockSpec((CONV_HALO, CONV_W), lambda i: (jnp.maximum(i * hpt - 1, 0), 0)),
                  pl.BlockSpec((tm, CONV_W), lambda i: (i, 0)),
                  pl.BlockSpec((CONV_HALO, CONV_W), lambda i: (jnp.minimum((i + 1) * hpt, n_halo - 1), 0)),
                  pl.BlockSpec((CONV_K, CONV_W), const2),
                  pl.BlockSpec((1, CONV_W), const2),
                  pl.BlockSpec((1, CONV_W), const2),
                  pl.BlockSpec((1, CONV_W), const2)],
        out_specs=pl.BlockSpec((tm, CONV_W), lambda i: (i, 0)),
        out_shape=jax.ShapeDtypeStruct((t_all, CONV_W), BF16),
        scratch_shapes=[pltpu.VMEM((tm + 2 * CONV_HALO, CONV_W), F32)],
        compiler_params=_cparams(("arbitrary",)),
        name="conv",
    )(u, u, u, dw, b, lnw, lnb)


def _split_dot(x, a):
    hi = x.astype(BF16)
    lo = (x - hi.astype(F32)).astype(BF16)
    return (jnp.dot(hi, a, preferred_element_type=F32) + jnp.dot(lo, a, preferred_element_type=F32))


def _ret_direction(q, k, v, gate, r_ref, dmask, xi, zeta, gchunk, bdmask, avg, o_ref):
    lane_k = lax.broadcasted_iota(jnp.int32, (1, RET_QK_W), 1) // RET_DK
    lane_v = lax.broadcasted_iota(jnp.int32, (1, RET_W), 1) // RET_DV
    zero_k = jnp.zeros_like(k)
    zero_v = jnp.zeros_like(v)
    k_bd = jnp.concatenate([jnp.where(lane_k == hd, k, zero_k) for hd in range(RET_HEADS)], axis=0)
    v_bd = jnp.concatenate([jnp.where(lane_v == hd, v, zero_v) for hd in range(RET_HEADS)], axis=0)
    s = lax.dot_general(q, k_bd, (((1,), (1,)), ((), ())), preferred_element_type=F32)
    s = (s * dmask).astype(BF16)
    inner = jnp.dot(s, v_bd, preferred_element_type=F32)
    r = r_ref[...]
    cross = jnp.dot(q, r.astype(BF16), preferred_element_type=F32) * xi
    o = inner + cross
    kz = (k.astype(F32) * zeta).astype(BF16)
    ds = lax.dot_general(kz, v, (((0,), (0,)), ((), ())), preferred_element_type=F32)
    r_ref[...] = gchunk * r + ds * bdmask
    mu = _split_dot(o, avg)
    dlt = o - mu
    var = _split_dot(dlt * dlt, avg)
    o_ref[...] = (gate * (dlt * lax.rsqrt(var + RET_GN_EPS))).astype(BF16)


def _ret_kernel(qf_ref, kf_ref, vf_ref, gf_ref, qb_ref, kb_ref, vb_ref, gb_ref,
                dmf_ref, dmb_ref, xif_ref, xib_ref, ztf_ref, ztb_ref, gcf_ref, gcb_ref, bdm_ref, avg_ref,
                of_ref, ob_ref, rf_ref, rb_ref):
    @pl.when(pl.program_id(0) == 0)
    def _():
        rf_ref[...] = jnp.zeros_like(rf_ref)
        rb_ref[...] = jnp.zeros_like(rb_ref)

    bdm = bdm_ref[...]
    avg = avg_ref[...]
    _ret_direction(qf_ref[...], kf_ref[...], vf_ref[...], gf_ref[...], rf_ref, dmf_ref[...], xif_ref[...],
                   ztf_ref[...], gcf_ref[...], bdm, avg, of_ref)
    _ret_direction(qb_ref[...], kb_ref[...], vb_ref[...], gb_ref[...], rb_ref, dmb_ref[...], xib_ref[...],
                   ztb_ref[...], gcb_ref[...], bdm, avg, ob_ref)


def _ret_tables():
    c = RET_CHUNK
    gamma_f = 1.0 - 2.0 ** (-5.0 - jnp.arange(RET_HEADS, dtype=F32))
    gamma_b = gamma_f[::-1]
    idx = jnp.arange(c, dtype=F32)
    diff = idx[:, None] - idx[None, :]

    def tables(gamma, reverse):
        lg = jnp.log(gamma)
        d = -diff if reverse else diff
        dm = jnp.where(d[None] >= 0, jnp.exp(jnp.maximum(d, 0.0)[None] * lg[:, None, None]), 0.0)
        dm = jnp.transpose(dm, (1, 0, 2)).reshape(c, RET_HEADS * c)
        xi_e = (c - idx) if reverse else (idx + 1.0)
        zt_e = idx if reverse else (c - 1.0 - idx)
        xi = jnp.repeat(jnp.exp(xi_e[:, None] * lg[None, :]), RET_DV, axis=1)
        zt = jnp.repeat(jnp.exp(zt_e[:, None] * lg[None, :]), RET_DK, axis=1)
        gc = jnp.repeat(jnp.exp(c * lg), RET_DV)[None, :]
        return dm, xi, zt, gc

    dmf, xif, ztf, gcf = tables(gamma_f, False)
    dmb, xib, ztb, gcb = tables(gamma_b, True)
    hk = jnp.arange(RET_QK_W) // RET_DK
    hv = jnp.arange(RET_W) // RET_DV
    bdm = (hk[:, None] == hv[None, :]).astype(F32)
    avg = ((hv[:, None] == hv[None, :]).astype(F32) / RET_DV).astype(BF16)
    return (dmf, dmb, xif, xib, ztf, ztb, gcf, gcb, bdm, avg)


def _ret_call(rq, rk, rv, gf, gb, tabs, n_ctx_chunks):
    t_all = rq.shape[0]
    c = RET_CHUNK
    n = t_all // c

    def fwd(i):
        return (i, 0)

    def bwd(i):
        return (jnp.where(i < n_ctx_chunks, n_ctx_chunks - 1 - i, n - 1 + n_ctx_chunks - i), 0)

    const2 = lambda i: (0, 0)
    tab_specs = [pl.BlockSpec(t.shape, const2) for t in tabs]
    return pl.pallas_call(
        _ret_kernel,
        grid=(n,),
        in_specs=[pl.BlockSpec((c, RET_QK_W), fwd), pl.BlockSpec((c, RET_QK_W), fwd),
                  pl.BlockSpec((c, RET_W), fwd), pl.BlockSpec((c, RET_W), fwd),
                  pl.BlockSpec((c, RET_QK_W), bwd), pl.BlockSpec((c, RET_QK_W), bwd),
                  pl.BlockSpec((c, RET_W), bwd), pl.BlockSpec((c, RET_W), bwd)] + tab_specs,
        out_specs=(pl.BlockSpec((c, RET_W), fwd), pl.BlockSpec((c, RET_W), bwd)),
        out_shape=(jax.ShapeDtypeStruct((t_all, RET_W), BF16), jax.ShapeDtypeStruct((t_all, RET_W), BF16)),
        scratch_shapes=[pltpu.VMEM((RET_QK_W, RET_W), F32), pltpu.VMEM((RET_QK_W, RET_W), F32)],
        compiler_params=_cparams(("arbitrary",)),
        name="retention",
    )(rq, rk, rv, gf, rq, rk, rv, gb, *tabs)


def _attn_kernel(q_ref, k_ref, vt_ref, o_ref):
    q = q_ref[...]
    tq = q.shape[0]
    nkc = k_ref.shape[0]

    def body(c, carry):
        m, acc = carry
        s = lax.dot_general(k_ref[c], q, (((1,), (1,)), ((), ())), preferred_element_type=F32)
        m_new = jnp.maximum(m, jnp.max(s, axis=0, keepdims=True))
        alpha = jnp.exp2(m - m_new)
        p = jnp.exp2(s - m_new).astype(BF16)
        acc = alpha * acc + jnp.dot(vt_ref[c], p, preferred_element_type=F32)
        return m_new, acc

    m0 = jnp.full((1, tq), -jnp.inf, F32)
    acc0 = jnp.zeros((HEAD_PAD, tq), F32)
    _, acc = lax.fori_loop(0, nkc, body, (m0, acc0))
    out_t = acc / acc[MLA_D_V:MLA_D_V + 1, :]
    o_ref[...] = out_t.T.astype(BF16)


def _attn_call(q, k, v, tq, tk):
    nh, t_q, _ = q.shape
    t_k = k.shape[1]
    nkc = t_k // tk
    k4 = k.reshape(nh, nkc, tk, HEAD_PAD)
    vt4 = jnp.swapaxes(v.reshape(nh, nkc, tk, HEAD_PAD), 2, 3)
    return pl.pallas_call(
        _attn_kernel,
        grid=(nh, t_q // tq),
        in_specs=[pl.BlockSpec((None, tq, HEAD_PAD), lambda h, j: (h, j, 0)),
                  pl.BlockSpec((None, nkc, tk, HEAD_PAD), lambda h, j: (h, 0, 0, 0)),
                  pl.BlockSpec((None, nkc, HEAD_PAD, tk), lambda h, j: (h, 0, 0, 0))],
        out_specs=pl.BlockSpec((tq, HEAD_PAD), lambda h, j: (j, h)),
        out_shape=jax.ShapeDtypeStruct((t_q, nh * HEAD_PAD), BF16),
        compiler_params=_cparams(("arbitrary", "arbitrary")),
        name="attention",
    )(q, k4, vt4)


def _top2_sum(a, b, c, d):
    hi1, lo1 = jnp.maximum(a, b), jnp.minimum(a, b)
    hi2, lo2 = jnp.maximum(c, d), jnp.minimum(c, d)
    return jnp.maximum(hi1, hi2) + jnp.maximum(jnp.minimum(hi1, hi2), jnp.maximum(lo1, lo2))


def _gates_t(aff, sel, gt_ref):
    rows = [sel[e:e + 1, :] for e in range(N_EXPERTS)]
    g_score = [_top2_sum(*rows[g * EXPERTS_PER_GROUP:(g + 1) * EXPERTS_PER_GROUP]) for g in range(N_GROUPS)]
    best = g_score[0]
    best_g = jnp.zeros_like(best, dtype=jnp.int32)
    for g in range(1, N_GROUPS):
        better = g_score[g] > best
        best = jnp.where(better, g_score[g], best)
        best_g = jnp.where(better, g, best_g)
    picked = []
    for e in range(N_EXPERTS):
        g = e // EXPERTS_PER_GROUP
        rank = jnp.zeros_like(best_g)
        for o in range(g * EXPERTS_PER_GROUP, (g + 1) * EXPERTS_PER_GROUP):
            if o == e:
                continue
            ahead = (rows[o] >= rows[e]) if o < e else (rows[o] > rows[e])
            rank = rank + jnp.where(ahead, 1, 0)
        picked.append(jnp.where(best_g == g, rank, 2) < 2)
    w = [jnp.where(picked[e], aff[e:e + 1, :], 0.0) for e in range(N_EXPERTS)]
    total = w[0]
    for e in range(1, N_EXPERTS):
        total = total + w[e]
    for e in range(N_EXPERTS):
        gt_ref[e:e + 1, :] = w[e] / total


def _outproj_kernel(x_ref, conv_ref, of_ref, ob_ref, att_ref, w_ref, mod_ref, nw_ref, rwt_ref, rb_ref,
                    x1_ref, h2_ref, gt_ref):
    ret = (of_ref[...].astype(F32) + ob_ref[...].astype(F32)).astype(BF16)
    o = jnp.dot(conv_ref[...], w_ref[0:CONV_W, :], preferred_element_type=F32)
    o = o + jnp.dot(ret, w_ref[CONV_W:CONV_W + RET_W, :], preferred_element_type=F32)
    o = o + jnp.dot(att_ref[...], w_ref[CONV_W + RET_W:, :], preferred_element_type=F32)
    x1 = x_ref[...] + mod_ref[2:3, :] * o
    x1_ref[...] = x1
    y = x1 * lax.rsqrt(jnp.mean(x1 * x1, axis=-1, keepdims=True) + EPS) * nw_ref[...]
    h2 = y * (1.0 + mod_ref[4:5, :]) + mod_ref[3:4, :]
    h2_ref[...] = h2.astype(BF16)
    logits = lax.dot_general(rwt_ref[...], h2, (((1,), (1,)), ((), ())), preferred_element_type=F32,
                             precision=lax.Precision.HIGHEST)
    aff = jax.nn.sigmoid(logits)
    _gates_t(aff, aff + rb_ref[...], gt_ref)


def _outproj_call(x_all, conv, of, ob, att, w_ext, mod, nw, rwt, rb, n_ctx_tiles):
    t_all, d = x_all.shape
    tm = ROW_TILE
    row = lambda i: (i, 0)
    const2 = lambda i: (0, 0)
    cls = lambda i: (jnp.where(i < n_ctx_tiles, 0, 1), 0, 0)
    return pl.pallas_call(
        _outproj_kernel,
        grid=(t_all // tm,),
        in_specs=[pl.BlockSpec((tm, d), row),
                  pl.BlockSpec((tm, CONV_W), row),
                  pl.BlockSpec((tm, RET_W), row),
                  pl.BlockSpec((tm, RET_W), row),
                  pl.BlockSpec((tm, ATT_W), row),
                  pl.BlockSpec(w_ext.shape, const2),
                  pl.BlockSpec((None, 8, d), cls),
                  pl.BlockSpec((1, d), const2),
                  pl.BlockSpec((N_EXPERTS, d), const2),
                  pl.BlockSpec((N_EXPERTS, 1), const2)],
        out_specs=(pl.BlockSpec((tm, d), row), pl.BlockSpec((tm, d), row),
                   pl.BlockSpec((N_EXPERTS, tm), lambda i: (0, i))),
        out_shape=(jax.ShapeDtypeStruct((t_all, d), F32), jax.ShapeDtypeStruct((t_all, d), BF16),
                   jax.ShapeDtypeStruct((N_EXPERTS, t_all), F32)),
        compiler_params=_cparams(("arbitrary",)),
        name="outproj",
    )(x_all, conv, of, ob, att, w_ext, mod, nw, rwt, rb)


def _moe_kernel(h_ref, g_ref, x1_ref, g2_ref, wgu_ref, wd_ref, o_ref, acc_ref, *, n_ctx_rows):
    e = pl.program_id(1)

    @pl.when(e == 0)
    def _():
        acc_ref[...] = jnp.zeros_like(acc_ref)

    gu = jnp.dot(h_ref[...], wgu_ref[...], preferred_element_type=F32)
    he = _silu(gu[:, :D_EXPERT]) * gu[:, D_EXPERT:]
    pick = (lax.broadcasted_iota(jnp.int32, (N_EXPERTS, LANES), 0) == e).astype(BF16)
    gate = _split_dot(g_ref[...], pick)
    heg = (he * jnp.concatenate([gate, gate], axis=1)).astype(BF16)
    acc_ref[...] += jnp.dot(heg, wd_ref[...], preferred_element_type=F32)

    @pl.when(e == pl.num_programs(1) - 1)
    def _():
        tm = acc_ref.shape[0]
        row = pl.program_id(0) * tm + lax.broadcasted_iota(jnp.int32, (tm, 1), 0)
        g2 = jnp.where(row >= n_ctx_rows, g2_ref[1:2, :], g2_ref[0:1, :])
        o_ref[...] = x1_ref[...] + g2 * acc_ref[...]


def _moe_call(h2, gates, x1, g2, wgu, wd, n_ctx_rows):
    t_all, d = x1.shape
    tm = MOE_TILE
    return pl.pallas_call(
        functools.partial(_moe_kernel, n_ctx_rows=n_ctx_rows),
        grid=(t_all // tm, N_EXPERTS),
        in_specs=[pl.BlockSpec((tm, d), lambda i, e: (i, 0)),
                  pl.BlockSpec((tm, N_EXPERTS), lambda i, e: (i, 0)),
                  pl.BlockSpec((tm, d), lambda i, e: (i, 0)),
                  pl.BlockSpec((2, d), lambda i, e: (0, 0)),
                  pl.BlockSpec((None, d, 2 * D_EXPERT), lambda i, e: (e, 0, 0)),
                  pl.BlockSpec((None, D_EXPERT, d), lambda i, e: (e, 0, 0))],
        out_specs=pl.BlockSpec((tm, d), lambda i, e: (i, 0)),
        out_shape=jax.ShapeDtypeStruct((t_all, d), F32),
        scratch_shapes=[pltpu.VMEM((tm, d), F32)],
        compiler_params=_cparams(("arbitrary", "arbitrary")),
        name="experts",
    )(h2, gates, x1, g2, wgu, wd)


def _final_kernel(x_ref, w_ref, o_ref):
    x = x_ref[...]
    o_ref[...] = x * lax.rsqrt(jnp.mean(x * x, axis=-1, keepdims=True) + EPS) * w_ref[...]


def _final_call(x_all, w, n_ctx_tiles, seq):
    d = x_all.shape[1]
    tm = ROW_TILE
    return pl.pallas_call(
        _final_kernel,
        grid=(seq // tm,),
        in_specs=[pl.BlockSpec((tm, d), lambda i: (i + n_ctx_tiles, 0)),
                  pl.BlockSpec((1, d), lambda i: (0, 0))],
        out_specs=pl.BlockSpec((tm, d), lambda i: (i, 0)),
        out_shape=jax.ShapeDtypeStruct((seq, d), F32),
        compiler_params=_cparams(("arbitrary",)),
        name="final_norm",
    )(x_all, w)


_SWAP32 = np.concatenate([np.arange(8, 16), np.arange(0, 8), np.arange(24, 32), np.arange(16, 24)])


def _pad_cols(w, width):
    return jnp.pad(w, ((0, 0), (0, width - w.shape[1])))


def _in_weight(w_in):
    sizes = (2 * CONV_W, RET_QK_W, RET_QK_W, RET_W, RET_W, RET_W, MLA_Q_RANK, MLA_KV_RANK, MLA_D_ROPE)
    offs = np.concatenate([[0], np.cumsum(sizes)])
    conv, rq, rk, rv, gf, gb, cq, ckv, kr = [w_in[:, offs[i]:offs[i + 1]] for i in range(len(sizes))]
    swap192 = np.concatenate([h * ROPE_DIM + _SWAP32 for h in range(RET_HEADS)])
    d = w_in.shape[0]

    def place_rope(w):
        return jnp.concatenate([jnp.zeros((d, MLA_D_NOPE), w.dtype), w,
                                jnp.zeros((d, LANES - MLA_D_NOPE - MLA_D_ROPE), w.dtype)], axis=1)

    ext = jnp.concatenate([
        conv, _pad_cols(rq, 256), _pad_cols(rq[:, swap192], 256), _pad_cols(rk, 256), _pad_cols(rk[:, swap192], 256),
        rv, gf, gb, _pad_cols(cq, 256), ckv, place_rope(kr), place_rope(kr[:, _SWAP32])], axis=1)
    assert ext.shape[1] == IN_EXT
    return ext.astype(BF16)


def _uq_weight(w_uq):
    r = w_uq.shape[0]
    w = w_uq.reshape(r, MLA_HEADS, MLA_D_NOPE + MLA_D_ROPE)
    nope, rope = w[..., :MLA_D_NOPE], w[..., MLA_D_NOPE:]
    zpad = jnp.zeros((r, MLA_HEADS, HEAD_PAD - MLA_D_NOPE - MLA_D_ROPE), w.dtype)
    main = jnp.concatenate([nope, rope, zpad], axis=-1).reshape(r, ATT_W)
    part = jnp.concatenate([jnp.zeros_like(nope), rope[..., _SWAP32], zpad], axis=-1).reshape(r, ATT_W)
    return jnp.concatenate([main, part], axis=1).astype(BF16)


def _ukv_weight(w_ukv):
    r = w_ukv.shape[0]
    w = w_ukv.reshape(r, MLA_HEADS, MLA_D_NOPE + MLA_D_V)
    zpad = jnp.zeros((r, MLA_HEADS, HEAD_PAD - MLA_D_NOPE), w.dtype)
    kpart = jnp.concatenate([w[..., :MLA_D_NOPE], zpad], axis=-1).reshape(r, ATT_W)
    vpart = jnp.concatenate([w[..., MLA_D_NOPE:], zpad], axis=-1).reshape(r, ATT_W)
    return jnp.concatenate([kpart, vpart], axis=1).astype(BF16)


def _out_weight(w_out):
    d = w_out.shape[1]
    conv, ret = w_out[:CONV_W], w_out[CONV_W:CONV_W + RET_W]
    att = w_out[CONV_W + RET_W:].reshape(MLA_HEADS, MLA_D_V, d)
    att = jnp.pad(att, ((0, 0), (0, HEAD_PAD - MLA_D_V), (0, 0))).reshape(ATT_W, d)
    return jnp.concatenate([conv, ret, att], axis=0).astype(BF16)


def _rope_tables(n_ctx, seq):
    pos = jnp.arange(seq)
    r = (pos // GRID_W).astype(F32)
    cl = (pos % GRID_W).astype(F32)
    inv = ROPE_BASE ** (-jnp.arange(ROPE_PAIRS, dtype=F32) / ROPE_PAIRS)
    ar, ac = r[:, None] * inv, cl[:, None] * inv
    cs32 = jnp.concatenate([jnp.cos(ar), jnp.cos(ar), jnp.cos(ac), jnp.cos(ac)], axis=1)
    sn32 = jnp.concatenate([-jnp.sin(ar), jnp.sin(ar), -jnp.sin(ac), jnp.sin(ac)], axis=1)
    cs = jnp.concatenate([jnp.ones((n_ctx, ROPE_DIM), F32), cs32], axis=0)
    sn = jnp.concatenate([jnp.zeros((n_ctx, ROPE_DIM), F32), sn32], axis=0)
    return jnp.tile(cs, (1, LANES // ROPE_DIM)), jnp.tile(sn, (1, LANES // ROPE_DIM))


def kernel(x, c, ctx, c_ctx, ada_w, ada_b, norm1_w, norm2_w, w_in, conv_dw, conv_b, conv_ln_w, conv_ln_b,
           mla_q_norm_w, mla_w_uq, mla_kv_norm_w, mla_w_ukv, w_out, router_w, router_bias,
           moe_w_gate, moe_w_up, moe_w_down, final_norm_w):
    batch, seq, d = x.shape
    n_ctx = ctx.shape[1]
    depth = ada_w.shape[0]
    assert batch == 1 and d == D_MODEL
    assert n_ctx % ROW_TILE == 0 and seq % ROW_TILE == 0 and (n_ctx + seq) % MOE_TILE == 0
    t_all = n_ctx + seq
    n_ctx_tiles = n_ctx // ROW_TILE

    cond = jnp.zeros((8, d), F32).at[0].set(c_ctx).at[1].set(c[0])
    mod = _ada_call(cond, ada_w, ada_b)[:, :2, :].reshape(depth, 2, 6, d)
    mod8 = jnp.pad(mod, ((0, 0), (0, 0), (0, 2), (0, 0)))
    cs, sn = _rope_tables(n_ctx, seq)
    ret_tabs = _ret_tables()
    rwt = router_w.T
    rb = router_bias.reshape(N_EXPERTS, 1)

    x_all = jnp.concatenate([ctx[0], x[0]], axis=0)
    for l in range(depth):
        u, rq, rk, rv, gf, gb, q_att, k_att, v_att = _inproj_call(
            x_all, norm1_w[l][None, :], mod8[l][:, :2, :], _in_weight(w_in[l]), cs, sn,
            mla_q_norm_w[l][None, :], _uq_weight(mla_w_uq[l]), mla_kv_norm_w[l][None, :], _ukv_weight(mla_w_ukv[l]),
            n_ctx_tiles)
        conv = _conv_call(u, conv_dw[l], conv_b[l][None, :], conv_ln_w[l][None, :], conv_ln_b[l][None, :],
                          n_ctx_tiles)
        of, ob = _ret_call(rq, rk, rv, gf, gb, ret_tabs, n_ctx // RET_CHUNK)
        att_l = _attn_call(q_att[:, n_ctx:], k_att, v_att, ATT_TQ, ATT_TK)
        att_c = _attn_call(q_att[:, :n_ctx], k_att[:, :n_ctx], v_att[:, :n_ctx], ATT_TQ, n_ctx)
        att = jnp.concatenate([att_c, att_l], axis=0)
        x1, h2, gates_t = _outproj_call(x_all, conv, of, ob, att, _out_weight(w_out[l]), mod8[l],
                                        norm2_w[l][None, :], rwt, rb, n_ctx_tiles)
        wgu = jnp.concatenate([moe_w_gate[l], moe_w_up[l]], axis=-1).astype(BF16)
        x_all = _moe_call(h2, gates_t.T, x1, mod[l, :, 5, :], wgu, moe_w_down[l].astype(BF16), n_ctx)
    out = _final_call(x_all, final_norm_w[None, :], n_ctx_tiles, seq)
    return out[None]
```

```python
import functools
import math

import numpy as np
import jax
import jax.numpy as jnp
from jax import lax
from jax.experimental import pallas as pl
from jax.experimental.pallas import tpu as pltpu

F32 = jnp.float32
BF16 = jnp.bfloat16

D_MODEL = 1024
GRID_W = 64
CONV_W = 256
CONV_K = 31
RET_HEADS = 6
RET_DK = 32
RET_DV = 64
RET_QK_W = RET_HEADS * RET_DK
RET_W = RET_HEADS * RET_DV
RET_CHUNK = 128
RET_GN_EPS = 1e-5
MLA_HEADS = 6
MLA_Q_RANK = 192
MLA_KV_RANK = 128
MLA_D_NOPE = 64
MLA_D_ROPE = 32
MLA_D_V = 64
MLA_SCALE = (MLA_D_NOPE + MLA_D_ROPE) ** -0.5
ROPE_DIM = 32
ROPE_PAIRS = ROPE_DIM // 4
ROPE_BASE = 10000.0
N_EXPERTS = 16
N_GROUPS = 4
EXPERTS_PER_GROUP = N_EXPERTS // N_GROUPS
D_EXPERT = 256
EPS = 1e-6

LANES = 128
HEAD_PAD = LANES
ATT_W = MLA_HEADS * HEAD_PAD
LOG2E = math.log2(math.e)

C_CONV = 0
C_RQ = 512
C_RQS = 768
C_RK = 1024
C_RKS = 1280
C_RV = 1536
C_RGF = 1920
C_RGB = 2304
C_CQ = 2688
C_CKV = 2944
C_KR = 3072
C_KRS = 3200
IN_EXT = 3328

ROW_TILE = 256
MOE_TILE = 640
ATT_TQ = 1024
ATT_TK = 1280
VMEM_LIMIT = 48 * 1024 * 1024


def _cparams(sem):
    return pltpu.CompilerParams(dimension_semantics=sem, vmem_limit_bytes=VMEM_LIMIT)


def _silu(x):
    return x * jax.nn.sigmoid(x)


def _ada_kernel(c_ref, w_ref, b_ref, o_ref):
    s = _silu(c_ref[...])
    o_ref[...] = jnp.dot(s, w_ref[...], preferred_element_type=F32,
                         precision=lax.Precision.HIGHEST) + b_ref[...]


def _ada_call(cond, ada_w, ada_b):
    depth, d, n = ada_w.shape
    tn = 1536
    return pl.pallas_call(
        _ada_kernel,
        grid=(depth, n // tn),
        in_specs=[pl.BlockSpec((8, d), lambda l, j: (0, 0)),
                  pl.BlockSpec((None, d, tn), lambda l, j: (l, 0, j)),
                  pl.BlockSpec((None, 1, tn), lambda l, j: (l, 0, j))],
        out_specs=pl.BlockSpec((None, 8, tn), lambda l, j: (l, 0, j)),
        out_shape=jax.ShapeDtypeStruct((depth, 8, n), F32),
        compiler_params=_cparams(("arbitrary", "arbitrary")),
        name="adaln",
    )(cond, ada_w, ada_b.reshape(depth, 1, n))


def _inproj_kernel(x_ref, nw_ref, mod_ref, w_ref, cs_ref, sn_ref, qnw_ref, wuq_ref, kvnw_ref, wukv_ref,
                   u_ref, rq_ref, rk_ref, rv_ref, gf_ref, gb_ref, q_ref, k_ref, v_ref):
    x = x_ref[...]
    y = x * lax.rsqrt(jnp.mean(x * x, axis=-1, keepdims=True) + EPS) * nw_ref[...]
    h = (y * (1.0 + mod_ref[1:2, :]) + mod_ref[0:1, :]).astype(BF16)

    def proj(c0, width):
        return jnp.dot(h, w_ref[:, c0:c0 + width], preferred_element_type=F32)

    cs = cs_ref[...]
    sn = sn_ref[...]
    lane = lax.broadcasted_iota(jnp.int32, (1, LANES), 1)
    rope_lanes = (lane >= MLA_D_NOPE) & (lane < MLA_D_NOPE + MLA_D_ROPE)
    csq = jnp.where(rope_lanes, cs, 1.0)
    snq = jnp.where(rope_lanes, sn, 0.0)

    ag = proj(C_CONV, 2 * CONV_W)
    u_ref[...] = ag[:, :CONV_W] * jax.nn.sigmoid(ag[:, CONV_W:])

    cs192 = jnp.concatenate([cs, cs[:, :RET_QK_W - LANES]], axis=1)
    sn192 = jnp.concatenate([sn, sn[:, :RET_QK_W - LANES]], axis=1)
    rq = proj(C_RQ, 256)[:, :RET_QK_W] * cs192 + proj(C_RQS, 256)[:, :RET_QK_W] * sn192
    rq_ref[...] = rq.astype(BF16)
    rk = proj(C_RK, 256)[:, :RET_QK_W] * cs192 + proj(C_RKS, 256)[:, :RET_QK_W] * sn192
    rk_ref[...] = (rk * (RET_DK ** -0.5)).astype(BF16)
    rv_ref[...] = proj(C_RV, RET_W).astype(BF16)
    gf_ref[...] = _silu(proj(C_RGF, RET_W))
    gb_ref[...] = _silu(proj(C_RGB, RET_W))

    cq = proj(C_CQ, 256)[:, :MLA_Q_RANK]
    cqn = (cq * lax.rsqrt(jnp.mean(cq * cq, axis=-1, keepdims=True) + EPS) * qnw_ref[...]).astype(BF16)
    qa = jnp.dot(cqn, wuq_ref[:, :ATT_W], preferred_element_type=F32)
    qb = jnp.dot(cqn, wuq_ref[:, ATT_W:], preferred_element_type=F32)
    ckv = proj(C_CKV, MLA_KV_RANK)
    ckvn = (ckv * lax.rsqrt(jnp.mean(ckv * ckv, axis=-1, keepdims=True) + EPS) * kvnw_ref[...]).astype(BF16)
    ka = jnp.dot(ckvn, wukv_ref[:, :ATT_W], preferred_element_type=F32)
    va = jnp.dot(ckvn, wukv_ref[:, ATT_W:], preferred_element_type=F32)
    kr = proj(C_KR, LANES) * csq + proj(C_KRS, LANES) * snq
    ones_lane = (lane == MLA_D_V).astype(F32)
    for hd in range(MLA_HEADS):
        sl = slice(hd * HEAD_PAD, (hd + 1) * HEAD_PAD)
        q_ref[hd] = ((qa[:, sl] * csq + qb[:, sl] * snq) * (MLA_SCALE * LOG2E)).astype(BF16)
        k_ref[hd] = (ka[:, sl] + kr).astype(BF16)
        v_ref[hd] = (va[:, sl] + ones_lane).astype(BF16)


def _inproj_call(x_all, nw, mod, w_ext, cs, sn, qnw, wuq, kvnw, wukv, n_ctx_tiles):
    t_all, d = x_all.shape
    tm = ROW_TILE
    row = lambda i: (i, 0)
    const2 = lambda i: (0, 0)
    head = lambda i: (0, i, 0)
    cls = lambda i: (jnp.where(i < n_ctx_tiles, 0, 1), 0, 0)
    out_shape = (
        jax.ShapeDtypeStruct((t_all, CONV_W), F32),
        jax.ShapeDtypeStruct((t_all, RET_QK_W), BF16),
        jax.ShapeDtypeStruct((t_all, RET_QK_W), BF16),
        jax.ShapeDtypeStruct((t_all, RET_W), BF16),
        jax.ShapeDtypeStruct((t_all, RET_W), F32),
        jax.ShapeDtypeStruct((t_all, RET_W), F32),
        jax.ShapeDtypeStruct((MLA_HEADS, t_all, HEAD_PAD), BF16),
        jax.ShapeDtypeStruct((MLA_HEADS, t_all, HEAD_PAD), BF16),
        jax.ShapeDtypeStruct((MLA_HEADS, t_all, HEAD_PAD), BF16),
    )
    out_specs = (
        pl.BlockSpec((tm, CONV_W), row),
        pl.BlockSpec((tm, RET_QK_W), row),
        pl.BlockSpec((tm, RET_QK_W), row),
        pl.BlockSpec((tm, RET_W), row),
        pl.BlockSpec((tm, RET_W), row),
        pl.BlockSpec((tm, RET_W), row),
        pl.BlockSpec((MLA_HEADS, tm, HEAD_PAD), head),
        pl.BlockSpec((MLA_HEADS, tm, HEAD_PAD), head),
        pl.BlockSpec((MLA_HEADS, tm, HEAD_PAD), head),
    )
    return pl.pallas_call(
        _inproj_kernel,
        grid=(t_all // tm,),
        in_specs=[pl.BlockSpec((tm, d), row),
                  pl.BlockSpec((1, d), const2),
                  pl.BlockSpec((None, 2, d), cls),
                  pl.BlockSpec((d, IN_EXT), const2),
                  pl.BlockSpec((tm, LANES), row),
                  pl.BlockSpec((tm, LANES), row),
                  pl.BlockSpec((1, MLA_Q_RANK), const2),
                  pl.BlockSpec((MLA_Q_RANK, 2 * ATT_W), const2),
                  pl.BlockSpec((1, MLA_KV_RANK), const2),
                  pl.BlockSpec((MLA_KV_RANK, 2 * ATT_W), const2)],
        out_specs=out_specs,
        out_shape=out_shape,
        compiler_params=_cparams(("arbitrary",)),
        name="inproj",
    )(x_all, nw, mod, w_ext, cs, sn, qnw, wuq, kvnw, wukv)


CONV_HALO = 16
CONV_SUB = 64


def _conv_kernel(prev_ref, cur_ref, next_ref, dw_ref, b_ref, lnw_ref, lnb_ref, o_ref, ext_ref, *, n_ctx_tiles):
    i = pl.program_id(0)
    n = pl.num_programs(0)
    tm = cur_ref.shape[0]
    seq_start = (i == 0) | (i == n_ctx_tiles)
    seq_end = (i == n_ctx_tiles - 1) | (i == n - 1)
    ext_ref[0:CONV_HALO, :] = jnp.where(seq_start, 0.0, prev_ref[...])
    ext_ref[CONV_HALO:CONV_HALO + tm, :] = cur_ref[...]
    ext_ref[CONV_HALO + tm:, :] = jnp.where(seq_end, 0.0, next_ref[...])
    base = CONV_HALO - CONV_K // 2
    for r in range(tm // CONV_SUB):
        acc = jnp.zeros((CONV_SUB, CONV_W), F32) + b_ref[...]
        for k in range(CONV_K):
            acc = acc + ext_ref[pl.ds(r * CONV_SUB + base + k, CONV_SUB), :] * dw_ref[k:k + 1, :]
        mu = jnp.mean(acc, axis=-1, keepdims=True)
        dlt = acc - mu
        var = jnp.mean(dlt * dlt, axis=-1, keepdims=True)
        y = dlt * lax.rsqrt(var + EPS) * lnw_ref[...] + lnb_ref[...]
        o_ref[r * CONV_SUB:(r + 1) * CONV_SUB, :] = _silu(y).astype(BF16)


def _conv_call(u, dw, b, lnw, lnb, n_ctx_tiles):
    t_all = u.shape[0]
    tm = ROW_TILE
    hpt = tm // CONV_HALO
    n_halo = t_all // CONV_HALO
    const2 = lambda i: (0, 0)
    return pl.pallas_call(
        functools.partial(_conv_kernel, n_ctx_tiles=n_ctx_tiles),
        grid=(t_all // tm,),
        in_specs=[pl.BlockSpec((CONV_HALO, CONV_W), lambda i: (jnp.maximum(i * hpt - 1, 0), 0)),
                  pl.BlockSpec((tm, CONV_W), lambda i: (i, 0)),
                  pl.BlockSpec((CONV_HALO, CONV_W), lambda i: (jnp.minimum((i + 1) * hpt, n_halo - 1), 0)),
                  pl.BlockSpec((CONV_K, CONV_W), const2),
                  pl.BlockSpec((1, CONV_W), const2),
                  pl.BlockSpec((1, CONV_W), const2),
                  pl.BlockSpec((1, CONV_W), const2)],
        out_specs=pl.BlockSpec((tm, CONV_W), lambda i: (i, 0)),
        out_shape=jax.ShapeDtypeStruct((t_all, CONV_W), BF16),
        scratch_shapes=[pltpu.VMEM((tm + 2 * CONV_HALO, CONV_W), F32)],
        compiler_params=_cparams(("arbitrary",)),
        name="conv",
    )(u, u, u, dw, b, lnw, lnb)


def _split_dot(x, a):
    hi = x.astype(BF16)
    lo = (x - hi.astype(F32)).astype(BF16)
    return (jnp.dot(hi, a, preferred_element_type=F32) + jnp.dot(lo, a, preferred_element_type=F32))


def _ret_direction(q, k, v, gate, r_ref, dmask, xi, zeta, gchunk, bdmask, avg, o_ref):
    lane_k = lax.broadcasted_iota(jnp.int32, (1, RET_QK_W), 1) // RET_DK
    lane_v = lax.broadcasted_iota(jnp.int32, (1, RET_W), 1) // RET_DV
    zero_k = jnp.zeros_like(k)
    zero_v = jnp.zeros_like(v)
    k_bd = jnp.concatenate([jnp.where(lane_k == hd, k, zero_k) for hd in range(RET_HEADS)], axis=0)
    v_bd = jnp.concatenate([jnp.where(lane_v == hd, v, zero_v) for hd in range(RET_HEADS)], axis=0)
    s = lax.dot_general(q, k_bd, (((1,), (1,)), ((), ())), preferred_element_type=F32)
    s = (s * dmask).astype(BF16)
    inner = jnp.dot(s, v_bd, preferred_element_type=F32)
    r = r_ref[...]
    cross = jnp.dot(q, r.astype(BF16), preferred_element_type=F32) * xi
    o = inner + cross
    kz = (k.astype(F32) * zeta).astype(BF16)
    ds = lax.dot_general(kz, v, (((0,), (0,)), ((), ())), preferred_element_type=F32)
    r_ref[...] = gchunk * r + ds * bdmask
    mu = _split_dot(o, avg)
    dlt = o - mu
    var = _split_dot(dlt * dlt, avg)
    o_ref[...] = (gate * (dlt * lax.rsqrt(var + RET_GN_EPS))).astype(BF16)


def _ret_kernel(qf_ref, kf_ref, vf_ref, gf_ref, qb_ref, kb_ref, vb_ref, gb_ref,
                dmf_ref, dmb_ref, xif_ref, xib_ref, ztf_ref, ztb_ref, gcf_ref, gcb_ref, bdm_ref, avg_ref,
                of_ref, ob_ref, rf_ref, rb_ref):
    @pl.when(pl.program_id(0) == 0)
    def _():
        rf_ref[...] = jnp.zeros_like(rf_ref)
        rb_ref[...] = jnp.zeros_like(rb_ref)

    bdm = bdm_ref[...]
    avg = avg_ref[...]
    _ret_direction(qf_ref[...], kf_ref[...], vf_ref[...], gf_ref[...], rf_ref, dmf_ref[...], xif_ref[...],
                   ztf_ref[...], gcf_ref[...], bdm, avg, of_ref)
    _ret_direction(qb_ref[...], kb_ref[...], vb_ref[...], gb_ref[...], rb_ref, dmb_ref[...], xib_ref[...],
                   ztb_ref[...], gcb_ref[...], bdm, avg, ob_ref)


def _ret_tables():
    c = RET_CHUNK
    gamma_f = 1.0 - 2.0 ** (-5.0 - jnp.arange(RET_HEADS, dtype=F32))
    gamma_b = gamma_f[::-1]
    idx = jnp.arange(c, dtype=F32)
    diff = idx[:, None] - idx[None, :]

    def tables(gamma, reverse):
        lg = jnp.log(gamma)
        d = -diff if reverse else diff
        dm = jnp.where(d[None] >= 0, jnp.exp(jnp.maximum(d, 0.0)[None] * lg[:, None, None]), 0.0)
        dm = jnp.transpose(dm, (1, 0, 2)).reshape(c, RET_HEADS * c)
        xi_e = (c - idx) if reverse else (idx + 1.0)
        zt_e = idx if reverse else (c - 1.0 - idx)
        xi = jnp.repeat(jnp.exp(xi_e[:, None] * lg[None, :]), RET_DV, axis=1)
        zt = jnp.repeat(jnp.exp(zt_e[:, None] * lg[None, :]), RET_DK, axis=1)
        gc = jnp.repeat(jnp.exp(c * lg), RET_DV)[None, :]
        return dm, xi, zt, gc

    dmf, xif, ztf, gcf = tables(gamma_f, False)
    dmb, xib, ztb, gcb = tables(gamma_b, True)
    hk = jnp.arange(RET_QK_W) // RET_DK
    hv = jnp.arange(RET_W) // RET_DV
    bdm = (hk[:, None] == hv[None, :]).astype(F32)
    avg = ((hv[:, None] == hv[None, :]).astype(F32) / RET_DV).astype(BF16)
    return (dmf, dmb, xif, xib, ztf, ztb, gcf, gcb, bdm, avg)


def _ret_call(rq, rk, rv, gf, gb, tabs, n_ctx_chunks):
    t_all = rq.shape[0]
    c = RET_CHUNK
    n = t_all // c

    def fwd(i):
        return (i, 0)

    def bwd(i):
        return (jnp.where(i < n_ctx_chunks, n_ctx_chunks - 1 - i, n - 1 + n_ctx_chunks - i), 0)

    const2 = lambda i: (0, 0)
    tab_specs = [pl.BlockSpec(t.shape, const2) for t in tabs]
    return pl.pallas_call(
        _ret_kernel,
        grid=(n,),
        in_specs=[pl.BlockSpec((c, RET_QK_W), fwd), pl.BlockSpec((c, RET_QK_W), fwd),
                  pl.BlockSpec((c, RET_W), fwd), pl.BlockSpec((c, RET_W), fwd),
                  pl.BlockSpec((c, RET_QK_W), bwd), pl.BlockSpec((c, RET_QK_W), bwd),
                  pl.BlockSpec((c, RET_W), bwd), pl.BlockSpec((c, RET_W), bwd)] + tab_specs,
        out_specs=(pl.BlockSpec((c, RET_W), fwd), pl.BlockSpec((c, RET_W), bwd)),
        out_shape=(jax.ShapeDtypeStruct((t_all, RET_W), BF16), jax.ShapeDtypeStruct((t_all, RET_W), BF16)),
        scratch_shapes=[pltpu.VMEM((RET_QK_W, RET_W), F32), pltpu.VMEM((RET_QK_W, RET_W), F32)],
        compiler_params=_cparams(("arbitrary",)),
        name="retention",
    )(rq, rk, rv, gf, rq, rk, rv, gb, *tabs)


ATT_SUB = 128


def _attn_kernel(q_ref, k_ref, vt_ref, o_ref, s0_scr, s1_scr, p0_scr, p1_scr, acc_scr):
    q = q_ref[...]
    tq = q.shape[0]
    nkc, tk, _ = k_ref.shape
    nsub = tk // ATT_SUB

    s_bufs = (s0_scr, s1_scr)
    p_bufs = (p0_scr, p1_scr)

    def pv(c, par, alpha):
        part = jnp.dot(vt_ref[c], p_bufs[par][...], preferred_element_type=F32)
        acc_scr[...] = alpha * acc_scr[...] + part

    def step(c, par, m_old, m_blk, alpha_prev, with_scores, with_pv):
        if with_pv:
            pv(c - 1, 1 - par, alpha_prev)
        m_new = jnp.maximum(m_old, m_blk)
        alpha = jnp.exp2(m_old - m_new)
        mx = jnp.full((8, tq), -jnp.inf, F32)
        for j in range(nsub):
            rows = pl.ds(j * ATT_SUB, ATT_SUB)
            p_bufs[par][rows, :] = jnp.exp2(s_bufs[par][rows, :] - m_new).astype(BF16)
            if with_scores:
                mx = jnp.maximum(mx, score_rows(c + 1, 1 - par, rows))
        return m_new, jnp.max(mx, axis=0, keepdims=True), alpha

    def score_rows(c, par, rows):
        s = jnp.dot(k_ref[c, rows, :], qt, preferred_element_type=F32)
        s_bufs[par][rows, :] = s
        return jnp.max(s.reshape(ATT_SUB // 8, 8, tq), axis=0)

    def scores(c, par):
        mx = jnp.full((8, tq), -jnp.inf, F32)
        for j in range(nsub):
            mx = jnp.maximum(mx, score_rows(c, par, pl.ds(j * ATT_SUB, ATT_SUB)))
        return jnp.max(mx, axis=0, keepdims=True)

    acc_scr[...] = jnp.zeros_like(acc_scr)
    qt = q.astype(F32).T.astype(BF16)
    m = jnp.full((1, tq), -jnp.inf, F32)
    m_blk = scores(0, 0)
    alpha = jnp.ones((1, tq), F32)
    if nkc > 1:
        m, m_blk, alpha = step(0, 0, m, m_blk, alpha, True, False)
        def body(c, carry):
            return lax.cond(c % 2 == 1,
                            lambda cr: step(c, 1, *cr, True, True),
                            lambda cr: step(c, 0, *cr, True, True), carry)

        m, m_blk, alpha = lax.fori_loop(1, nkc - 1, body, (m, m_blk, alpha))
    last = nkc - 1
    m, _, alpha = step(last, last % 2, m, m_blk, alpha, False, nkc > 1)
    pv(last, last % 2, alpha)
    acc = acc_scr[...]
    out_t = acc / acc[MLA_D_V:MLA_D_V + 1, :]
    o_ref[...] = out_t.T.astype(BF16)


def _attn_call(q, k, v, tq, tk):
    nh, t_q, _ = q.shape
    t_k = k.shape[1]
    nkc = t_k // tk
    k4 = k.reshape(nh, nkc, tk, HEAD_PAD)
    vt4 = jnp.swapaxes(v.reshape(nh, nkc, tk, HEAD_PAD), 2, 3)
    return pl.pallas_call(
        _attn_kernel,
        grid=(nh, t_q // tq),
        in_specs=[pl.BlockSpec((None, tq, HEAD_PAD), lambda h, j: (h, j, 0)),
                  pl.BlockSpec((None, nkc, tk, HEAD_PAD), lambda h, j: (h, 0, 0, 0)),
                  pl.BlockSpec((None, nkc, HEAD_PAD, tk), lambda h, j: (h, 0, 0, 0))],
        out_specs=pl.BlockSpec((tq, HEAD_PAD), lambda h, j: (j, h)),
        out_shape=jax.ShapeDtypeStruct((t_q, nh * HEAD_PAD), BF16),
        scratch_shapes=[pltpu.VMEM((tk, tq), F32), pltpu.VMEM((tk, tq), F32),
                        pltpu.VMEM((tk, tq), BF16), pltpu.VMEM((tk, tq), BF16), pltpu.VMEM((HEAD_PAD, tq), F32)],
        compiler_params=_cparams(("arbitrary", "arbitrary")),
        name="attention",
    )(q, k4, vt4)


def _top2_sum(a, b, c, d):
    hi1, lo1 = jnp.maximum(a, b), jnp.minimum(a, b)
    hi2, lo2 = jnp.maximum(c, d), jnp.minimum(c, d)
    return jnp.maximum(hi1, hi2) + jnp.maximum(jnp.minimum(hi1, hi2), jnp.maximum(lo1, lo2))


def _gates_t(aff, sel, gt_ref):
    rows = [sel[e:e + 1, :] for e in range(N_EXPERTS)]
    g_score = [_top2_sum(*rows[g * EXPERTS_PER_GROUP:(g + 1) * EXPERTS_PER_GROUP]) for g in range(N_GROUPS)]
    best = g_score[0]
    best_g = jnp.zeros_like(best, dtype=jnp.int32)
    for g in range(1, N_GROUPS):
        better = g_score[g] > best
        best = jnp.where(better, g_score[g], best)
        best_g = jnp.where(better, g, best_g)
    picked = []
    for e in range(N_EXPERTS):
        g = e // EXPERTS_PER_GROUP
        rank = jnp.zeros_like(best_g)
        for o in range(g * EXPERTS_PER_GROUP, (g + 1) * EXPERTS_PER_GROUP):
            if o == e:
                continue
            ahead = (rows[o] >= rows[e]) if o < e else (rows[o] > rows[e])
            rank = rank + jnp.where(ahead, 1, 0)
        picked.append(jnp.where(best_g == g, rank, 2) < 2)
    w = [jnp.where(picked[e], aff[e:e + 1, :], 0.0) for e in range(N_EXPERTS)]
    total = w[0]
    for e in range(1, N_EXPERTS):
        total = total + w[e]
    for e in range(N_EXPERTS):
        gt_ref[e:e + 1, :] = w[e] / total


def _outproj_kernel(x_ref, conv_ref, of_ref, ob_ref, att_ref, w_ref, mod_ref, nw_ref, rwt_ref, rb_ref,
                    x1_ref, h2_ref, gt_ref):
    ret = (of_ref[...].astype(F32) + ob_ref[...].astype(F32)).astype(BF16)
    o = jnp.dot(conv_ref[...], w_ref[0:CONV_W, :], preferred_element_type=F32)
    o = o + jnp.dot(ret, w_ref[CONV_W:CONV_W + RET_W, :], preferred_element_type=F32)
    o = o + jnp.dot(att_ref[...], w_ref[CONV_W + RET_W:, :], preferred_element_type=F32)
    x1 = x_ref[...] + mod_ref[2:3, :] * o
    x1_ref[...] = x1
    y = x1 * lax.rsqrt(jnp.mean(x1 * x1, axis=-1, keepdims=True) + EPS) * nw_ref[...]
    h2 = y * (1.0 + mod_ref[4:5, :]) + mod_ref[3:4, :]
    h2_ref[...] = h2.astype(BF16)
    logits = lax.dot_general(rwt_ref[...], h2, (((1,), (1,)), ((), ())), preferred_element_type=F32,
                             precision=lax.Precision.HIGHEST)
    aff = jax.nn.sigmoid(logits)
    _gates_t(aff, aff + rb_ref[...], gt_ref)


def _outproj_call(x_all, conv, of, ob, att, w_ext, mod, nw, rwt, rb, n_ctx_tiles):
    t_all, d = x_all.shape
    tm = ROW_TILE
    row = lambda i: (i, 0)
    const2 = lambda i: (0, 0)
    cls = lambda i: (jnp.where(i < n_ctx_tiles, 0, 1), 0, 0)
    return pl.pallas_call(
        _outproj_kernel,
        grid=(t_all // tm,),
        in_specs=[pl.BlockSpec((tm, d), row),
                  pl.BlockSpec((tm, CONV_W), row),
                  pl.BlockSpec((tm, RET_W), row),
                  pl.BlockSpec((tm, RET_W), row),
                  pl.BlockSpec((tm, ATT_W), row),
                  pl.BlockSpec(w_ext.shape, const2),
                  pl.BlockSpec((None, 8, d), cls),
                  pl.BlockSpec((1, d), const2),
                  pl.BlockSpec((N_EXPERTS, d), const2),
                  pl.BlockSpec((N_EXPERTS, 1), const2)],
        out_specs=(pl.BlockSpec((tm, d), row), pl.BlockSpec((tm, d), row),
                   pl.BlockSpec((N_EXPERTS, tm), lambda i: (0, i))),
        out_shape=(jax.ShapeDtypeStruct((t_all, d), F32), jax.ShapeDtypeStruct((t_all, d), BF16),
                   jax.ShapeDtypeStruct((N_EXPERTS, t_all), F32)),
        compiler_params=_cparams(("arbitrary",)),
        name="outproj",
    )(x_all, conv, of, ob, att, w_ext, mod, nw, rwt, rb)


def _moe_kernel(h_ref, g_ref, x1_ref, g2_ref, wgu_ref, wd_ref, o_ref, acc_ref, *, n_ctx_rows):
    e = pl.program_id(1)

    @pl.when(e == 0)
    def _():
        acc_ref[...] = jnp.zeros_like(acc_ref)

    gu = jnp.dot(h_ref[...], wgu_ref[...], preferred_element_type=F32)
    he = _silu(gu[:, :D_EXPERT]) * gu[:, D_EXPERT:]
    pick = (lax.broadcasted_iota(jnp.int32, (N_EXPERTS, LANES), 0) == e).astype(BF16)
    gate = _split_dot(g_ref[...], pick)
    heg = (he * jnp.concatenate([gate, gate], axis=1)).astype(BF16)
    acc_ref[...] += jnp.dot(heg, wd_ref[...], preferred_element_type=F32)

    @pl.when(e == pl.num_programs(1) - 1)
    def _():
        tm = acc_ref.shape[0]
        row = pl.program_id(0) * tm + lax.broadcasted_iota(jnp.int32, (tm, 1), 0)
        g2 = jnp.where(row >= n_ctx_rows, g2_ref[1:2, :], g2_ref[0:1, :])
        o_ref[...] = x1_ref[...] + g2 * acc_ref[...]


def _moe_call(h2, gates, x1, g2, wgu, wd, n_ctx_rows):
    t_all, d = x1.shape
    tm = MOE_TILE
    return pl.pallas_call(
        functools.partial(_moe_kernel, n_ctx_rows=n_ctx_rows),
        grid=(t_all // tm, N_EXPERTS),
        in_specs=[pl.BlockSpec((tm, d), lambda i, e: (i, 0)),
                  pl.BlockSpec((tm, N_EXPERTS), lambda i, e: (i, 0)),
                  pl.BlockSpec((tm, d), lambda i, e: (i, 0)),
                  pl.BlockSpec((2, d), lambda i, e: (0, 0)),
                  pl.BlockSpec((None, d, 2 * D_EXPERT), lambda i, e: (e, 0, 0)),
                  pl.BlockSpec((None, D_EXPERT, d), lambda i, e: (e, 0, 0))],
        out_specs=pl.BlockSpec((tm, d), lambda i, e: (i, 0)),
        out_shape=jax.ShapeDtypeStruct((t_all, d), F32),
        scratch_shapes=[pltpu.VMEM((tm, d), F32)],
        compiler_params=_cparams(("arbitrary", "arbitrary")),
        name="experts",
    )(h2, gates, x1, g2, wgu, wd)


def _final_kernel(x_ref, w_ref, o_ref):
    x = x_ref[...]
    o_ref[...] = x * lax.rsqrt(jnp.mean(x * x, axis=-1, keepdims=True) + EPS) * w_ref[...]


def _final_call(x_all, w, n_ctx_tiles, seq):
    d = x_all.shape[1]
    tm = ROW_TILE
    return pl.pallas_call(
        _final_kernel,
        grid=(seq // tm,),
        in_specs=[pl.BlockSpec((tm, d), lambda i: (i + n_ctx_tiles, 0)),
                  pl.BlockSpec((1, d), lambda i: (0, 0))],
        out_specs=pl.BlockSpec((tm, d), lambda i: (i, 0)),
        out_shape=jax.ShapeDtypeStruct((seq, d), F32),
        compiler_params=_cparams(("arbitrary",)),
        name="final_norm",
    )(x_all, w)


_SWAP32 = np.concatenate([np.arange(8, 16), np.arange(0, 8), np.arange(24, 32), np.arange(16, 24)])


def _pad_cols(w, width):
    return jnp.pad(w, ((0, 0), (0, width - w.shape[1])))


def _in_weight(w_in):
    sizes = (2 * CONV_W, RET_QK_W, RET_QK_W, RET_W, RET_W, RET_W, MLA_Q_RANK, MLA_KV_RANK, MLA_D_ROPE)
    offs = np.concatenate([[0], np.cumsum(sizes)])
    conv, rq, rk, rv, gf, gb, cq, ckv, kr = [w_in[:, offs[i]:offs[i + 1]] for i in range(len(sizes))]
    swap192 = np.concatenate([h * ROPE_DIM + _SWAP32 for h in range(RET_HEADS)])
    d = w_in.shape[0]

    def place_rope(w):
        return jnp.concatenate([jnp.zeros((d, MLA_D_NOPE), w.dtype), w,
                                jnp.zeros((d, LANES - MLA_D_NOPE - MLA_D_ROPE), w.dtype)], axis=1)

    ext = jnp.concatenate([
        conv, _pad_cols(rq, 256), _pad_cols(rq[:, swap192], 256), _pad_cols(rk, 256), _pad_cols(rk[:, swap192], 256),
        rv, gf, gb, _pad_cols(cq, 256), ckv, place_rope(kr), place_rope(kr[:, _SWAP32])], axis=1)
    assert ext.shape[1] == IN_EXT
    return ext.astype(BF16)


def _uq_weight(w_uq):
    r = w_uq.shape[0]
    w = w_uq.reshape(r, MLA_HEADS, MLA_D_NOPE + MLA_D_ROPE)
    nope, rope = w[..., :MLA_D_NOPE], w[..., MLA_D_NOPE:]
    zpad = jnp.zeros((r, MLA_HEADS, HEAD_PAD - MLA_D_NOPE - MLA_D_ROPE), w.dtype)
    main = jnp.concatenate([nope, rope, zpad], axis=-1).reshape(r, ATT_W)
    part = jnp.concatenate([jnp.zeros_like(nope), rope[..., _SWAP32], zpad], axis=-1).reshape(r, ATT_W)
    return jnp.concatenate([main, part], axis=1).astype(BF16)


def _ukv_weight(w_ukv):
    r = w_ukv.shape[0]
    w = w_ukv.reshape(r, MLA_HEADS, MLA_D_NOPE + MLA_D_V)
    zpad = jnp.zeros((r, MLA_HEADS, HEAD_PAD - MLA_D_NOPE), w.dtype)
    kpart = jnp.concatenate([w[..., :MLA_D_NOPE], zpad], axis=-1).reshape(r, ATT_W)
    vpart = jnp.concatenate([w[..., MLA_D_NOPE:], zpad], axis=-1).reshape(r, ATT_W)
    return jnp.concatenate([kpart, vpart], axis=1).astype(BF16)


def _out_weight(w_out):
    d = w_out.shape[1]
    conv, ret = w_out[:CONV_W], w_out[CONV_W:CONV_W + RET_W]
    att = w_out[CONV_W + RET_W:].reshape(MLA_HEADS, MLA_D_V, d)
    att = jnp.pad(att, ((0, 0), (0, HEAD_PAD - MLA_D_V), (0, 0))).reshape(ATT_W, d)
    return jnp.concatenate([conv, ret, att], axis=0).astype(BF16)


def _rope_tables(n_ctx, seq):
    pos = jnp.arange(seq)
    r = (pos // GRID_W).astype(F32)
    cl = (pos % GRID_W).astype(F32)
    inv = ROPE_BASE ** (-jnp.arange(ROPE_PAIRS, dtype=F32) / ROPE_PAIRS)
    ar, ac = r[:, None] * inv, cl[:, None] * inv
    cs32 = jnp.concatenate([jnp.cos(ar), jnp.cos(ar), jnp.cos(ac), jnp.cos(ac)], axis=1)
    sn32 = jnp.concatenate([-jnp.sin(ar), jnp.sin(ar), -jnp.sin(ac), jnp.sin(ac)], axis=1)
    cs = jnp.concatenate([jnp.ones((n_ctx, ROPE_DIM), F32), cs32], axis=0)
    sn = jnp.concatenate([jnp.zeros((n_ctx, ROPE_DIM), F32), sn32], axis=0)
    return jnp.tile(cs, (1, LANES // ROPE_DIM)), jnp.tile(sn, (1, LANES // ROPE_DIM))


def kernel(x, c, ctx, c_ctx, ada_w, ada_b, norm1_w, norm2_w, w_in, conv_dw, conv_b, conv_ln_w, conv_ln_b,
           mla_q_norm_w, mla_w_uq, mla_kv_norm_w, mla_w_ukv, w_out, router_w, router_bias,
           moe_w_gate, moe_w_up, moe_w_down, final_norm_w):
    batch, seq, d = x.shape
    n_ctx = ctx.shape[1]
    depth = ada_w.shape[0]
    assert batch == 1 and d == D_MODEL
    assert n_ctx % ROW_TILE == 0 and seq % ROW_TILE == 0 and (n_ctx + seq) % MOE_TILE == 0
    assert seq % ATT_TQ == 0 and (n_ctx + seq) % ATT_TK == 0
    t_all = n_ctx + seq
    n_ctx_tiles = n_ctx // ROW_TILE

    cond = jnp.zeros((8, d), F32).at[0].set(c_ctx).at[1].set(c[0])
    mod = _ada_call(cond, ada_w, ada_b)[:, :2, :].reshape(depth, 2, 6, d)
    mod8 = jnp.pad(mod, ((0, 0), (0, 0), (0, 2), (0, 0)))
    cs, sn = _rope_tables(n_ctx, seq)
    ret_tabs = _ret_tables()
    rwt = router_w.T
    rb = router_bias.reshape(N_EXPERTS, 1)

    x_all = jnp.concatenate([ctx[0], x[0]], axis=0)
    for l in range(depth):
        u, rq, rk, rv, gf, gb, q_att, k_att, v_att = _inproj_call(
            x_all, norm1_w[l][None, :], mod8[l][:, :2, :], _in_weight(w_in[l]), cs, sn,
            mla_q_norm_w[l][None, :], _uq_weight(mla_w_uq[l]), mla_kv_norm_w[l][None, :], _ukv_weight(mla_w_ukv[l]),
            n_ctx_tiles)
        conv = _conv_call(u, conv_dw[l], conv_b[l][None, :], conv_ln_w[l][None, :], conv_ln_b[l][None, :],
                          n_ctx_tiles)
        of, ob = _ret_call(rq, rk, rv, gf, gb, ret_tabs, n_ctx // RET_CHUNK)
        att_l = _attn_call(q_att[:, n_ctx:], k_att, v_att, ATT_TQ, ATT_TK)
        att_c = _attn_call(q_att[:, :n_ctx], k_att[:, :n_ctx], v_att[:, :n_ctx], min(ATT_TQ, n_ctx), n_ctx)
        att = jnp.concatenate([att_c, att_l], axis=0)
        x1, h2, gates_t = _outproj_call(x_all, conv, of, ob, att, _out_weight(w_out[l]), mod8[l],
                                        norm2_w[l][None, :], rwt, rb, n_ctx_tiles)
        wgu = jnp.concatenate([moe_w_gate[l], moe_w_up[l]], axis=-1).astype(BF16)
        x_all = _moe_call(h2, gates_t.T, x1, mod[l, :, 5, :], wgu, moe_w_down[l].astype(BF16), n_ctx)
    out = _final_call(x_all, final_norm_w[None, :], n_ctx_tiles, seq)
    return out[None]
```

```python
import functools
import math

import numpy as np
import jax
import jax.numpy as jnp
from jax import lax
from jax.experimental import pallas as pl
from jax.experimental.pallas import tpu as pltpu

F32 = jnp.float32
BF16 = jnp.bfloat16

D_MODEL = 1024
GRID_W = 64
CONV_W = 256
CONV_K = 31
RET_HEADS = 6
RET_DK = 32
RET_DV = 64
RET_QK_W = RET_HEADS * RET_DK
RET_W = RET_HEADS * RET_DV
RET_CHUNK = 128
RET_GN_EPS = 1e-5
MLA_HEADS = 6
MLA_Q_RANK = 192
MLA_KV_RANK = 128
MLA_D_NOPE = 64
MLA_D_ROPE = 32
MLA_D_V = 64
MLA_SCALE = (MLA_D_NOPE + MLA_D_ROPE) ** -0.5
ROPE_DIM = 32
ROPE_PAIRS = ROPE_DIM // 4
ROPE_BASE = 10000.0
N_EXPERTS = 16
N_GROUPS = 4
EXPERTS_PER_GROUP = N_EXPERTS // N_GROUPS
D_EXPERT = 256
EPS = 1e-6

LANES = 128
HEAD_PAD = LANES
ATT_W = MLA_HEADS * HEAD_PAD
LOG2E = math.log2(math.e)

C_CONV = 0
C_RQ = 512
C_RQS = 768
C_RK = 1024
C_RKS = 1280
C_RV = 1536
C_RGF = 1920
C_RGB = 2304
C_CQ = 2688
C_CKV = 2944
C_KR = 3072
C_KRS = 3200
IN_EXT = 3328

ROW_TILE = 256
MOE_TILE = 1024
ATT_TQ = 1024
ATT_TK = 1280
VMEM_LIMIT = 48 * 1024 * 1024


def _cparams(sem):
    return pltpu.CompilerParams(dimension_semantics=sem, vmem_limit_bytes=VMEM_LIMIT)


def _silu(x):
    return x * jax.nn.sigmoid(x)


def _ada_kernel(c_ref, w_ref, b_ref, o_ref):
    s = _silu(c_ref[...])
    o_ref[...] = jnp.dot(s, w_ref[...], preferred_element_type=F32,
                         precision=lax.Precision.HIGHEST) + b_ref[...]


def _ada_call(cond, ada_w, ada_b):
    depth, d, n = ada_w.shape
    tn = 1536
    return pl.pallas_call(
        _ada_kernel,
        grid=(depth, n // tn),
        in_specs=[pl.BlockSpec((8, d), lambda l, j: (0, 0)),
                  pl.BlockSpec((None, d, tn), lambda l, j: (l, 0, j)),
                  pl.BlockSpec((None, 1, tn), lambda l, j: (l, 0, j))],
        out_specs=pl.BlockSpec((None, 8, tn), lambda l, j: (l, 0, j)),
        out_shape=jax.ShapeDtypeStruct((depth, 8, n), F32),
        compiler_params=_cparams(("arbitrary", "arbitrary")),
        name="adaln",
    )(cond, ada_w, ada_b.reshape(depth, 1, n))


def _inproj_kernel(x_ref, nw_ref, mod_ref, w_ref, cs_ref, sn_ref, qnw_ref, wuq_ref, kvnw_ref, wukv_ref,
                   u_ref, rq_ref, rk_ref, rv_ref, gf_ref, gb_ref, q_ref, k_ref, v_ref):
    x = x_ref[...]
    y = x * lax.rsqrt(jnp.mean(x * x, axis=-1, keepdims=True) + EPS) * nw_ref[...]
    h = (y * (1.0 + mod_ref[1:2, :]) + mod_ref[0:1, :]).astype(BF16)

    def proj(c0, width):
        return jnp.dot(h, w_ref[:, c0:c0 + width], preferred_element_type=F32)

    cs = cs_ref[...]
    sn = sn_ref[...]
    lane = lax.broadcasted_iota(jnp.int32, (1, LANES), 1)
    rope_lanes = (lane >= MLA_D_NOPE) & (lane < MLA_D_NOPE + MLA_D_ROPE)
    csq = jnp.where(rope_lanes, cs, 1.0)
    snq = jnp.where(rope_lanes, sn, 0.0)

    ag = proj(C_CONV, 2 * CONV_W)
    u_ref[...] = ag[:, :CONV_W] * jax.nn.sigmoid(ag[:, CONV_W:])

    cs192 = jnp.concatenate([cs, cs[:, :RET_QK_W - LANES]], axis=1)
    sn192 = jnp.concatenate([sn, sn[:, :RET_QK_W - LANES]], axis=1)
    rq = proj(C_RQ, 256)[:, :RET_QK_W] * cs192 + proj(C_RQS, 256)[:, :RET_QK_W] * sn192
    rq_ref[...] = rq.astype(BF16)
    rk = proj(C_RK, 256)[:, :RET_QK_W] * cs192 + proj(C_RKS, 256)[:, :RET_QK_W] * sn192
    rk_ref[...] = (rk * (RET_DK ** -0.5)).astype(BF16)
    rv_ref[...] = proj(C_RV, RET_W).astype(BF16)
    gf_ref[...] = _silu(proj(C_RGF, RET_W))
    gb_ref[...] = _silu(proj(C_RGB, RET_W))

    cq = proj(C_CQ, 256)[:, :MLA_Q_RANK]
    cqn = (cq * lax.rsqrt(jnp.mean(cq * cq, axis=-1, keepdims=True) + EPS) * qnw_ref[...]).astype(BF16)
    qa = jnp.dot(cqn, wuq_ref[:, :ATT_W], preferred_element_type=F32)
    qb = jnp.dot(cqn, wuq_ref[:, ATT_W:], preferred_element_type=F32)
    ckv = proj(C_CKV, MLA_KV_RANK)
    ckvn = (ckv * lax.rsqrt(jnp.mean(ckv * ckv, axis=-1, keepdims=True) + EPS) * kvnw_ref[...]).astype(BF16)
    ka = jnp.dot(ckvn, wukv_ref[:, :ATT_W], preferred_element_type=F32)
    va = jnp.dot(ckvn, wukv_ref[:, ATT_W:], preferred_element_type=F32)
    kr = proj(C_KR, LANES) * csq + proj(C_KRS, LANES) * snq
    ones_lane = (lane == MLA_D_V).astype(F32)
    for hd in range(MLA_HEADS):
        sl = slice(hd * HEAD_PAD, (hd + 1) * HEAD_PAD)
        q_ref[hd] = ((qa[:, sl] * csq + qb[:, sl] * snq) * (MLA_SCALE * LOG2E)).astype(BF16)
        k_ref[hd] = (ka[:, sl] + kr).astype(BF16)
        v_ref[hd] = (va[:, sl] + ones_lane).astype(BF16)


def _inproj_call(x_all, nw, mod, w_ext, cs, sn, qnw, wuq, kvnw, wukv, n_lat_tiles):
    t_all, d = x_all.shape
    tm = ROW_TILE
    row = lambda i: (i, 0)
    const2 = lambda i: (0, 0)
    head = lambda i: (0, i, 0)
    cls = lambda i: (jnp.where(i < n_lat_tiles, 1, 0), 0, 0)
    out_shape = (
        jax.ShapeDtypeStruct((t_all, CONV_W), F32),
        jax.ShapeDtypeStruct((t_all, RET_QK_W), BF16),
        jax.ShapeDtypeStruct((t_all, RET_QK_W), BF16),
        jax.ShapeDtypeStruct((t_all, RET_W), BF16),
        jax.ShapeDtypeStruct((t_all, RET_W), F32),
        jax.ShapeDtypeStruct((t_all, RET_W), F32),
        jax.ShapeDtypeStruct((MLA_HEADS, t_all, HEAD_PAD), BF16),
        jax.ShapeDtypeStruct((MLA_HEADS, t_all, HEAD_PAD), BF16),
        jax.ShapeDtypeStruct((MLA_HEADS, t_all, HEAD_PAD), BF16),
    )
    out_specs = (
        pl.BlockSpec((tm, CONV_W), row),
        pl.BlockSpec((tm, RET_QK_W), row),
        pl.BlockSpec((tm, RET_QK_W), row),
        pl.BlockSpec((tm, RET_W), row),
        pl.BlockSpec((tm, RET_W), row),
        pl.BlockSpec((tm, RET_W), row),
        pl.BlockSpec((MLA_HEADS, tm, HEAD_PAD), head),
        pl.BlockSpec((MLA_HEADS, tm, HEAD_PAD), head),
        pl.BlockSpec((MLA_HEADS, tm, HEAD_PAD), head),
    )
    return pl.pallas_call(
        _inproj_kernel,
        grid=(t_all // tm,),
        in_specs=[pl.BlockSpec((tm, d), row),
                  pl.BlockSpec((1, d), const2),
                  pl.BlockSpec((None, 2, d), cls),
                  pl.BlockSpec((d, IN_EXT), const2),
                  pl.BlockSpec((tm, LANES), row),
                  pl.BlockSpec((tm, LANES), row),
                  pl.BlockSpec((1, MLA_Q_RANK), const2),
                  pl.BlockSpec((MLA_Q_RANK, 2 * ATT_W), const2),
                  pl.BlockSpec((1, MLA_KV_RANK), const2),
                  pl.BlockSpec((MLA_KV_RANK, 2 * ATT_W), const2)],
        out_specs=out_specs,
        out_shape=out_shape,
        compiler_params=_cparams(("arbitrary",)),
        name="inproj",
    )(x_all, nw, mod, w_ext, cs, sn, qnw, wuq, kvnw, wukv)


CONV_HALO = 16
CONV_SUB = 64
SUBLANES = 8


def _conv_kernel(prev_ref, cur_ref, next_ref, dw_ref, b_ref, lnw_ref, lnb_ref, o_ref, ext_ref, sh_ref, *,
                 n_lat_tiles):
    i = pl.program_id(0)
    n = pl.num_programs(0)
    tm = cur_ref.shape[0]
    seq_start = (i == 0) | (i == n_lat_tiles)
    seq_end = (i == n_lat_tiles - 1) | (i == n - 1)
    ext_ref[0:CONV_HALO, :] = jnp.where(seq_start, 0.0, prev_ref[...])
    ext_ref[CONV_HALO:CONV_HALO + tm, :] = cur_ref[...]
    ext_ref[CONV_HALO + tm:, :] = jnp.where(seq_end, 0.0, next_ref[...])
    span = sh_ref.shape[1]
    for ph in range(SUBLANES):
        sh_ref[ph] = ext_ref[pl.ds(ph, span), :]
    base = CONV_HALO - CONV_K // 2
    for r in range(tm // CONV_SUB):
        acc = jnp.zeros((CONV_SUB, CONV_W), F32) + b_ref[...]
        for k in range(CONV_K):
            off = base + k
            acc = acc + (sh_ref[off % SUBLANES, pl.ds(r * CONV_SUB + off - off % SUBLANES, CONV_SUB), :]
                         * dw_ref[k:k + 1, :])
        mu = jnp.mean(acc, axis=-1, keepdims=True)
        dlt = acc - mu
        var = jnp.mean(dlt * dlt, axis=-1, keepdims=True)
        y = dlt * lax.rsqrt(var + EPS) * lnw_ref[...] + lnb_ref[...]
        o_ref[r * CONV_SUB:(r + 1) * CONV_SUB, :] = _silu(y).astype(BF16)


def _conv_call(u, dw, b, lnw, lnb, n_lat_tiles):
    t_all = u.shape[0]
    tm = ROW_TILE
    hpt = tm // CONV_HALO
    n_halo = t_all // CONV_HALO
    const2 = lambda i: (0, 0)
    return pl.pallas_call(
        functools.partial(_conv_kernel, n_lat_tiles=n_lat_tiles),
        grid=(t_all // tm,),
        in_specs=[pl.BlockSpec((CONV_HALO, CONV_W), lambda i: (jnp.maximum(i * hpt - 1, 0), 0)),
                  pl.BlockSpec((tm, CONV_W), lambda i: (i, 0)),
                  pl.BlockSpec((CONV_HALO, CONV_W), lambda i: (jnp.minimum((i + 1) * hpt, n_halo - 1), 0)),
                  pl.BlockSpec((CONV_K, CONV_W), const2),
                  pl.BlockSpec((1, CONV_W), const2),
                  pl.BlockSpec((1, CONV_W), const2),
                  pl.BlockSpec((1, CONV_W), const2)],
        out_specs=pl.BlockSpec((tm, CONV_W), lambda i: (i, 0)),
        out_shape=jax.ShapeDtypeStruct((t_all, CONV_W), BF16),
        scratch_shapes=[pltpu.VMEM((tm + 2 * CONV_HALO, CONV_W), F32),
                        pltpu.VMEM((SUBLANES, tm + 2 * CONV_HALO - SUBLANES, CONV_W), F32)],
        compiler_params=_cparams(("arbitrary",)),
        name="conv",
    )(u, u, u, dw, b, lnw, lnb)


def _split_dot(x, a):
    hi = x.astype(BF16)
    lo = (x - hi.astype(F32)).astype(BF16)
    return (jnp.dot(hi, a, preferred_element_type=F32) + jnp.dot(lo, a, preferred_element_type=F32))


def _ret_direction(q, k, v, gate, r_ref, dmask, xi, zeta, gchunk, bdmask, avg, o_ref):
    lane_k = lax.broadcasted_iota(jnp.int32, (1, RET_QK_W), 1) // RET_DK
    lane_v = lax.broadcasted_iota(jnp.int32, (1, RET_W), 1) // RET_DV
    zero_k = jnp.zeros_like(k)
    zero_v = jnp.zeros_like(v)
    k_bd = jnp.concatenate([jnp.where(lane_k == hd, k, zero_k) for hd in range(RET_HEADS)], axis=0)
    v_bd = jnp.concatenate([jnp.where(lane_v == hd, v, zero_v) for hd in range(RET_HEADS)], axis=0)
    s = lax.dot_general(q, k_bd, (((1,), (1,)), ((), ())), preferred_element_type=F32)
    s = (s * dmask).astype(BF16)
    inner = jnp.dot(s, v_bd, preferred_element_type=F32)
    r = r_ref[...]
    cross = jnp.dot(q, r.astype(BF16), preferred_element_type=F32) * xi
    o = inner + cross
    kz = (k.astype(F32) * zeta).astype(BF16)
    ds = lax.dot_general(kz, v, (((0,), (0,)), ((), ())), preferred_element_type=F32)
    r_ref[...] = gchunk * r + ds * bdmask
    mu = jnp.dot(o.astype(BF16), avg, preferred_element_type=F32)
    dlt = o - mu
    var = jnp.dot((dlt * dlt).astype(BF16), avg, preferred_element_type=F32)
    o_ref[...] = (gate * (dlt * lax.rsqrt(var + RET_GN_EPS))).astype(BF16)


def _ret_kernel(qf_ref, kf_ref, vf_ref, gf_ref, qb_ref, kb_ref, vb_ref, gb_ref,
                dmf_ref, dmb_ref, xif_ref, xib_ref, ztf_ref, ztb_ref, gcf_ref, gcb_ref, bdm_ref, avg_ref,
                of_ref, ob_ref, rf_ref, rb_ref):
    @pl.when(pl.program_id(0) == 0)
    def _():
        rf_ref[...] = jnp.zeros_like(rf_ref)
        rb_ref[...] = jnp.zeros_like(rb_ref)

    bdm = bdm_ref[...]
    avg = avg_ref[...]
    _ret_direction(qf_ref[...], kf_ref[...], vf_ref[...], gf_ref[...], rf_ref, dmf_ref[...], xif_ref[...],
                   ztf_ref[...], gcf_ref[...], bdm, avg, of_ref)
    _ret_direction(qb_ref[...], kb_ref[...], vb_ref[...], gb_ref[...], rb_ref, dmb_ref[...], xib_ref[...],
                   ztb_ref[...], gcb_ref[...], bdm, avg, ob_ref)


def _ret_tables():
    c = RET_CHUNK
    gamma_f = 1.0 - 2.0 ** (-5.0 - jnp.arange(RET_HEADS, dtype=F32))
    gamma_b = gamma_f[::-1]
    idx = jnp.arange(c, dtype=F32)
    diff = idx[:, None] - idx[None, :]

    def tables(gamma, reverse):
        lg = jnp.log(gamma)
        d = -diff if reverse else diff
        dm = jnp.where(d[None] >= 0, jnp.exp(jnp.maximum(d, 0.0)[None] * lg[:, None, None]), 0.0)
        dm = jnp.transpose(dm, (1, 0, 2)).reshape(c, RET_HEADS * c)
        xi_e = (c - idx) if reverse else (idx + 1.0)
        zt_e = idx if reverse else (c - 1.0 - idx)
        xi = jnp.repeat(jnp.exp(xi_e[:, None] * lg[None, :]), RET_DV, axis=1)
        zt = jnp.repeat(jnp.exp(zt_e[:, None] * lg[None, :]), RET_DK, axis=1)
        gc = jnp.repeat(jnp.exp(c * lg), RET_DV)[None, :]
        return dm, xi, zt, gc

    dmf, xif, ztf, gcf = tables(gamma_f, False)
    dmb, xib, ztb, gcb = tables(gamma_b, True)
    hk = jnp.arange(RET_QK_W) // RET_DK
    hv = jnp.arange(RET_W) // RET_DV
    bdm = (hk[:, None] == hv[None, :]).astype(F32)
    avg = ((hv[:, None] == hv[None, :]).astype(F32) / RET_DV).astype(BF16)
    return (dmf, dmb, xif, xib, ztf, ztb, gcf, gcb, bdm, avg)


def _ret_call(rq, rk, rv, gf, gb, tabs, n_lat_chunks):
    t_all = rq.shape[0]
    c = RET_CHUNK
    n = t_all // c
    n_ctx_chunks = n - n_lat_chunks

    def fwd(i):
        return (jnp.where(i < n_ctx_chunks, n_lat_chunks + i, i - n_ctx_chunks), 0)

    def bwd(i):
        return (n - 1 - i, 0)

    const2 = lambda i: (0, 0)
    tab_specs = [pl.BlockSpec(t.shape, const2) for t in tabs]
    return pl.pallas_call(
        _ret_kernel,
        grid=(n,),
        in_specs=[pl.BlockSpec((c, RET_QK_W), fwd), pl.BlockSpec((c, RET_QK_W), fwd),
                  pl.BlockSpec((c, RET_W), fwd), pl.BlockSpec((c, RET_W), fwd),
                  pl.BlockSpec((c, RET_QK_W), bwd), pl.BlockSpec((c, RET_QK_W), bwd),
                  pl.BlockSpec((c, RET_W), bwd), pl.BlockSpec((c, RET_W), bwd)] + tab_specs,
        out_specs=(pl.BlockSpec((c, RET_W), fwd), pl.BlockSpec((c, RET_W), bwd)),
        out_shape=(jax.ShapeDtypeStruct((t_all, RET_W), BF16), jax.ShapeDtypeStruct((t_all, RET_W), BF16)),
        scratch_shapes=[pltpu.VMEM((RET_QK_W, RET_W), F32), pltpu.VMEM((RET_QK_W, RET_W), F32)],
        compiler_params=_cparams(("arbitrary",)),
        name="retention",
    )(rq, rk, rv, gf, rq, rk, rv, gb, *tabs)


ATT_SUB = 128


def _attn_kernel(q_ref, k_ref, vt_ref, o_ref, s0_scr, s1_scr, p0_scr, p1_scr, acc_scr):
    q = q_ref[...]
    tq = q.shape[0]
    nkc, tk, _ = k_ref.shape
    nsub = tk // ATT_SUB

    s_bufs = (s0_scr, s1_scr)
    p_bufs = (p0_scr, p1_scr)

    def pv(c, par, alpha):
        part = jnp.dot(vt_ref[c], p_bufs[par][...], preferred_element_type=F32)
        acc_scr[...] = alpha * acc_scr[...] + part

    def step(c, par, m_old, m_blk, alpha_prev, with_scores, with_pv):
        if with_pv:
            pv(c - 1, 1 - par, alpha_prev)
        m_new = jnp.maximum(m_old, m_blk)
        alpha = jnp.exp2(m_old - m_new)
        mx = jnp.full((8, tq), -jnp.inf, F32)
        for j in range(nsub):
            rows = pl.ds(j * ATT_SUB, ATT_SUB)
            p_bufs[par][rows, :] = jnp.exp2(s_bufs[par][rows, :] - m_new).astype(BF16)
            if with_scores:
                mx = jnp.maximum(mx, score_rows(c + 1, 1 - par, rows))
        return m_new, jnp.max(mx, axis=0, keepdims=True), alpha

    def score_rows(c, par, rows):
        s = jnp.dot(k_ref[c, rows, :], qt, preferred_element_type=F32)
        s_bufs[par][rows, :] = s
        return jnp.max(s.reshape(ATT_SUB // 8, 8, tq), axis=0)

    def scores(c, par):
        mx = jnp.full((8, tq), -jnp.inf, F32)
        for j in range(nsub):
            mx = jnp.maximum(mx, score_rows(c, par, pl.ds(j * ATT_SUB, ATT_SUB)))
        return jnp.max(mx, axis=0, keepdims=True)

    acc_scr[...] = jnp.zeros_like(acc_scr)
    qt = q.astype(F32).T.astype(BF16)
    m = jnp.full((1, tq), -jnp.inf, F32)
    m_blk = scores(0, 0)
    alpha = jnp.ones((1, tq), F32)
    if nkc > 1:
        m, m_blk, alpha = step(0, 0, m, m_blk, alpha, True, False)
        def body(c, carry):
            return lax.cond(c % 2 == 1,
                            lambda cr: step(c, 1, *cr, True, True),
                            lambda cr: step(c, 0, *cr, True, True), carry)

        m, m_blk, alpha = lax.fori_loop(1, nkc - 1, body, (m, m_blk, alpha))
    last = nkc - 1
    m, _, alpha = step(last, last % 2, m, m_blk, alpha, False, nkc > 1)
    pv(last, last % 2, alpha)
    acc = acc_scr[...]
    out_t = acc / acc[MLA_D_V:MLA_D_V + 1, :]
    o_ref[...] = out_t.T.astype(BF16)


def _attn_kernel_into(q_ref, k_ref, vt_ref, dst_ref, o_ref, *scratch):
    del dst_ref
    _attn_kernel(q_ref, k_ref, vt_ref, o_ref, *scratch)


def _attn_call(q, k, v, tq, tk, n_q_blocks, out_rows, into=None, out_block0=0):
    nh = q.shape[0]
    t_k = k.shape[1]
    nkc = t_k // tk
    k4 = k.reshape(nh, nkc, tk, HEAD_PAD)
    vt4 = jnp.swapaxes(v.reshape(nh, nkc, tk, HEAD_PAD), 2, 3)
    in_specs = [pl.BlockSpec((None, tq, HEAD_PAD), lambda h, j: (h, j, 0)),
                pl.BlockSpec((None, nkc, tk, HEAD_PAD), lambda h, j: (h, 0, 0, 0)),
                pl.BlockSpec((None, nkc, HEAD_PAD, tk), lambda h, j: (h, 0, 0, 0))]
    args = [q, k4, vt4]
    body, aliases = _attn_kernel, {}
    if into is not None:
        in_specs.append(pl.BlockSpec(memory_space=pl.ANY))
        args.append(into)
        body, aliases = _attn_kernel_into, {3: 0}
    return pl.pallas_call(
        body,
        grid=(nh, n_q_blocks),
        in_specs=in_specs,
        out_specs=pl.BlockSpec((tq, HEAD_PAD), lambda h, j: (out_block0 + j, h)),
        out_shape=jax.ShapeDtypeStruct((out_rows, nh * HEAD_PAD), BF16),
        scratch_shapes=[pltpu.VMEM((tk, tq), F32), pltpu.VMEM((tk, tq), F32),
                        pltpu.VMEM((tk, tq), BF16), pltpu.VMEM((tk, tq), BF16), pltpu.VMEM((HEAD_PAD, tq), F32)],
        input_output_aliases=aliases,
        compiler_params=_cparams(("arbitrary", "arbitrary")),
        name="attention",
    )(*args)


def _top2_sum(a, b, c, d):
    hi1, lo1 = jnp.maximum(a, b), jnp.minimum(a, b)
    hi2, lo2 = jnp.maximum(c, d), jnp.minimum(c, d)
    return jnp.maximum(hi1, hi2) + jnp.maximum(jnp.minimum(hi1, hi2), jnp.maximum(lo1, lo2))


def _gates_t(aff, sel, gt_ref):
    rows = [sel[e:e + 1, :] for e in range(N_EXPERTS)]
    g_score = [_top2_sum(*rows[g * EXPERTS_PER_GROUP:(g + 1) * EXPERTS_PER_GROUP]) for g in range(N_GROUPS)]
    best = g_score[0]
    best_g = jnp.zeros_like(best, dtype=jnp.int32)
    for g in range(1, N_GROUPS):
        better = g_score[g] > best
        best = jnp.where(better, g_score[g], best)
        best_g = jnp.where(better, g, best_g)
    picked = []
    for e in range(N_EXPERTS):
        g = e // EXPERTS_PER_GROUP
        rank = jnp.zeros_like(best_g)
        for o in range(g * EXPERTS_PER_GROUP, (g + 1) * EXPERTS_PER_GROUP):
            if o == e:
                continue
            ahead = (rows[o] >= rows[e]) if o < e else (rows[o] > rows[e])
            rank = rank + jnp.where(ahead, 1, 0)
        picked.append(jnp.where(best_g == g, rank, 2) < 2)
    w = [jnp.where(picked[e], aff[e:e + 1, :], 0.0) for e in range(N_EXPERTS)]
    total = w[0]
    for e in range(1, N_EXPERTS):
        total = total + w[e]
    for e in range(N_EXPERTS):
        gt_ref[e:e + 1, :] = w[e] / total


OUT_HALF = D_MODEL // 2


def _outproj_kernel(x_ref, conv_ref, of_ref, ob_ref, att_ref, w_ref, mod_ref, nw_ref, rw_ref, rb_ref,
                    x1_ref, h2_ref, gt_ref):
    ret = (of_ref[...].astype(F32) + ob_ref[...].astype(F32)).astype(BF16)
    mix = jnp.concatenate([conv_ref[...], ret, att_ref[...]], axis=1)
    halves = []
    ssq = 0.0
    for hf in range(2):
        cols = slice(hf * OUT_HALF, (hf + 1) * OUT_HALF)
        o = jnp.dot(mix, w_ref[:, cols], preferred_element_type=F32)
        x1 = x_ref[:, cols] + mod_ref[2:3, cols] * o
        x1_ref[:, cols] = x1
        ssq = ssq + jnp.sum(x1 * x1, axis=-1, keepdims=True)
        halves.append(x1)
    inv = lax.rsqrt(ssq * (1.0 / D_MODEL) + EPS)
    logit_parts = 0.0
    for hf in range(2):
        cols = slice(hf * OUT_HALF, (hf + 1) * OUT_HALF)
        h2 = (halves[hf] * inv * nw_ref[:, cols]) * (1.0 + mod_ref[4:5, cols]) + mod_ref[3:4, cols]
        hi = h2.astype(BF16)
        h2_ref[:, cols] = hi
        lo = (h2 - hi.astype(F32)).astype(BF16)
        logit_parts = (logit_parts + jnp.dot(hi, rw_ref[cols, :], preferred_element_type=F32)
                       + jnp.dot(lo, rw_ref[cols, :], preferred_element_type=F32))
    lt = logit_parts.T
    logits = lt[0:N_EXPERTS, :] + lt[N_EXPERTS:2 * N_EXPERTS, :]
    aff = jax.nn.sigmoid(logits)
    _gates_t(aff, aff + rb_ref[...], gt_ref)


def _outproj_call(x_all, conv, of, ob, att, w_ext, mod, nw, rw3, rb, n_lat_tiles, n_tiles):
    d = x_all.shape[1]
    tm = ROW_TILE
    row = lambda i: (i, 0)
    const2 = lambda i: (0, 0)
    cls = lambda i: (jnp.where(i < n_lat_tiles, 1, 0), 0, 0)
    return pl.pallas_call(
        _outproj_kernel,
        grid=(n_tiles,),
        in_specs=[pl.BlockSpec((tm, d), row),
                  pl.BlockSpec((tm, CONV_W), row),
                  pl.BlockSpec((tm, RET_W), row),
                  pl.BlockSpec((tm, RET_W), row),
                  pl.BlockSpec((tm, ATT_W), row),
                  pl.BlockSpec(w_ext.shape, const2),
                  pl.BlockSpec((None, 8, d), cls),
                  pl.BlockSpec((1, d), const2),
                  pl.BlockSpec((d, LANES), const2),
                  pl.BlockSpec((N_EXPERTS, 1), const2)],
        out_specs=(pl.BlockSpec((tm, d), row), pl.BlockSpec((tm, d), row),
                   pl.BlockSpec((N_EXPERTS, tm), lambda i: (0, i))),
        out_shape=(jax.ShapeDtypeStruct((n_tiles * tm, d), F32), jax.ShapeDtypeStruct((n_tiles * tm, d), BF16),
                   jax.ShapeDtypeStruct((N_EXPERTS, n_tiles * tm), F32)),
        compiler_params=_cparams(("arbitrary",)),
        name="outproj",
    )(x_all, conv, of, ob, att, w_ext, mod, nw, rw3, rb)


def _moe_kernel(h_ref, g_ref, x1_ref, g2_ref, wgu_ref, wd_ref, *rest, final_norm, into):
    rest = list(rest)
    fw_ref = rest.pop(0) if final_norm else None
    if into:
        rest.pop(0)
    o_ref, acc_ref = rest
    e = pl.program_id(1)

    @pl.when(e == 0)
    def _():
        acc_ref[...] = jnp.zeros_like(acc_ref)

    gu = jnp.dot(h_ref[...], wgu_ref[...], preferred_element_type=F32)
    he = _silu(gu[:, :D_EXPERT]) * gu[:, D_EXPERT:]
    pick = (lax.broadcasted_iota(jnp.int32, (N_EXPERTS, LANES), 0) == e).astype(BF16)
    gate = _split_dot(g_ref[...], pick)
    heg = (he * jnp.concatenate([gate, gate], axis=1)).astype(BF16)
    acc_ref[...] += jnp.dot(heg, wd_ref[...], preferred_element_type=F32)

    @pl.when(e == pl.num_programs(1) - 1)
    def _():
        x2 = x1_ref[...] + g2_ref[...] * acc_ref[...]
        if final_norm:
            x2 = x2 * lax.rsqrt(jnp.mean(x2 * x2, axis=-1, keepdims=True) + EPS) * fw_ref[...]
        o_ref[...] = x2


def _moe_call(h2, gates, x1, g2, wgu, wd, tm, block0, n_tiles, out_rows, final_w=None, into=None):
    d = x1.shape[1]
    rows = lambda i, e: (block0 + i, 0)
    in_specs = [pl.BlockSpec((tm, d), rows),
                pl.BlockSpec((tm, N_EXPERTS), rows),
                pl.BlockSpec((tm, d), rows),
                pl.BlockSpec((1, d), lambda i, e: (0, 0)),
                pl.BlockSpec((None, d, 2 * D_EXPERT), lambda i, e: (e, 0, 0)),
                pl.BlockSpec((None, D_EXPERT, d), lambda i, e: (e, 0, 0))]
    args = [h2, gates, x1, g2, wgu, wd]
    aliases = {}
    if final_w is not None:
        in_specs.append(pl.BlockSpec((1, d), lambda i, e: (0, 0)))
        args.append(final_w)
    if into is not None:
        in_specs.append(pl.BlockSpec(memory_space=pl.ANY))
        aliases = {len(args): 0}
        args.append(into)
    return pl.pallas_call(
        functools.partial(_moe_kernel, final_norm=final_w is not None, into=into is not None),
        grid=(n_tiles, N_EXPERTS),
        in_specs=in_specs,
        out_specs=pl.BlockSpec((tm, d), rows),
        out_shape=jax.ShapeDtypeStruct((out_rows, d), F32),
        scratch_shapes=[pltpu.VMEM((tm, d), F32)],
        input_output_aliases=aliases,
        compiler_params=_cparams(("arbitrary", "arbitrary")),
        name="experts",
    )(*args)


_SWAP32 = np.concatenate([np.arange(8, 16), np.arange(0, 8), np.arange(24, 32), np.arange(16, 24)])


def _pad_cols(w, width):
    return jnp.pad(w, ((0, 0), (0, width - w.shape[1])))


def _in_weight(w_in):
    sizes = (2 * CONV_W, RET_QK_W, RET_QK_W, RET_W, RET_W, RET_W, MLA_Q_RANK, MLA_KV_RANK, MLA_D_ROPE)
    offs = np.concatenate([[0], np.cumsum(sizes)])
    conv, rq, rk, rv, gf, gb, cq, ckv, kr = [w_in[:, offs[i]:offs[i + 1]] for i in range(len(sizes))]
    swap192 = np.concatenate([h * ROPE_DIM + _SWAP32 for h in range(RET_HEADS)])
    d = w_in.shape[0]

    def place_rope(w):
        return jnp.concatenate([jnp.zeros((d, MLA_D_NOPE), w.dtype), w,
                                jnp.zeros((d, LANES - MLA_D_NOPE - MLA_D_ROPE), w.dtype)], axis=1)

    ext = jnp.concatenate([
        conv, _pad_cols(rq, 256), _pad_cols(rq[:, swap192], 256), _pad_cols(rk, 256), _pad_cols(rk[:, swap192], 256),
        rv, gf, gb, _pad_cols(cq, 256), ckv, place_rope(kr), place_rope(kr[:, _SWAP32])], axis=1)
    assert ext.shape[1] == IN_EXT
    return ext.astype(BF16)


def _uq_weight(w_uq):
    r = w_uq.shape[0]
    w = w_uq.reshape(r, MLA_HEADS, MLA_D_NOPE + MLA_D_ROPE)
    nope, rope = w[..., :MLA_D_NOPE], w[..., MLA_D_NOPE:]
    zpad = jnp.zeros((r, MLA_HEADS, HEAD_PAD - MLA_D_NOPE - MLA_D_ROPE), w.dtype)
    main = jnp.concatenate([nope, rope, zpad], axis=-1).reshape(r, ATT_W)
    part = jnp.concatenate([jnp.zeros_like(nope), rope[..., _SWAP32], zpad], axis=-1).reshape(r, ATT_W)
    return jnp.concatenate([main, part], axis=1).astype(BF16)


def _ukv_weight(w_ukv):
    r = w_ukv.shape[0]
    w = w_ukv.reshape(r, MLA_HEADS, MLA_D_NOPE + MLA_D_V)
    zpad = jnp.zeros((r, MLA_HEADS, HEAD_PAD - MLA_D_NOPE), w.dtype)
    kpart = jnp.concatenate([w[..., :MLA_D_NOPE], zpad], axis=-1).reshape(r, ATT_W)
    vpart = jnp.concatenate([w[..., MLA_D_NOPE:], zpad], axis=-1).reshape(r, ATT_W)
    return jnp.concatenate([kpart, vpart], axis=1).astype(BF16)


def _out_weight(w_out):
    d = w_out.shape[1]
    conv, ret = w_out[:CONV_W], w_out[CONV_W:CONV_W + RET_W]
    att = w_out[CONV_W + RET_W:].reshape(MLA_HEADS, MLA_D_V, d)
    att = jnp.pad(att, ((0, 0), (0, HEAD_PAD - MLA_D_V), (0, 0))).reshape(ATT_W, d)
    return jnp.concatenate([conv, ret, att], axis=0).astype(BF16)


def _rope_tables(n_ctx, seq):
    pos = jnp.arange(seq)
    r = (pos // GRID_W).astype(F32)
    cl = (pos % GRID_W).astype(F32)
    inv = ROPE_BASE ** (-jnp.arange(ROPE_PAIRS, dtype=F32) / ROPE_PAIRS)
    ar, ac = r[:, None] * inv, cl[:, None] * inv
    cs32 = jnp.concatenate([jnp.cos(ar), jnp.cos(ar), jnp.cos(ac), jnp.cos(ac)], axis=1)
    sn32 = jnp.concatenate([-jnp.sin(ar), jnp.sin(ar), -jnp.sin(ac), jnp.sin(ac)], axis=1)
    cs = jnp.concatenate([cs32, jnp.ones((n_ctx, ROPE_DIM), F32)], axis=0)
    sn = jnp.concatenate([sn32, jnp.zeros((n_ctx, ROPE_DIM), F32)], axis=0)
    return jnp.tile(cs, (1, LANES // ROPE_DIM)), jnp.tile(sn, (1, LANES // ROPE_DIM))


def _router_weight(router_w):
    hi = router_w.astype(BF16)
    lo = (router_w - hi.astype(F32)).astype(BF16)
    return jnp.pad(jnp.concatenate([hi, lo], axis=1), ((0, 0), (0, LANES - 2 * N_EXPERTS)))


def kernel(x, c, ctx, c_ctx, ada_w, ada_b, norm1_w, norm2_w, w_in, conv_dw, conv_b, conv_ln_w, conv_ln_b,
           mla_q_norm_w, mla_w_uq, mla_kv_norm_w, mla_w_ukv, w_out, router_w, router_bias,
           moe_w_gate, moe_w_up, moe_w_down, final_norm_w):
    batch, seq, d = x.shape
    n_ctx = ctx.shape[1]
    depth = ada_w.shape[0]
    t_all = seq + n_ctx
    assert batch == 1 and d == D_MODEL
    assert n_ctx % ROW_TILE == 0 and seq % ROW_TILE == 0 and n_ctx <= MOE_TILE and seq % MOE_TILE == 0
    assert seq % ATT_TQ == 0 and t_all % ATT_TK == 0 and n_ctx % RET_CHUNK == 0 and seq % n_ctx == 0
    n_lat_tiles = seq // ROW_TILE
    n_all_tiles = t_all // ROW_TILE

    cond = jnp.zeros((8, d), F32).at[0].set(c_ctx).at[1].set(c[0])
    mod = _ada_call(cond, ada_w, ada_b)[:, :2, :].reshape(depth, 2, 6, d)
    mod8 = jnp.pad(mod, ((0, 0), (0, 0), (0, 2), (0, 0)))
    cs, sn = _rope_tables(n_ctx, seq)
    ret_tabs = _ret_tables()
    rw3 = _router_weight(router_w)
    rb = router_bias.reshape(N_EXPERTS, 1)

    x_all = jnp.concatenate([x[0], ctx[0]], axis=0)
    for l in range(depth):
        last = l == depth - 1
        u, rq, rk, rv, gf, gb, q_att, k_att, v_att = _inproj_call(
            x_all, norm1_w[l][None, :], mod8[l][:, :2, :], _in_weight(w_in[l]), cs, sn,
            mla_q_norm_w[l][None, :], _uq_weight(mla_w_uq[l]), mla_kv_norm_w[l][None, :], _ukv_weight(mla_w_ukv[l]),
            n_lat_tiles)
        conv = _conv_call(u, conv_dw[l], conv_b[l][None, :], conv_ln_w[l][None, :], conv_ln_b[l][None, :],
                          n_lat_tiles)
        of, ob = _ret_call(rq, rk, rv, gf, gb, ret_tabs, seq // RET_CHUNK)
        att = _attn_call(q_att, k_att, v_att, ATT_TQ, ATT_TK, seq // ATT_TQ, seq if last else t_all)
        if not last:
            tqc = min(ATT_TQ, n_ctx)
            att = _attn_call(q_att[:, seq:], k_att[:, seq:], v_att[:, seq:], tqc, n_ctx, n_ctx // tqc, t_all,
                             into=att, out_block0=seq // tqc)
        n_tiles = n_lat_tiles if last else n_all_tiles
        x1, h2, gates_t = _outproj_call(x_all, conv, of, ob, att, _out_weight(w_out[l]), mod8[l],
                                        norm2_w[l][None, :], rw3, rb, n_lat_tiles, n_tiles)
        gates = gates_t.T
        wgu = jnp.concatenate([moe_w_gate[l], moe_w_up[l]], axis=-1).astype(BF16)
        wd = moe_w_down[l].astype(BF16)
        x_all = _moe_call(h2, gates, x1, mod[l, 1, 5][None, :], wgu, wd, MOE_TILE, 0, seq // MOE_TILE,
                          seq if last else t_all, final_w=final_norm_w[None, :] if last else None)
        if not last:
            x_all = _moe_call(h2, gates, x1, mod[l, 0, 5][None, :], wgu, wd, n_ctx, seq // n_ctx, 1, t_all,
                              into=x_all)
    return x_all[None]
```

```python
import functools
import math

import numpy as np
import jax
import jax.numpy as jnp
from jax import lax
from jax.experimental import pallas as pl
from jax.experimental.pallas import tpu as pltpu

F32 = jnp.float32
BF16 = jnp.bfloat16

D_MODEL = 1024
GRID_W = 64
CONV_W = 256
CONV_K = 31
RET_HEADS = 6
RET_DK = 32
RET_DV = 64
RET_QK_W = RET_HEADS * RET_DK
RET_W = RET_HEADS * RET_DV
RET_CHUNK = 128
RET_GN_EPS = 1e-5
MLA_HEADS = 6
MLA_Q_RANK = 192
MLA_KV_RANK = 128
MLA_D_NOPE = 64
MLA_D_ROPE = 32
MLA_D_V = 64
MLA_SCALE = (MLA_D_NOPE + MLA_D_ROPE) ** -0.5
ROPE_DIM = 32
ROPE_PAIRS = ROPE_DIM // 4
ROPE_BASE = 10000.0
N_EXPERTS = 16
N_GROUPS = 4
EXPERTS_PER_GROUP = N_EXPERTS // N_GROUPS
D_EXPERT = 256
EPS = 1e-6

LANES = 128
HEAD_PAD = LANES
ATT_W = MLA_HEADS * HEAD_PAD
LOG2E = math.log2(math.e)

C_CONV = 0
C_RQ = 512
C_RQS = 768
C_RK = 1024
C_RKS = 1280
C_RV = 1536
C_RGF = 1920
C_RGB = 2304
C_CQ = 2688
C_CKV = 2944
C_KR = 3072
C_KRS = 3200
IN_EXT = 3328

ROW_TILE = 256
MOE_TILE = 512
ATT_TQ = 1024
ATT_TK = 1280
VMEM_LIMIT = 48 * 1024 * 1024


def _cparams(sem):
    return pltpu.CompilerParams(dimension_semantics=sem, vmem_limit_bytes=VMEM_LIMIT)


def _silu(x):
    return x * jax.nn.sigmoid(x)


def _ada_kernel(c_ref, w_ref, b_ref, o_ref):
    s = _silu(c_ref[...])
    o_ref[...] = jnp.dot(s, w_ref[...], preferred_element_type=F32,
                         precision=lax.Precision.HIGHEST) + b_ref[...]


def _ada_call(cond, ada_w, ada_b):
    depth, d, n = ada_w.shape
    tn = 1536
    return pl.pallas_call(
        _ada_kernel,
        grid=(depth, n // tn),
        in_specs=[pl.BlockSpec((8, d), lambda l, j: (0, 0)),
                  pl.BlockSpec((None, d, tn), lambda l, j: (l, 0, j)),
                  pl.BlockSpec((None, 1, tn), lambda l, j: (l, 0, j))],
        out_specs=pl.BlockSpec((None, 8, tn), lambda l, j: (l, 0, j)),
        out_shape=jax.ShapeDtypeStruct((depth, 8, n), F32),
        compiler_params=_cparams(("arbitrary", "arbitrary")),
        name="adaln",
    )(cond, ada_w, ada_b.reshape(depth, 1, n))


def _inproj_kernel(x_ref, nw_ref, mod_ref, w_ref, cs_ref, sn_ref, qnw_ref, wuq_ref, kvnw_ref, wukv_ref,
                   u_ref, rq_ref, rk_ref, rv_ref, gf_ref, gb_ref, q_ref, k_ref, v_ref):
    x = x_ref[...]
    y = x * lax.rsqrt(jnp.mean(x * x, axis=-1, keepdims=True) + EPS) * nw_ref[...]
    h = (y * (1.0 + mod_ref[1:2, :]) + mod_ref[0:1, :]).astype(BF16)

    def proj(c0, width):
        return jnp.dot(h, w_ref[:, c0:c0 + width], preferred_element_type=F32)

    cs = cs_ref[...]
    sn = sn_ref[...]
    lane = lax.broadcasted_iota(jnp.int32, (1, LANES), 1)
    rope_lanes = (lane >= MLA_D_NOPE) & (lane < MLA_D_NOPE + MLA_D_ROPE)
    csq = jnp.where(rope_lanes, cs, 1.0)
    snq = jnp.where(rope_lanes, sn, 0.0)

    ag = proj(C_CONV, 2 * CONV_W)
    u_ref[...] = ag[:, :CONV_W] * jax.nn.sigmoid(ag[:, CONV_W:])

    cs192 = jnp.concatenate([cs, cs[:, :RET_QK_W - LANES]], axis=1)
    sn192 = jnp.concatenate([sn, sn[:, :RET_QK_W - LANES]], axis=1)
    rq = proj(C_RQ, 256)[:, :RET_QK_W] * cs192 + proj(C_RQS, 256)[:, :RET_QK_W] * sn192
    rq_ref[...] = rq.astype(BF16)
    rk = proj(C_RK, 256)[:, :RET_QK_W] * cs192 + proj(C_RKS, 256)[:, :RET_QK_W] * sn192
    rk_ref[...] = (rk * (RET_DK ** -0.5)).astype(BF16)
    rv_ref[...] = proj(C_RV, RET_W).astype(BF16)
    gf_ref[...] = _silu(proj(C_RGF, RET_W))
    gb_ref[...] = _silu(proj(C_RGB, RET_W))

    cq = proj(C_CQ, 256)[:, :MLA_Q_RANK]
    cqn = (cq * lax.rsqrt(jnp.mean(cq * cq, axis=-1, keepdims=True) + EPS) * qnw_ref[...]).astype(BF16)
    qa = jnp.dot(cqn, wuq_ref[:, :ATT_W], preferred_element_type=F32)
    qb = jnp.dot(cqn, wuq_ref[:, ATT_W:], preferred_element_type=F32)
    ckv = proj(C_CKV, MLA_KV_RANK)
    ckvn = (ckv * lax.rsqrt(jnp.mean(ckv * ckv, axis=-1, keepdims=True) + EPS) * kvnw_ref[...]).astype(BF16)
    ka = jnp.dot(ckvn, wukv_ref[:, :ATT_W], preferred_element_type=F32)
    va = jnp.dot(ckvn, wukv_ref[:, ATT_W:], preferred_element_type=F32)
    kr = proj(C_KR, LANES) * csq + proj(C_KRS, LANES) * snq
    ones_lane = (lane == MLA_D_V).astype(F32)
    for hd in range(MLA_HEADS):
        sl = slice(hd * HEAD_PAD, (hd + 1) * HEAD_PAD)
        q_ref[hd] = ((qa[:, sl] * csq + qb[:, sl] * snq) * (MLA_SCALE * LOG2E)).astype(BF16)
        k_ref[hd] = (ka[:, sl] + kr).astype(BF16)
        v_ref[hd] = (va[:, sl] + ones_lane).astype(BF16)


def _inproj_call(x_all, nw, mod, w_ext, cs, sn, qnw, wuq, kvnw, wukv, n_lat_tiles):
    t_all, d = x_all.shape
    tm = ROW_TILE
    row = lambda i: (i, 0)
    const2 = lambda i: (0, 0)
    head = lambda i: (0, i, 0)
    cls = lambda i: (jnp.where(i < n_lat_tiles, 1, 0), 0, 0)
    out_shape = (
        jax.ShapeDtypeStruct((t_all, CONV_W), F32),
        jax.ShapeDtypeStruct((t_all, RET_QK_W), BF16),
        jax.ShapeDtypeStruct((t_all, RET_QK_W), BF16),
        jax.ShapeDtypeStruct((t_all, RET_W), BF16),
        jax.ShapeDtypeStruct((t_all, RET_W), F32),
        jax.ShapeDtypeStruct((t_all, RET_W), F32),
        jax.ShapeDtypeStruct((MLA_HEADS, t_all, HEAD_PAD), BF16),
        jax.ShapeDtypeStruct((MLA_HEADS, t_all, HEAD_PAD), BF16),
        jax.ShapeDtypeStruct((MLA_HEADS, t_all, HEAD_PAD), BF16),
    )
    out_specs = (
        pl.BlockSpec((tm, CONV_W), row),
        pl.BlockSpec((tm, RET_QK_W), row),
        pl.BlockSpec((tm, RET_QK_W), row),
        pl.BlockSpec((tm, RET_W), row),
        pl.BlockSpec((tm, RET_W), row),
        pl.BlockSpec((tm, RET_W), row),
        pl.BlockSpec((MLA_HEADS, tm, HEAD_PAD), head),
        pl.BlockSpec((MLA_HEADS, tm, HEAD_PAD), head),
        pl.BlockSpec((MLA_HEADS, tm, HEAD_PAD), head),
    )
    return pl.pallas_call(
        _inproj_kernel,
        grid=(t_all // tm,),
        in_specs=[pl.BlockSpec((tm, d), row),
                  pl.BlockSpec((1, d), const2),
                  pl.BlockSpec((None, 2, d), cls),
                  pl.BlockSpec((d, IN_EXT), const2),
                  pl.BlockSpec((tm, LANES), row),
                  pl.BlockSpec((tm, LANES), row),
                  pl.BlockSpec((1, MLA_Q_RANK), const2),
                  pl.BlockSpec((MLA_Q_RANK, 2 * ATT_W), const2),
                  pl.BlockSpec((1, MLA_KV_RANK), const2),
                  pl.BlockSpec((MLA_KV_RANK, 2 * ATT_W), const2)],
        out_specs=out_specs,
        out_shape=out_shape,
        compiler_params=_cparams(("arbitrary",)),
        name="inproj",
    )(x_all, nw, mod, w_ext, cs, sn, qnw, wuq, kvnw, wukv)


CONV_HALO = 16
CONV_SUB = 64
SUBLANES = 8


def _conv_kernel(prev_ref, cur_ref, next_ref, dw_ref, b_ref, lnw_ref, lnb_ref, o_ref, ext_ref, sh_ref, *,
                 n_lat_tiles):
    i = pl.program_id(0)
    n = pl.num_programs(0)
    tm = cur_ref.shape[0]
    seq_start = (i == 0) | (i == n_lat_tiles)
    seq_end = (i == n_lat_tiles - 1) | (i == n - 1)
    ext_ref[0:CONV_HALO, :] = jnp.where(seq_start, 0.0, prev_ref[...])
    ext_ref[CONV_HALO:CONV_HALO + tm, :] = cur_ref[...]
    ext_ref[CONV_HALO + tm:, :] = jnp.where(seq_end, 0.0, next_ref[...])
    span = sh_ref.shape[1]
    for ph in range(SUBLANES):
        sh_ref[ph] = ext_ref[pl.ds(ph, span), :]
    base = CONV_HALO - CONV_K // 2
    for r in range(tm // CONV_SUB):
        acc = jnp.zeros((CONV_SUB, CONV_W), F32) + b_ref[...]
        for k in range(CONV_K):
            off = base + k
            acc = acc + (sh_ref[off % SUBLANES, pl.ds(r * CONV_SUB + off - off % SUBLANES, CONV_SUB), :]
                         * dw_ref[k:k + 1, :])
        mu = jnp.mean(acc, axis=-1, keepdims=True)
        dlt = acc - mu
        var = jnp.mean(dlt * dlt, axis=-1, keepdims=True)
        y = dlt * lax.rsqrt(var + EPS) * lnw_ref[...] + lnb_ref[...]
        o_ref[r * CONV_SUB:(r + 1) * CONV_SUB, :] = _silu(y).astype(BF16)


def _conv_call(u, dw, b, lnw, lnb, n_lat_tiles):
    t_all = u.shape[0]
    tm = ROW_TILE
    hpt = tm // CONV_HALO
    n_halo = t_all // CONV_HALO
    const2 = lambda i: (0, 0)
    return pl.pallas_call(
        functools.partial(_conv_kernel, n_lat_tiles=n_lat_tiles),
        grid=(t_all // tm,),
        in_specs=[pl.BlockSpec((CONV_HALO, CONV_W), lambda i: (jnp.maximum(i * hpt - 1, 0), 0)),
                  pl.BlockSpec((tm, CONV_W), lambda i: (i, 0)),
                  pl.BlockSpec((CONV_HALO, CONV_W), lambda i: (jnp.minimum((i + 1) * hpt, n_halo - 1), 0)),
                  pl.BlockSpec((CONV_K, CONV_W), const2),
                  pl.BlockSpec((1, CONV_W), const2),
                  pl.BlockSpec((1, CONV_W), const2),
                  pl.BlockSpec((1, CONV_W), const2)],
        out_specs=pl.BlockSpec((tm, CONV_W), lambda i: (i, 0)),
        out_shape=jax.ShapeDtypeStruct((t_all, CONV_W), BF16),
        scratch_shapes=[pltpu.VMEM((tm + 2 * CONV_HALO, CONV_W), F32),
                        pltpu.VMEM((SUBLANES, tm + 2 * CONV_HALO - SUBLANES, CONV_W), F32)],
        compiler_params=_cparams(("arbitrary",)),
        name="conv",
    )(u, u, u, dw, b, lnw, lnb)


def _split_dot(x, a):
    hi = x.astype(BF16)
    lo = (x - hi.astype(F32)).astype(BF16)
    return (jnp.dot(hi, a, preferred_element_type=F32) + jnp.dot(lo, a, preferred_element_type=F32))


def _ret_direction(q, k, v, gate, r_ref, dmask, xi, zeta, gchunk, bdmask, avg, o_ref):
    lane_k = lax.broadcasted_iota(jnp.int32, (1, RET_QK_W), 1) // RET_DK
    lane_v = lax.broadcasted_iota(jnp.int32, (1, RET_W), 1) // RET_DV
    zero_k = jnp.zeros_like(k)
    zero_v = jnp.zeros_like(v)
    k_bd = jnp.concatenate([jnp.where(lane_k == hd, k, zero_k) for hd in range(RET_HEADS)], axis=0)
    v_bd = jnp.concatenate([jnp.where(lane_v == hd, v, zero_v) for hd in range(RET_HEADS)], axis=0)
    s = lax.dot_general(q, k_bd, (((1,), (1,)), ((), ())), preferred_element_type=F32)
    s = (s * dmask).astype(BF16)
    inner = jnp.dot(s, v_bd, preferred_element_type=F32)
    r = r_ref[...]
    cross = jnp.dot(q, r.astype(BF16), preferred_element_type=F32) * xi
    o = inner + cross
    kz = (k.astype(F32) * zeta).astype(BF16)
    ds = lax.dot_general(kz, v, (((0,), (0,)), ((), ())), preferred_element_type=F32)
    r_ref[...] = gchunk * r + ds * bdmask
    mu = jnp.dot(o.astype(BF16), avg, preferred_element_type=F32)
    dlt = o - mu
    var = jnp.dot((dlt * dlt).astype(BF16), avg, preferred_element_type=F32)
    o_ref[...] = (gate * (dlt * lax.rsqrt(var + RET_GN_EPS))).astype(BF16)


def _ret_kernel(qf_ref, kf_ref, vf_ref, gf_ref, qb_ref, kb_ref, vb_ref, gb_ref,
                dmf_ref, dmb_ref, xif_ref, xib_ref, ztf_ref, ztb_ref, gcf_ref, gcb_ref, bdm_ref, avg_ref,
                of_ref, ob_ref, rf_ref, rb_ref):
    @pl.when(pl.program_id(0) == 0)
    def _():
        rf_ref[...] = jnp.zeros_like(rf_ref)
        rb_ref[...] = jnp.zeros_like(rb_ref)

    bdm = bdm_ref[...]
    avg = avg_ref[...]
    _ret_direction(qf_ref[...], kf_ref[...], vf_ref[...], gf_ref[...], rf_ref, dmf_ref[...], xif_ref[...],
                   ztf_ref[...], gcf_ref[...], bdm, avg, of_ref)
    _ret_direction(qb_ref[...], kb_ref[...], vb_ref[...], gb_ref[...], rb_ref, dmb_ref[...], xib_ref[...],
                   ztb_ref[...], gcb_ref[...], bdm, avg, ob_ref)


def _ret_tables():
    c = RET_CHUNK
    gamma_f = 1.0 - 2.0 ** (-5.0 - jnp.arange(RET_HEADS, dtype=F32))
    gamma_b = gamma_f[::-1]
    idx = jnp.arange(c, dtype=F32)
    diff = idx[:, None] - idx[None, :]

    def tables(gamma, reverse):
        lg = jnp.log(gamma)
        d = -diff if reverse else diff
        dm = jnp.where(d[None] >= 0, jnp.exp(jnp.maximum(d, 0.0)[None] * lg[:, None, None]), 0.0)
        dm = jnp.transpose(dm, (1, 0, 2)).reshape(c, RET_HEADS * c)
        xi_e = (c - idx) if reverse else (idx + 1.0)
        zt_e = idx if reverse else (c - 1.0 - idx)
        xi = jnp.repeat(jnp.exp(xi_e[:, None] * lg[None, :]), RET_DV, axis=1)
        zt = jnp.repeat(jnp.exp(zt_e[:, None] * lg[None, :]), RET_DK, axis=1)
        gc = jnp.repeat(jnp.exp(c * lg), RET_DV)[None, :]
        return dm, xi, zt, gc

    dmf, xif, ztf, gcf = tables(gamma_f, False)
    dmb, xib, ztb, gcb = tables(gamma_b, True)
    hk = jnp.arange(RET_QK_W) // RET_DK
    hv = jnp.arange(RET_W) // RET_DV
    bdm = (hk[:, None] == hv[None, :]).astype(F32)
    avg = ((hv[:, None] == hv[None, :]).astype(F32) / RET_DV).astype(BF16)
    return (dmf, dmb, xif, xib, ztf, ztb, gcf, gcb, bdm, avg)


def _ret_call(rq, rk, rv, gf, gb, tabs, n_lat_chunks):
    t_all = rq.shape[0]
    c = RET_CHUNK
    n = t_all // c
    n_ctx_chunks = n - n_lat_chunks

    def fwd(i):
        return (jnp.where(i < n_ctx_chunks, n_lat_chunks + i, i - n_ctx_chunks), 0)

    def bwd(i):
        return (n - 1 - i, 0)

    const2 = lambda i: (0, 0)
    tab_specs = [pl.BlockSpec(t.shape, const2) for t in tabs]
    return pl.pallas_call(
        _ret_kernel,
        grid=(n,),
        in_specs=[pl.BlockSpec((c, RET_QK_W), fwd), pl.BlockSpec((c, RET_QK_W), fwd),
                  pl.BlockSpec((c, RET_W), fwd), pl.BlockSpec((c, RET_W), fwd),
                  pl.BlockSpec((c, RET_QK_W), bwd), pl.BlockSpec((c, RET_QK_W), bwd),
                  pl.BlockSpec((c, RET_W), bwd), pl.BlockSpec((c, RET_W), bwd)] + tab_specs,
        out_specs=(pl.BlockSpec((c, RET_W), fwd), pl.BlockSpec((c, RET_W), bwd)),
        out_shape=(jax.ShapeDtypeStruct((t_all, RET_W), BF16), jax.ShapeDtypeStruct((t_all, RET_W), BF16)),
        scratch_shapes=[pltpu.VMEM((RET_QK_W, RET_W), F32), pltpu.VMEM((RET_QK_W, RET_W), F32)],
        compiler_params=_cparams(("arbitrary",)),
        name="retention",
    )(rq, rk, rv, gf, rq, rk, rv, gb, *tabs)


ATT_SUB = 128
VT_ROWS = 80
PV_KEYS = 256


def _attn_kernel(q_ref, k_ref, vt_ref, o_ref, s0_scr, s1_scr, p0_scr, p1_scr, acc_scr):
    q = q_ref[...]
    tq = q.shape[0]
    nkc, tk, _ = k_ref.shape
    nsub = tk // ATT_SUB

    s_bufs = (s0_scr, s1_scr)
    p_bufs = (p0_scr, p1_scr)

    def pv_part(c, par, t, alpha):
        keys = pl.ds(t * PV_KEYS, PV_KEYS)
        part = jnp.dot(vt_ref[c, :, keys], p_bufs[par][keys, :], preferred_element_type=F32)
        if t == 0:
            acc_scr[...] = alpha * acc_scr[...] + part
        else:
            acc_scr[...] += part

    def pv(c, par, alpha):
        for t in range(tk // PV_KEYS):
            pv_part(c, par, t, alpha)

    def step(c, par, m_old, m_blk, alpha_prev, with_scores, with_pv):
        m_new = jnp.maximum(m_old, m_blk)
        alpha = jnp.exp2(m_old - m_new)
        mx = jnp.full((8, tq), -jnp.inf, F32)
        per_part = PV_KEYS // ATT_SUB
        for j in range(nsub):
            rows = pl.ds(j * ATT_SUB, ATT_SUB)
            p_bufs[par][rows, :] = jnp.exp2(s_bufs[par][rows, :] - m_new).astype(BF16)
            if with_scores:
                mx = jnp.maximum(mx, score_rows(c + 1, 1 - par, rows))
            if with_pv and j % per_part == per_part - 1:
                pv_part(c - 1, 1 - par, j // per_part, alpha_prev)
        return m_new, jnp.max(mx, axis=0, keepdims=True), alpha

    def score_rows(c, par, rows):
        s = jnp.dot(k_ref[c, rows, :], qt, preferred_element_type=F32)
        s_bufs[par][rows, :] = s
        return jnp.max(s.reshape(ATT_SUB // 8, 8, tq), axis=0)

    def scores(c, par):
        mx = jnp.full((8, tq), -jnp.inf, F32)
        for j in range(nsub):
            mx = jnp.maximum(mx, score_rows(c, par, pl.ds(j * ATT_SUB, ATT_SUB)))
        return jnp.max(mx, axis=0, keepdims=True)

    acc_scr[...] = jnp.zeros_like(acc_scr)
    qt = q.astype(F32).T.astype(BF16)
    m = jnp.full((1, tq), -jnp.inf, F32)
    m_blk = scores(0, 0)
    alpha = jnp.ones((1, tq), F32)
    if nkc > 1:
        m, m_blk, alpha = step(0, 0, m, m_blk, alpha, True, False)
        def body(c, carry):
            return lax.cond(c % 2 == 1,
                            lambda cr: step(c, 1, *cr, True, True),
                            lambda cr: step(c, 0, *cr, True, True), carry)

        m, m_blk, alpha = lax.fori_loop(1, nkc - 1, body, (m, m_blk, alpha))
    last = nkc - 1
    m, _, alpha = step(last, last % 2, m, m_blk, alpha, False, nkc > 1)
    pv(last, last % 2, alpha)
    acc = acc_scr[...]
    out_t = acc / acc[MLA_D_V:MLA_D_V + 1, :]
    out_t = jnp.concatenate([out_t, jnp.zeros((HEAD_PAD - VT_ROWS, tq), F32)], axis=0)
    o_ref[...] = out_t.T.astype(BF16)


def _attn_kernel_into(q_ref, k_ref, vt_ref, dst_ref, o_ref, *scratch):
    del dst_ref
    _attn_kernel(q_ref, k_ref, vt_ref, o_ref, *scratch)


def _attn_call(q, k, v, tq, tk, n_q_blocks, out_rows, into=None, out_block0=0):
    nh = q.shape[0]
    t_k = k.shape[1]
    nkc = t_k // tk
    k4 = k.reshape(nh, nkc, tk, HEAD_PAD)
    vt4 = jnp.swapaxes(v[:, :, :VT_ROWS].reshape(nh, nkc, tk, VT_ROWS), 2, 3)
    in_specs = [pl.BlockSpec((None, tq, HEAD_PAD), lambda h, j: (h, j, 0)),
                pl.BlockSpec((None, nkc, tk, HEAD_PAD), lambda h, j: (h, 0, 0, 0)),
                pl.BlockSpec((None, nkc, VT_ROWS, tk), lambda h, j: (h, 0, 0, 0))]
    args = [q, k4, vt4]
    body, aliases = _attn_kernel, {}
    if into is not None:
        in_specs.append(pl.BlockSpec(memory_space=pl.ANY))
        args.append(into)
        body, aliases = _attn_kernel_into, {3: 0}
    return pl.pallas_call(
        body,
        grid=(nh, n_q_blocks),
        in_specs=in_specs,
        out_specs=pl.BlockSpec((tq, HEAD_PAD), lambda h, j: (out_block0 + j, h)),
        out_shape=jax.ShapeDtypeStruct((out_rows, nh * HEAD_PAD), BF16),
        scratch_shapes=[pltpu.VMEM((tk, tq), F32), pltpu.VMEM((tk, tq), F32),
                        pltpu.VMEM((tk, tq), BF16), pltpu.VMEM((tk, tq), BF16), pltpu.VMEM((VT_ROWS, tq), F32)],
        input_output_aliases=aliases,
        compiler_params=_cparams(("arbitrary", "arbitrary")),
        name="attention",
    )(*args)


def _top2_sum(a, b, c, d):
    hi1, lo1 = jnp.maximum(a, b), jnp.minimum(a, b)
    hi2, lo2 = jnp.maximum(c, d), jnp.minimum(c, d)
    return jnp.maximum(hi1, hi2) + jnp.maximum(jnp.minimum(hi1, hi2), jnp.maximum(lo1, lo2))


def _gates_t(aff, sel, gt_ref, grp_ref):
    rows = [sel[e:e + 1, :] for e in range(N_EXPERTS)]
    g_score = [_top2_sum(*rows[g * EXPERTS_PER_GROUP:(g + 1) * EXPERTS_PER_GROUP]) for g in range(N_GROUPS)]
    best = g_score[0]
    best_g = jnp.zeros_like(best, dtype=jnp.int32)
    for g in range(1, N_GROUPS):
        better = g_score[g] > best
        best = jnp.where(better, g_score[g], best)
        best_g = jnp.where(better, g, best_g)
    picked = []
    for e in range(N_EXPERTS):
        g = e // EXPERTS_PER_GROUP
        rank = jnp.zeros_like(best_g)
        for o in range(g * EXPERTS_PER_GROUP, (g + 1) * EXPERTS_PER_GROUP):
            if o == e:
                continue
            ahead = (rows[o] >= rows[e]) if o < e else (rows[o] > rows[e])
            rank = rank + jnp.where(ahead, 1, 0)
        picked.append(jnp.where(best_g == g, rank, 2) < 2)
    w = [jnp.where(picked[e], aff[e:e + 1, :], 0.0) for e in range(N_EXPERTS)]
    total = w[0]
    for e in range(1, N_EXPERTS):
        total = total + w[e]
    for e in range(N_EXPERTS):
        gt_ref[e:e + 1, :] = w[e] / total
    grp_ref[...] = best_g


OUT_HALF = D_MODEL // 2


def _outproj_kernel(x_ref, conv_ref, of_ref, ob_ref, att_ref, w_ref, mod_ref, nw_ref, rw_ref, rb_ref,
                    x1_ref, h2_ref, gt_ref, grp_ref):
    ret = (of_ref[...].astype(F32) + ob_ref[...].astype(F32)).astype(BF16)
    mix = jnp.concatenate([conv_ref[...], ret, att_ref[...]], axis=1)
    halves = []
    ssq = 0.0
    for hf in range(2):
        cols = slice(hf * OUT_HALF, (hf + 1) * OUT_HALF)
        o = jnp.dot(mix, w_ref[:, cols], preferred_element_type=F32)
        x1 = x_ref[:, cols] + mod_ref[2:3, cols] * o
        x1_ref[:, cols] = x1
        ssq = ssq + jnp.sum(x1 * x1, axis=-1, keepdims=True)
        halves.append(x1)
    inv = lax.rsqrt(ssq * (1.0 / D_MODEL) + EPS)
    logit_parts = 0.0
    for hf in range(2):
        cols = slice(hf * OUT_HALF, (hf + 1) * OUT_HALF)
        h2 = (halves[hf] * inv * nw_ref[:, cols]) * (1.0 + mod_ref[4:5, cols]) + mod_ref[3:4, cols]
        hi = h2.astype(BF16)
        h2_ref[:, cols] = hi
        lo = (h2 - hi.astype(F32)).astype(BF16)
        logit_parts = (logit_parts + jnp.dot(hi, rw_ref[cols, :], preferred_element_type=F32)
                       + jnp.dot(lo, rw_ref[cols, :], preferred_element_type=F32))
    lt = logit_parts.T
    logits = lt[0:N_EXPERTS, :] + lt[N_EXPERTS:2 * N_EXPERTS, :]
    aff = jax.nn.sigmoid(logits)
    _gates_t(aff, aff + rb_ref[...], gt_ref, grp_ref)


def _outproj_call(x_all, conv, of, ob, att, w_ext, mod, nw, rw3, rb, n_lat_tiles, n_tiles):
    d = x_all.shape[1]
    tm = ROW_TILE
    row = lambda i: (i, 0)
    const2 = lambda i: (0, 0)
    cls = lambda i: (jnp.where(i < n_lat_tiles, 1, 0), 0, 0)
    return pl.pallas_call(
        _outproj_kernel,
        grid=(n_tiles,),
        in_specs=[pl.BlockSpec((tm, d), row),
                  pl.BlockSpec((tm, CONV_W), row),
                  pl.BlockSpec((tm, RET_W), row),
                  pl.BlockSpec((tm, RET_W), row),
                  pl.BlockSpec((tm, ATT_W), row),
                  pl.BlockSpec(w_ext.shape, const2),
                  pl.BlockSpec((None, 8, d), cls),
                  pl.BlockSpec((1, d), const2),
                  pl.BlockSpec((d, LANES), const2),
                  pl.BlockSpec((N_EXPERTS, 1), const2)],
        out_specs=(pl.BlockSpec((tm, d), row), pl.BlockSpec((tm, d), row),
                   pl.BlockSpec((N_EXPERTS, tm), lambda i: (0, i)), pl.BlockSpec((1, tm), lambda i: (0, i))),
        out_shape=(jax.ShapeDtypeStruct((n_tiles * tm, d), F32), jax.ShapeDtypeStruct((n_tiles * tm, d), BF16),
                   jax.ShapeDtypeStruct((N_EXPERTS, n_tiles * tm), F32),
                   jax.ShapeDtypeStruct((1, n_tiles * tm), jnp.int32)),
        compiler_params=_cparams(("arbitrary",)),
        name="outproj",
    )(x_all, conv, of, ob, att, w_ext, mod, nw, rw3, rb)


MOE_DENSE_ROWS = 256


def _group_mlp(xb, gate4, wgu_ref, wd_ref):
    gu = jnp.dot(xb, wgu_ref[...], preferred_element_type=F32)
    parts = []
    for k in range(EXPERTS_PER_GROUP):
        a = gu[:, 2 * D_EXPERT * k:2 * D_EXPERT * k + D_EXPERT]
        b = gu[:, 2 * D_EXPERT * k + D_EXPERT:2 * D_EXPERT * (k + 1)]
        parts.append((_silu(a) * b * gate4[:, k:k + 1]).astype(BF16))
    return jnp.dot(jnp.concatenate(parts, axis=1), wd_ref[...], preferred_element_type=F32)


def _moe_kernel(h_ref, gate8_ref, grp_ref, x1_ref, g2_ref, tri_ref, wgu_ref, wd_ref, *rest, cap, final_norm, into):
    rest = list(rest)
    fw_ref = rest.pop(0) if final_norm else None
    if into:
        rest.pop(0)
    o_ref, cnt_scr = rest
    g = pl.program_id(1)
    tm = h_ref.shape[0]

    @pl.when(g == 0)
    def _():
        o_ref[...] = jnp.zeros_like(o_ref)
        rows = lax.broadcasted_iota(jnp.int32, (SUBLANES, tm), 0)
        member8 = jnp.where(rows == grp_ref[...], 1.0, 0.0).astype(BF16)
        cnt_scr[...] = jnp.dot(member8, tri_ref[...], preferred_element_type=F32)

    member = grp_ref[...] == g
    incl = cnt_scr[pl.ds(g, 1), :]
    count = jnp.max(incl)
    gate8 = gate8_ref[...]

    @pl.when(count <= cap)
    def _():
        pos = jnp.where(member, incl.astype(jnp.int32) - 1, -1)
        slot = lax.broadcasted_iota(jnp.int32, (cap, tm), 0)
        onehot = jnp.where(slot == pos, 1.0, 0.0).astype(BF16)
        xg = jnp.dot(onehot, h_ref[...], preferred_element_type=F32).astype(BF16)
        g8 = jnp.dot(onehot, gate8, preferred_element_type=F32)
        y = _group_mlp(xg, g8[:, :EXPERTS_PER_GROUP] + g8[:, EXPERTS_PER_GROUP:], wgu_ref, wd_ref)
        o_ref[...] += lax.dot_general(onehot, y.astype(BF16), (((0,), (0,)), ((), ())),
                                      preferred_element_type=F32)

    @pl.when(count > cap)
    def _():
        def chunk(ci, carry):
            rows = pl.ds(pl.multiple_of(ci * MOE_DENSE_ROWS, MOE_DENSE_ROWS), MOE_DENSE_ROWS)
            g8 = gate8_ref[rows, :].astype(F32)
            o_ref[rows, :] += _group_mlp(h_ref[rows, :], g8[:, :EXPERTS_PER_GROUP] + g8[:, EXPERTS_PER_GROUP:],
                                         wgu_ref, wd_ref)
            return carry

        lax.fori_loop(0, tm // MOE_DENSE_ROWS, chunk, 0)

    @pl.when(g == pl.num_programs(1) - 1)
    def _():
        x2 = x1_ref[...] + g2_ref[...] * o_ref[...]
        if final_norm:
            x2 = x2 * lax.rsqrt(jnp.mean(x2 * x2, axis=-1, keepdims=True) + EPS) * fw_ref[...]
        o_ref[...] = x2


def _moe_call(h2, gate8, grp, x1, g2, wgu, wd, tm, block0, n_tiles, out_rows, final_w=None, into=None):
    d = x1.shape[1]
    cap = tm * 5 // 16
    assert cap % 16 == 0 and tm % MOE_DENSE_ROWS == 0
    tri = (jnp.arange(tm)[:, None] <= jnp.arange(tm)[None, :]).astype(BF16)
    rows = lambda i, g: (block0 + i, 0)
    const = lambda i, g: (0, 0)
    in_specs = [pl.BlockSpec((tm, d), rows),
                pl.BlockSpec((None, tm, 2 * EXPERTS_PER_GROUP), lambda i, g: (g, block0 + i, 0)),
                pl.BlockSpec((1, tm), lambda i, g: (0, block0 + i)),
                pl.BlockSpec((tm, d), rows),
                pl.BlockSpec((1, d), const),
                pl.BlockSpec((tm, tm), const),
                pl.BlockSpec((None, d, wgu.shape[2]), lambda i, g: (g, 0, 0)),
                pl.BlockSpec((None, wd.shape[1], d), lambda i, g: (g, 0, 0))]
    args = [h2, gate8, grp, x1, g2, tri, wgu, wd]
    aliases = {}
    if final_w is not None:
        in_specs.append(pl.BlockSpec((1, d), const))
        args.append(final_w)
    if into is not None:
        in_specs.append(pl.BlockSpec(memory_space=pl.ANY))
        aliases = {len(args): 0}
        args.append(into)
    return pl.pallas_call(
        functools.partial(_moe_kernel, cap=cap, final_norm=final_w is not None, into=into is not None),
        grid=(n_tiles, N_GROUPS),
        in_specs=in_specs,
        out_specs=pl.BlockSpec((tm, d), rows),
        out_shape=jax.ShapeDtypeStruct((out_rows, d), F32),
        scratch_shapes=[pltpu.VMEM((SUBLANES, tm), F32)],
        input_output_aliases=aliases,
        compiler_params=_cparams(("arbitrary", "arbitrary")),
        name="experts",
    )(*args)


_SWAP32 = np.concatenate([np.arange(8, 16), np.arange(0, 8), np.arange(24, 32), np.arange(16, 24)])


def _pad_cols(w, width):
    return jnp.pad(w, ((0, 0), (0, width - w.shape[1])))


def _in_weight(w_in):
    sizes = (2 * CONV_W, RET_QK_W, RET_QK_W, RET_W, RET_W, RET_W, MLA_Q_RANK, MLA_KV_RANK, MLA_D_ROPE)
    offs = np.concatenate([[0], np.cumsum(sizes)])
    conv, rq, rk, rv, gf, gb, cq, ckv, kr = [w_in[:, offs[i]:offs[i + 1]] for i in range(len(sizes))]
    swap192 = np.concatenate([h * ROPE_DIM + _SWAP32 for h in range(RET_HEADS)])
    d = w_in.shape[0]

    def place_rope(w):
        return jnp.concatenate([jnp.zeros((d, MLA_D_NOPE), w.dtype), w,
                                jnp.zeros((d, LANES - MLA_D_NOPE - MLA_D_ROPE), w.dtype)], axis=1)

    ext = jnp.concatenate([
        conv, _pad_cols(rq, 256), _pad_cols(rq[:, swap192], 256), _pad_cols(rk, 256), _pad_cols(rk[:, swap192], 256),
        rv, gf, gb, _pad_cols(cq, 256), ckv, place_rope(kr), place_rope(kr[:, _SWAP32])], axis=1)
    assert ext.shape[1] == IN_EXT
    return ext.astype(BF16)


def _uq_weight(w_uq):
    r = w_uq.shape[0]
    w = w_uq.reshape(r, MLA_HEADS, MLA_D_NOPE + MLA_D_ROPE)
    nope, rope = w[..., :MLA_D_NOPE], w[..., MLA_D_NOPE:]
    zpad = jnp.zeros((r, MLA_HEADS, HEAD_PAD - MLA_D_NOPE - MLA_D_ROPE), w.dtype)
    main = jnp.concatenate([nope, rope, zpad], axis=-1).reshape(r, ATT_W)
    part = jnp.concatenate([jnp.zeros_like(nope), rope[..., _SWAP32], zpad], axis=-1).reshape(r, ATT_W)
    return jnp.concatenate([main, part], axis=1).astype(BF16)


def _ukv_weight(w_ukv):
    r = w_ukv.shape[0]
    w = w_ukv.reshape(r, MLA_HEADS, MLA_D_NOPE + MLA_D_V)
    zpad = jnp.zeros((r, MLA_HEADS, HEAD_PAD - MLA_D_NOPE), w.dtype)
    kpart = jnp.concatenate([w[..., :MLA_D_NOPE], zpad], axis=-1).reshape(r, ATT_W)
    vpart = jnp.concatenate([w[..., MLA_D_NOPE:], zpad], axis=-1).reshape(r, ATT_W)
    return jnp.concatenate([kpart, vpart], axis=1).astype(BF16)


def _out_weight(w_out):
    d = w_out.shape[1]
    conv, ret = w_out[:CONV_W], w_out[CONV_W:CONV_W + RET_W]
    att = w_out[CONV_W + RET_W:].reshape(MLA_HEADS, MLA_D_V, d)
    att = jnp.pad(att, ((0, 0), (0, HEAD_PAD - MLA_D_V), (0, 0))).reshape(ATT_W, d)
    return jnp.concatenate([conv, ret, att], axis=0).astype(BF16)


def _rope_tables(n_ctx, seq):
    pos = jnp.arange(seq)
    r = (pos // GRID_W).astype(F32)
    cl = (pos % GRID_W).astype(F32)
    inv = ROPE_BASE ** (-jnp.arange(ROPE_PAIRS, dtype=F32) / ROPE_PAIRS)
    ar, ac = r[:, None] * inv, cl[:, None] * inv
    cs32 = jnp.concatenate([jnp.cos(ar), jnp.cos(ar), jnp.cos(ac), jnp.cos(ac)], axis=1)
    sn32 = jnp.concatenate([-jnp.sin(ar), jnp.sin(ar), -jnp.sin(ac), jnp.sin(ac)], axis=1)
    cs = jnp.concatenate([cs32, jnp.ones((n_ctx, ROPE_DIM), F32)], axis=0)
    sn = jnp.concatenate([sn32, jnp.zeros((n_ctx, ROPE_DIM), F32)], axis=0)
    return jnp.tile(cs, (1, LANES // ROPE_DIM)), jnp.tile(sn, (1, LANES // ROPE_DIM))


def _router_weight(router_w):
    hi = router_w.astype(BF16)
    lo = (router_w - hi.astype(F32)).astype(BF16)
    return jnp.pad(jnp.concatenate([hi, lo], axis=1), ((0, 0), (0, LANES - 2 * N_EXPERTS)))


def _group_gates(gates_t):
    t = gates_t.shape[1]
    g4 = jnp.transpose(gates_t.reshape(N_GROUPS, EXPERTS_PER_GROUP, t), (0, 2, 1))
    hi = g4.astype(BF16)
    lo = (g4 - hi.astype(F32)).astype(BF16)
    return jnp.concatenate([hi, lo], axis=-1)


def _group_weights(w_gate, w_up, w_down):
    e, d, f = w_gate.shape
    gu = jnp.concatenate([w_gate, w_up], axis=-1).astype(BF16)
    gu = gu.reshape(N_GROUPS, EXPERTS_PER_GROUP, d, 2 * f)
    gu = jnp.transpose(gu, (0, 2, 1, 3)).reshape(N_GROUPS, d, EXPERTS_PER_GROUP * 2 * f)
    wd = w_down.astype(BF16).reshape(N_GROUPS, EXPERTS_PER_GROUP * f, d)
    return gu, wd


def kernel(x, c, ctx, c_ctx, ada_w, ada_b, norm1_w, norm2_w, w_in, conv_dw, conv_b, conv_ln_w, conv_ln_b,
           mla_q_norm_w, mla_w_uq, mla_kv_norm_w, mla_w_ukv, w_out, router_w, router_bias,
           moe_w_gate, moe_w_up, moe_w_down, final_norm_w):
    batch, seq, d = x.shape
    n_ctx = ctx.shape[1]
    depth = ada_w.shape[0]
    t_all = seq + n_ctx
    assert batch == 1 and d == D_MODEL
    assert n_ctx % ROW_TILE == 0 and seq % ROW_TILE == 0 and n_ctx <= MOE_TILE and seq % MOE_TILE == 0
    assert seq % ATT_TQ == 0 and t_all % ATT_TK == 0 and n_ctx % RET_CHUNK == 0 and seq % n_ctx == 0
    n_lat_tiles = seq // ROW_TILE
    n_all_tiles = t_all // ROW_TILE

    cond = jnp.zeros((8, d), F32).at[0].set(c_ctx).at[1].set(c[0])
    mod = _ada_call(cond, ada_w, ada_b)[:, :2, :].reshape(depth, 2, 6, d)
    mod8 = jnp.pad(mod, ((0, 0), (0, 0), (0, 2), (0, 0)))
    cs, sn = _rope_tables(n_ctx, seq)
    ret_tabs = _ret_tables()
    rw3 = _router_weight(router_w)
    rb = router_bias.reshape(N_EXPERTS, 1)

    x_all = jnp.concatenate([x[0], ctx[0]], axis=0)
    for l in range(depth):
        last = l == depth - 1
        u, rq, rk, rv, gf, gb, q_att, k_att, v_att = _inproj_call(
            x_all, norm1_w[l][None, :], mod8[l][:, :2, :], _in_weight(w_in[l]), cs, sn,
            mla_q_norm_w[l][None, :], _uq_weight(mla_w_uq[l]), mla_kv_norm_w[l][None, :], _ukv_weight(mla_w_ukv[l]),
            n_lat_tiles)
        conv = _conv_call(u, conv_dw[l], conv_b[l][None, :], conv_ln_w[l][None, :], conv_ln_b[l][None, :],
                          n_lat_tiles)
        of, ob = _ret_call(rq, rk, rv, gf, gb, ret_tabs, seq // RET_CHUNK)
        att = _attn_call(q_att, k_att, v_att, ATT_TQ, ATT_TK, seq // ATT_TQ, seq if last else t_all)
        if not last:
            tqc = min(ATT_TQ, n_ctx)
            att = _attn_call(q_att[:, seq:], k_att[:, seq:], v_att[:, seq:], tqc, n_ctx, n_ctx // tqc, t_all,
                             into=att, out_block0=seq // tqc)
        n_tiles = n_lat_tiles if last else n_all_tiles
        x1, h2, gates_t, grp = _outproj_call(x_all, conv, of, ob, att, _out_weight(w_out[l]), mod8[l],
                                             norm2_w[l][None, :], rw3, rb, n_lat_tiles, n_tiles)
        gate8 = _group_gates(gates_t)
        wgu, wd = _group_weights(moe_w_gate[l], moe_w_up[l], moe_w_down[l])
        x_all = _moe_call(h2, gate8, grp, x1, mod[l, 1, 5][None, :], wgu, wd, MOE_TILE, 0, seq // MOE_TILE,
                          seq if last else t_all, final_w=final_norm_w[None, :] if last else None)
        if not last:
            x_all = _moe_call(h2, gate8, grp, x1, mod[l, 0, 5][None, :], wgu, wd, n_ctx, seq // n_ctx, 1, t_all,
                              into=x_all)
    return x_all[None]
```

```python
import functools
import math

import numpy as np
import jax
import jax.numpy as jnp
from jax import lax
from jax.experimental import pallas as pl
from jax.experimental.pallas import tpu as pltpu

F32 = jnp.float32
BF16 = jnp.bfloat16

D_MODEL = 1024
GRID_W = 64
CONV_W = 256
CONV_K = 31
RET_HEADS = 6
RET_DK = 32
RET_DV = 64
RET_QK_W = RET_HEADS * RET_DK
RET_W = RET_HEADS * RET_DV
RET_CHUNK = 128
RET_GN_EPS = 1e-5
MLA_HEADS = 6
MLA_Q_RANK = 192
MLA_KV_RANK = 128
MLA_D_NOPE = 64
MLA_D_ROPE = 32
MLA_D_V = 64
MLA_SCALE = (MLA_D_NOPE + MLA_D_ROPE) ** -0.5
ROPE_DIM = 32
ROPE_PAIRS = ROPE_DIM // 4
ROPE_BASE = 10000.0
N_EXPERTS = 16
N_GROUPS = 4
EXPERTS_PER_GROUP = N_EXPERTS // N_GROUPS
D_EXPERT = 256
EPS = 1e-6

LANES = 128
HEAD_PAD = LANES
ATT_W = MLA_HEADS * HEAD_PAD
LOG2E = math.log2(math.e)

C_CONV = 0
C_RQ = 512
C_RQS = 768
C_RK = 1024
C_RKS = 1280
C_RV = 1536
C_RGF = 1920
C_RGB = 2304
C_CQ = 2688
C_CKV = 2944
C_KR = 3072
C_KRS = 3200
IN_EXT = 3328

ROW_TILE = 256
MOE_TILE = 512
ATT_TQ = 1024
ATT_TK = 1280
VMEM_LIMIT = 48 * 1024 * 1024


def _cparams(sem):
    return pltpu.CompilerParams(dimension_semantics=sem, vmem_limit_bytes=VMEM_LIMIT)


def _silu(x):
    return x * jax.nn.sigmoid(x)


def _ada_kernel(c_ref, w_ref, b_ref, o_ref):
    s = _silu(c_ref[...])
    o_ref[...] = jnp.dot(s, w_ref[...], preferred_element_type=F32,
                         precision=lax.Precision.HIGHEST) + b_ref[...]


def _ada_call(cond, ada_w, ada_b):
    depth, d, n = ada_w.shape
    tn = 1536
    return pl.pallas_call(
        _ada_kernel,
        grid=(depth, n // tn),
        in_specs=[pl.BlockSpec((8, d), lambda l, j: (0, 0)),
                  pl.BlockSpec((None, d, tn), lambda l, j: (l, 0, j)),
                  pl.BlockSpec((None, 1, tn), lambda l, j: (l, 0, j))],
        out_specs=pl.BlockSpec((None, 8, tn), lambda l, j: (l, 0, j)),
        out_shape=jax.ShapeDtypeStruct((depth, 8, n), F32),
        compiler_params=_cparams(("arbitrary", "arbitrary")),
        name="adaln",
    )(cond, ada_w, ada_b.reshape(depth, 1, n))


def _inproj_kernel(x_ref, nw_ref, mod_ref, w_ref, cs_ref, sn_ref, qnw_ref, wuq_ref, kvnw_ref, wukv_ref,
                   u_ref, rq_ref, rk_ref, rv_ref, gf_ref, gb_ref, q_ref, k_ref, v_ref):
    x = x_ref[...]
    y = x * lax.rsqrt(jnp.mean(x * x, axis=-1, keepdims=True) + EPS) * nw_ref[...]
    h = (y * (1.0 + mod_ref[1:2, :]) + mod_ref[0:1, :]).astype(BF16)

    def proj(c0, width):
        return jnp.dot(h, w_ref[:, c0:c0 + width], preferred_element_type=F32)

    cs = cs_ref[...]
    sn = sn_ref[...]
    lane = lax.broadcasted_iota(jnp.int32, (1, LANES), 1)
    rope_lanes = (lane >= MLA_D_NOPE) & (lane < MLA_D_NOPE + MLA_D_ROPE)
    csq = jnp.where(rope_lanes, cs, 1.0)
    snq = jnp.where(rope_lanes, sn, 0.0)

    ag = proj(C_CONV, 2 * CONV_W)
    u_ref[...] = ag[:, :CONV_W] * jax.nn.sigmoid(ag[:, CONV_W:])

    cs192 = jnp.concatenate([cs, cs[:, :RET_QK_W - LANES]], axis=1)
    sn192 = jnp.concatenate([sn, sn[:, :RET_QK_W - LANES]], axis=1)
    rq = proj(C_RQ, 256)[:, :RET_QK_W] * cs192 + proj(C_RQS, 256)[:, :RET_QK_W] * sn192
    rq_ref[...] = rq.astype(BF16)
    rk = proj(C_RK, 256)[:, :RET_QK_W] * cs192 + proj(C_RKS, 256)[:, :RET_QK_W] * sn192
    rk_ref[...] = (rk * (RET_DK ** -0.5)).astype(BF16)
    rv_ref[...] = proj(C_RV, RET_W).astype(BF16)
    gf_ref[...] = _silu(proj(C_RGF, RET_W))
    gb_ref[...] = _silu(proj(C_RGB, RET_W))

    cq = proj(C_CQ, 256)[:, :MLA_Q_RANK]
    cqn = (cq * lax.rsqrt(jnp.mean(cq * cq, axis=-1, keepdims=True) + EPS) * qnw_ref[...]).astype(BF16)
    qa = jnp.dot(cqn, wuq_ref[:, :ATT_W], preferred_element_type=F32)
    qb = jnp.dot(cqn, wuq_ref[:, ATT_W:], preferred_element_type=F32)
    ckv = proj(C_CKV, MLA_KV_RANK)
    ckvn = (ckv * lax.rsqrt(jnp.mean(ckv * ckv, axis=-1, keepdims=True) + EPS) * kvnw_ref[...]).astype(BF16)
    ka = jnp.dot(ckvn, wukv_ref[:, :ATT_W], preferred_element_type=F32)
    va = jnp.dot(ckvn, wukv_ref[:, ATT_W:], preferred_element_type=F32)
    kr = proj(C_KR, LANES) * csq + proj(C_KRS, LANES) * snq
    ones_lane = (lane == MLA_D_V).astype(F32)
    for hd in range(MLA_HEADS):
        sl = slice(hd * HEAD_PAD, (hd + 1) * HEAD_PAD)
        q_ref[hd] = ((qa[:, sl] * csq + qb[:, sl] * snq) * (MLA_SCALE * LOG2E)).astype(BF16)
        k_ref[hd] = (ka[:, sl] + kr).astype(BF16)
        v_ref[hd] = (va[:, sl] + ones_lane).astype(BF16)


def _inproj_call(x_all, nw, mod, w_ext, cs, sn, qnw, wuq, kvnw, wukv, n_lat_tiles):
    t_all, d = x_all.shape
    tm = ROW_TILE
    row = lambda i: (i, 0)
    const2 = lambda i: (0, 0)
    head = lambda i: (0, i, 0)
    cls = lambda i: (jnp.where(i < n_lat_tiles, 1, 0), 0, 0)
    out_shape = (
        jax.ShapeDtypeStruct((t_all, CONV_W), F32),
        jax.ShapeDtypeStruct((t_all, RET_QK_W), BF16),
        jax.ShapeDtypeStruct((t_all, RET_QK_W), BF16),
        jax.ShapeDtypeStruct((t_all, RET_W), BF16),
        jax.ShapeDtypeStruct((t_all, RET_W), F32),
        jax.ShapeDtypeStruct((t_all, RET_W), F32),
        jax.ShapeDtypeStruct((MLA_HEADS, t_all, HEAD_PAD), BF16),
        jax.ShapeDtypeStruct((MLA_HEADS, t_all, HEAD_PAD), BF16),
        jax.ShapeDtypeStruct((MLA_HEADS, t_all, HEAD_PAD), BF16),
    )
    out_specs = (
        pl.BlockSpec((tm, CONV_W), row),
        pl.BlockSpec((tm, RET_QK_W), row),
        pl.BlockSpec((tm, RET_QK_W), row),
        pl.BlockSpec((tm, RET_W), row),
        pl.BlockSpec((tm, RET_W), row),
        pl.BlockSpec((tm, RET_W), row),
        pl.BlockSpec((MLA_HEADS, tm, HEAD_PAD), head),
        pl.BlockSpec((MLA_HEADS, tm, HEAD_PAD), head),
        pl.BlockSpec((MLA_HEADS, tm, HEAD_PAD), head),
    )
    return pl.pallas_call(
        _inproj_kernel,
        grid=(t_all // tm,),
        in_specs=[pl.BlockSpec((tm, d), row),
                  pl.BlockSpec((1, d), const2),
                  pl.BlockSpec((None, 2, d), cls),
                  pl.BlockSpec((d, IN_EXT), const2),
                  pl.BlockSpec((tm, LANES), row),
                  pl.BlockSpec((tm, LANES), row),
                  pl.BlockSpec((1, MLA_Q_RANK), const2),
                  pl.BlockSpec((MLA_Q_RANK, 2 * ATT_W), const2),
                  pl.BlockSpec((1, MLA_KV_RANK), const2),
                  pl.BlockSpec((MLA_KV_RANK, 2 * ATT_W), const2)],
        out_specs=out_specs,
        out_shape=out_shape,
        compiler_params=_cparams(("arbitrary",)),
        name="inproj",
    )(x_all, nw, mod, w_ext, cs, sn, qnw, wuq, kvnw, wukv)


CONV_HALO = 16
CONV_SUB = 64
SUBLANES = 8


def _conv_kernel(prev_ref, cur_ref, next_ref, dw_ref, b_ref, lnw_ref, lnb_ref, o_ref, ext_ref, sh_ref, *,
                 n_lat_tiles):
    i = pl.program_id(0)
    n = pl.num_programs(0)
    tm = cur_ref.shape[0]
    seq_start = (i == 0) | (i == n_lat_tiles)
    seq_end = (i == n_lat_tiles - 1) | (i == n - 1)
    ext_ref[0:CONV_HALO, :] = jnp.where(seq_start, 0.0, prev_ref[...])
    ext_ref[CONV_HALO:CONV_HALO + tm, :] = cur_ref[...]
    ext_ref[CONV_HALO + tm:, :] = jnp.where(seq_end, 0.0, next_ref[...])
    span = sh_ref.shape[1]
    for ph in range(SUBLANES):
        sh_ref[ph] = ext_ref[pl.ds(ph, span), :]
    base = CONV_HALO - CONV_K // 2
    for r in range(tm // CONV_SUB):
        acc = jnp.zeros((CONV_SUB, CONV_W), F32) + b_ref[...]
        for k in range(CONV_K):
            off = base + k
            acc = acc + (sh_ref[off % SUBLANES, pl.ds(r * CONV_SUB + off - off % SUBLANES, CONV_SUB), :]
                         * dw_ref[k:k + 1, :])
        mu = jnp.mean(acc, axis=-1, keepdims=True)
        dlt = acc - mu
        var = jnp.mean(dlt * dlt, axis=-1, keepdims=True)
        y = dlt * lax.rsqrt(var + EPS) * lnw_ref[...] + lnb_ref[...]
        o_ref[r * CONV_SUB:(r + 1) * CONV_SUB, :] = _silu(y).astype(BF16)


def _conv_call(u, dw, b, lnw, lnb, n_lat_tiles):
    t_all = u.shape[0]
    tm = ROW_TILE
    hpt = tm // CONV_HALO
    n_halo = t_all // CONV_HALO
    const2 = lambda i: (0, 0)
    return pl.pallas_call(
        functools.partial(_conv_kernel, n_lat_tiles=n_lat_tiles),
        grid=(t_all // tm,),
        in_specs=[pl.BlockSpec((CONV_HALO, CONV_W), lambda i: (jnp.maximum(i * hpt - 1, 0), 0)),
                  pl.BlockSpec((tm, CONV_W), lambda i: (i, 0)),
                  pl.BlockSpec((CONV_HALO, CONV_W), lambda i: (jnp.minimum((i + 1) * hpt, n_halo - 1), 0)),
                  pl.BlockSpec((CONV_K, CONV_W), const2),
                  pl.BlockSpec((1, CONV_W), const2),
                  pl.BlockSpec((1, CONV_W), const2),
                  pl.BlockSpec((1, CONV_W), const2)],
        out_specs=pl.BlockSpec((tm, CONV_W), lambda i: (i, 0)),
        out_shape=jax.ShapeDtypeStruct((t_all, CONV_W), BF16),
        scratch_shapes=[pltpu.VMEM((tm + 2 * CONV_HALO, CONV_W), F32),
                        pltpu.VMEM((SUBLANES, tm + 2 * CONV_HALO - SUBLANES, CONV_W), F32)],
        compiler_params=_cparams(("arbitrary",)),
        name="conv",
    )(u, u, u, dw, b, lnw, lnb)


def _split_dot(x, a):
    hi = x.astype(BF16)
    lo = (x - hi.astype(F32)).astype(BF16)
    return (jnp.dot(hi, a, preferred_element_type=F32) + jnp.dot(lo, a, preferred_element_type=F32))


def _ret_direction(q, k, v, gate, r_ref, dmask, xi, zeta, gchunk, bdmask, avg, o_ref):
    lane_k = lax.broadcasted_iota(jnp.int32, (1, RET_QK_W), 1) // RET_DK
    lane_v = lax.broadcasted_iota(jnp.int32, (1, RET_W), 1) // RET_DV
    zero_k = jnp.zeros_like(k)
    zero_v = jnp.zeros_like(v)
    k_bd = jnp.concatenate([jnp.where(lane_k == hd, k, zero_k) for hd in range(RET_HEADS)], axis=0)
    v_bd = jnp.concatenate([jnp.where(lane_v == hd, v, zero_v) for hd in range(RET_HEADS)], axis=0)
    s = lax.dot_general(q, k_bd, (((1,), (1,)), ((), ())), preferred_element_type=F32)
    s = (s * dmask).astype(BF16)
    inner = jnp.dot(s, v_bd, preferred_element_type=F32)
    r = r_ref[...]
    cross = jnp.dot(q, r.astype(BF16), preferred_element_type=F32) * xi
    o = inner + cross
    kz = (k.astype(F32) * zeta).astype(BF16)
    ds = lax.dot_general(kz, v, (((0,), (0,)), ((), ())), preferred_element_type=F32)
    r_ref[...] = gchunk * r + ds * bdmask
    mu = jnp.dot(o.astype(BF16), avg, preferred_element_type=F32)
    dlt = o - mu
    var = jnp.dot((dlt * dlt).astype(BF16), avg, preferred_element_type=F32)
    o_ref[...] = (gate * (dlt * lax.rsqrt(var + RET_GN_EPS))).astype(BF16)


def _ret_kernel(qf_ref, kf_ref, vf_ref, gf_ref, qb_ref, kb_ref, vb_ref, gb_ref,
                dmf_ref, dmb_ref, xif_ref, xib_ref, ztf_ref, ztb_ref, gcf_ref, gcb_ref, bdm_ref, avg_ref,
                of_ref, ob_ref, rf_ref, rb_ref):
    @pl.when(pl.program_id(0) == 0)
    def _():
        rf_ref[...] = jnp.zeros_like(rf_ref)
        rb_ref[...] = jnp.zeros_like(rb_ref)

    bdm = bdm_ref[...]
    avg = avg_ref[...]
    _ret_direction(qf_ref[...], kf_ref[...], vf_ref[...], gf_ref[...], rf_ref, dmf_ref[...], xif_ref[...],
                   ztf_ref[...], gcf_ref[...], bdm, avg, of_ref)
    _ret_direction(qb_ref[...], kb_ref[...], vb_ref[...], gb_ref[...], rb_ref, dmb_ref[...], xib_ref[...],
                   ztb_ref[...], gcb_ref[...], bdm, avg, ob_ref)


def _ret_tables():
    c = RET_CHUNK
    gamma_f = 1.0 - 2.0 ** (-5.0 - jnp.arange(RET_HEADS, dtype=F32))
    gamma_b = gamma_f[::-1]
    idx = jnp.arange(c, dtype=F32)
    diff = idx[:, None] - idx[None, :]

    def tables(gamma, reverse):
        lg = jnp.log(gamma)
        d = -diff if reverse else diff
        dm = jnp.where(d[None] >= 0, jnp.exp(jnp.maximum(d, 0.0)[None] * lg[:, None, None]), 0.0)
        dm = jnp.transpose(dm, (1, 0, 2)).reshape(c, RET_HEADS * c)
        xi_e = (c - idx) if reverse else (idx + 1.0)
        zt_e = idx if reverse else (c - 1.0 - idx)
        xi = jnp.repeat(jnp.exp(xi_e[:, None] * lg[None, :]), RET_DV, axis=1)
        zt = jnp.repeat(jnp.exp(zt_e[:, None] * lg[None, :]), RET_DK, axis=1)
        gc = jnp.repeat(jnp.exp(c * lg), RET_DV)[None, :]
        return dm, xi, zt, gc

    dmf, xif, ztf, gcf = tables(gamma_f, False)
    dmb, xib, ztb, gcb = tables(gamma_b, True)
    hk = jnp.arange(RET_QK_W) // RET_DK
    hv = jnp.arange(RET_W) // RET_DV
    bdm = (hk[:, None] == hv[None, :]).astype(F32)
    avg = ((hv[:, None] == hv[None, :]).astype(F32) / RET_DV).astype(BF16)
    return (dmf, dmb, xif, xib, ztf, ztb, gcf, gcb, bdm, avg)


def _ret_call(rq, rk, rv, gf, gb, tabs, n_lat_chunks):
    t_all = rq.shape[0]
    c = RET_CHUNK
    n = t_all // c
    n_ctx_chunks = n - n_lat_chunks

    def fwd(i):
        return (jnp.where(i < n_ctx_chunks, n_lat_chunks + i, i - n_ctx_chunks), 0)

    def bwd(i):
        return (n - 1 - i, 0)

    const2 = lambda i: (0, 0)
    tab_specs = [pl.BlockSpec(t.shape, const2) for t in tabs]
    return pl.pallas_call(
        _ret_kernel,
        grid=(n,),
        in_specs=[pl.BlockSpec((c, RET_QK_W), fwd), pl.BlockSpec((c, RET_QK_W), fwd),
                  pl.BlockSpec((c, RET_W), fwd), pl.BlockSpec((c, RET_W), fwd),
                  pl.BlockSpec((c, RET_QK_W), bwd), pl.BlockSpec((c, RET_QK_W), bwd),
                  pl.BlockSpec((c, RET_W), bwd), pl.BlockSpec((c, RET_W), bwd)] + tab_specs,
        out_specs=(pl.BlockSpec((c, RET_W), fwd), pl.BlockSpec((c, RET_W), bwd)),
        out_shape=(jax.ShapeDtypeStruct((t_all, RET_W), BF16), jax.ShapeDtypeStruct((t_all, RET_W), BF16)),
        scratch_shapes=[pltpu.VMEM((RET_QK_W, RET_W), F32), pltpu.VMEM((RET_QK_W, RET_W), F32)],
        compiler_params=_cparams(("arbitrary",)),
        name="retention",
    )(rq, rk, rv, gf, rq, rk, rv, gb, *tabs)


ATT_SUB = 128
VT_ROWS = HEAD_PAD


def _attn_kernel(q_ref, k_ref, vt_ref, o_ref, s0_scr, s1_scr, p0_scr, p1_scr, acc_scr):
    q = q_ref[...]
    tq = q.shape[0]
    nkc, tk, _ = k_ref.shape
    nsub = tk // ATT_SUB

    s_bufs = (s0_scr, s1_scr)
    p_bufs = (p0_scr, p1_scr)

    def pv(c, par, alpha):
        part = jnp.dot(vt_ref[c], p_bufs[par][...], preferred_element_type=F32)
        acc_scr[...] = alpha * acc_scr[...] + part

    def step(c, par, m_old, m_blk, alpha_prev, with_scores, with_pv):
        if with_pv:
            pv(c - 1, 1 - par, alpha_prev)
        m_new = jnp.maximum(m_old, m_blk)
        alpha = jnp.exp2(m_old - m_new)
        mx = jnp.full((8, tq), -jnp.inf, F32)
        for j in range(nsub):
            rows = pl.ds(j * ATT_SUB, ATT_SUB)
            p_bufs[par][rows, :] = jnp.exp2(s_bufs[par][rows, :] - m_new).astype(BF16)
            if with_scores:
                mx = jnp.maximum(mx, score_rows(c + 1, 1 - par, rows))
        return m_new, jnp.max(mx, axis=0, keepdims=True), alpha

    def score_rows(c, par, rows):
        s = jnp.dot(k_ref[c, rows, :], qt, preferred_element_type=F32)
        s_bufs[par][rows, :] = s
        return jnp.max(s.reshape(ATT_SUB // 8, 8, tq), axis=0)

    def scores(c, par):
        mx = jnp.full((8, tq), -jnp.inf, F32)
        for j in range(nsub):
            mx = jnp.maximum(mx, score_rows(c, par, pl.ds(j * ATT_SUB, ATT_SUB)))
        return jnp.max(mx, axis=0, keepdims=True)

    acc_scr[...] = jnp.zeros_like(acc_scr)
    qt = q.astype(F32).T.astype(BF16)
    m = jnp.full((1, tq), -jnp.inf, F32)
    m_blk = scores(0, 0)
    alpha = jnp.ones((1, tq), F32)
    if nkc > 1:
        m, m_blk, alpha = step(0, 0, m, m_blk, alpha, True, False)
        def body(c, carry):
            return lax.cond(c % 2 == 1,
                            lambda cr: step(c, 1, *cr, True, True),
                            lambda cr: step(c, 0, *cr, True, True), carry)

        m, m_blk, alpha = lax.fori_loop(1, nkc - 1, body, (m, m_blk, alpha))
    last = nkc - 1
    m, _, alpha = step(last, last % 2, m, m_blk, alpha, False, nkc > 1)
    pv(last, last % 2, alpha)
    acc = acc_scr[...]
    out_t = acc / acc[MLA_D_V:MLA_D_V + 1, :]
    if VT_ROWS < HEAD_PAD:
        out_t = jnp.concatenate([out_t, jnp.zeros((HEAD_PAD - VT_ROWS, tq), F32)], axis=0)
    o_ref[...] = out_t.T.astype(BF16)


def _attn_kernel_into(q_ref, k_ref, vt_ref, dst_ref, o_ref, *scratch):
    del dst_ref
    _attn_kernel(q_ref, k_ref, vt_ref, o_ref, *scratch)


def _attn_call(q, k, v, tq, tk, n_q_blocks, out_rows, into=None, out_block0=0):
    nh = q.shape[0]
    t_k = k.shape[1]
    nkc = t_k // tk
    k4 = k.reshape(nh, nkc, tk, HEAD_PAD)
    vt4 = jnp.swapaxes(v[:, :, :VT_ROWS].reshape(nh, nkc, tk, VT_ROWS), 2, 3)
    in_specs = [pl.BlockSpec((None, tq, HEAD_PAD), lambda h, j: (h, j, 0)),
                pl.BlockSpec((None, nkc, tk, HEAD_PAD), lambda h, j: (h, 0, 0, 0)),
                pl.BlockSpec((None, nkc, VT_ROWS, tk), lambda h, j: (h, 0, 0, 0))]
    args = [q, k4, vt4]
    body, aliases = _attn_kernel, {}
    if into is not None:
        in_specs.append(pl.BlockSpec(memory_space=pl.ANY))
        args.append(into)
        body, aliases = _attn_kernel_into, {3: 0}
    return pl.pallas_call(
        body,
        grid=(nh, n_q_blocks),
        in_specs=in_specs,
        out_specs=pl.BlockSpec((tq, HEAD_PAD), lambda h, j: (out_block0 + j, h)),
        out_shape=jax.ShapeDtypeStruct((out_rows, nh * HEAD_PAD), BF16),
        scratch_shapes=[pltpu.VMEM((tk, tq), F32), pltpu.VMEM((tk, tq), F32),
                        pltpu.VMEM((tk, tq), BF16), pltpu.VMEM((tk, tq), BF16), pltpu.VMEM((VT_ROWS, tq), F32)],
        input_output_aliases=aliases,
        compiler_params=_cparams(("arbitrary", "arbitrary")),
        name="attention",
    )(*args)


def _top2_sum(a, b, c, d):
    hi1, lo1 = jnp.maximum(a, b), jnp.minimum(a, b)
    hi2, lo2 = jnp.maximum(c, d), jnp.minimum(c, d)
    return jnp.maximum(hi1, hi2) + jnp.maximum(jnp.minimum(hi1, hi2), jnp.maximum(lo1, lo2))


def _gates_t(aff, sel, gt_ref, grp_ref):
    rows = [sel[e:e + 1, :] for e in range(N_EXPERTS)]
    g_score = [_top2_sum(*rows[g * EXPERTS_PER_GROUP:(g + 1) * EXPERTS_PER_GROUP]) for g in range(N_GROUPS)]
    best = g_score[0]
    best_g = jnp.zeros_like(best, dtype=jnp.int32)
    for g in range(1, N_GROUPS):
        better = g_score[g] > best
        best = jnp.where(better, g_score[g], best)
        best_g = jnp.where(better, g, best_g)
    picked = []
    for e in range(N_EXPERTS):
        g = e // EXPERTS_PER_GROUP
        rank = jnp.zeros_like(best_g)
        for o in range(g * EXPERTS_PER_GROUP, (g + 1) * EXPERTS_PER_GROUP):
            if o == e:
                continue
            ahead = (rows[o] >= rows[e]) if o < e else (rows[o] > rows[e])
            rank = rank + jnp.where(ahead, 1, 0)
        picked.append(jnp.where(best_g == g, rank, 2) < 2)
    w = [jnp.where(picked[e], aff[e:e + 1, :], 0.0) for e in range(N_EXPERTS)]
    total = w[0]
    for e in range(1, N_EXPERTS):
        total = total + w[e]
    for e in range(N_EXPERTS):
        gt_ref[e:e + 1, :] = w[e] / total
    grp_ref[...] = best_g


OUT_HALF = D_MODEL // 2


def _outproj_kernel(x_ref, conv_ref, of_ref, ob_ref, att_ref, w_ref, mod_ref, nw_ref, rw_ref, rb_ref,
                    x1_ref, h2_ref, gt_ref, grp_ref):
    ret = (of_ref[...].astype(F32) + ob_ref[...].astype(F32)).astype(BF16)
    mix = jnp.concatenate([conv_ref[...], ret, att_ref[...]], axis=1)
    halves = []
    ssq = 0.0
    for hf in range(2):
        cols = slice(hf * OUT_HALF, (hf + 1) * OUT_HALF)
        o = jnp.dot(mix, w_ref[:, cols], preferred_element_type=F32)
        x1 = x_ref[:, cols] + mod_ref[2:3, cols] * o
        x1_ref[:, cols] = x1
        ssq = ssq + jnp.sum(x1 * x1, axis=-1, keepdims=True)
        halves.append(x1)
    inv = lax.rsqrt(ssq * (1.0 / D_MODEL) + EPS)
    logit_parts = 0.0
    for hf in range(2):
        cols = slice(hf * OUT_HALF, (hf + 1) * OUT_HALF)
        h2 = (halves[hf] * inv * nw_ref[:, cols]) * (1.0 + mod_ref[4:5, cols]) + mod_ref[3:4, cols]
        hi = h2.astype(BF16)
        h2_ref[:, cols] = hi
        lo = (h2 - hi.astype(F32)).astype(BF16)
        logit_parts = (logit_parts + jnp.dot(hi, rw_ref[cols, :], preferred_element_type=F32)
                       + jnp.dot(lo, rw_ref[cols, :], preferred_element_type=F32))
    lt = logit_parts.T
    logits = lt[0:N_EXPERTS, :] + lt[N_EXPERTS:2 * N_EXPERTS, :]
    aff = jax.nn.sigmoid(logits)
    _gates_t(aff, aff + rb_ref[...], gt_ref, grp_ref)


def _outproj_call(x_all, conv, of, ob, att, w_ext, mod, nw, rw3, rb, n_lat_tiles, n_tiles):
    d = x_all.shape[1]
    tm = ROW_TILE
    row = lambda i: (i, 0)
    const2 = lambda i: (0, 0)
    cls = lambda i: (jnp.where(i < n_lat_tiles, 1, 0), 0, 0)
    return pl.pallas_call(
        _outproj_kernel,
        grid=(n_tiles,),
        in_specs=[pl.BlockSpec((tm, d), row),
                  pl.BlockSpec((tm, CONV_W), row),
                  pl.BlockSpec((tm, RET_W), row),
                  pl.BlockSpec((tm, RET_W), row),
                  pl.BlockSpec((tm, ATT_W), row),
                  pl.BlockSpec(w_ext.shape, const2),
                  pl.BlockSpec((None, 8, d), cls),
                  pl.BlockSpec((1, d), const2),
                  pl.BlockSpec((d, LANES), const2),
                  pl.BlockSpec((N_EXPERTS, 1), const2)],
        out_specs=(pl.BlockSpec((tm, d), row), pl.BlockSpec((tm, d), row),
                   pl.BlockSpec((N_EXPERTS, tm), lambda i: (0, i)), pl.BlockSpec((1, tm), lambda i: (0, i))),
        out_shape=(jax.ShapeDtypeStruct((n_tiles * tm, d), F32), jax.ShapeDtypeStruct((n_tiles * tm, d), BF16),
                   jax.ShapeDtypeStruct((N_EXPERTS, n_tiles * tm), F32),
                   jax.ShapeDtypeStruct((1, n_tiles * tm), jnp.int32)),
        compiler_params=_cparams(("arbitrary",)),
        name="outproj",
    )(x_all, conv, of, ob, att, w_ext, mod, nw, rw3, rb)


MOE_DENSE_ROWS = 256


def _group_mlp(xb, gate4, wgu_ref, wd_ref):
    gu = jnp.dot(xb, wgu_ref[...], preferred_element_type=F32)
    parts = []
    for k in range(EXPERTS_PER_GROUP):
        a = gu[:, 2 * D_EXPERT * k:2 * D_EXPERT * k + D_EXPERT]
        b = gu[:, 2 * D_EXPERT * k + D_EXPERT:2 * D_EXPERT * (k + 1)]
        parts.append((_silu(a) * b * gate4[:, k:k + 1]).astype(BF16))
    return jnp.dot(jnp.concatenate(parts, axis=1), wd_ref[...], preferred_element_type=F32)


def _moe_kernel(h_ref, gate8_ref, grp_ref, x1_ref, g2_ref, tri_ref, wgu_ref, wd_ref, *rest, cap, final_norm, into):
    rest = list(rest)
    fw_ref = rest.pop(0) if final_norm else None
    if into:
        rest.pop(0)
    o_ref, cnt_scr = rest
    tm = h_ref.shape[0]
    grp = grp_ref[...]
    o_ref[...] = jnp.zeros_like(o_ref)
    rows8 = lax.broadcasted_iota(jnp.int32, (SUBLANES, tm), 0)
    member8 = jnp.where(rows8 == grp, 1.0, 0.0).astype(BF16)
    cnt_scr[...] = jnp.dot(member8, tri_ref[...], preferred_element_type=F32)

    def group(g, carry):
        incl = cnt_scr[pl.ds(g, 1), :]
        count = jnp.max(incl)
        gate8_g, wgu_g, wd_g = gate8_ref.at[g], wgu_ref.at[g], wd_ref.at[g]

        @pl.when(count <= cap)
        def _():
            pos = jnp.where(grp == g, incl.astype(jnp.int32) - 1, -1)
            slot = lax.broadcasted_iota(jnp.int32, (cap, tm), 0)
            onehot = jnp.where(slot == pos, 1.0, 0.0).astype(BF16)
            xg = jnp.dot(onehot, h_ref[...], preferred_element_type=F32).astype(BF16)
            g8 = jnp.dot(onehot, gate8_g[...], preferred_element_type=F32)
            y = _group_mlp(xg, g8[:, :EXPERTS_PER_GROUP] + g8[:, EXPERTS_PER_GROUP:], wgu_g, wd_g)
            o_ref[...] += lax.dot_general(onehot, y.astype(BF16), (((0,), (0,)), ((), ())),
                                          preferred_element_type=F32)

        @pl.when(count > cap)
        def _():
            def chunk(ci, inner):
                rows = pl.ds(pl.multiple_of(ci * MOE_DENSE_ROWS, MOE_DENSE_ROWS), MOE_DENSE_ROWS)
                g8 = gate8_g[rows, :].astype(F32)
                o_ref[rows, :] += _group_mlp(h_ref[rows, :],
                                             g8[:, :EXPERTS_PER_GROUP] + g8[:, EXPERTS_PER_GROUP:], wgu_g, wd_g)
                return inner

            lax.fori_loop(0, tm // MOE_DENSE_ROWS, chunk, 0)

        return carry

    lax.fori_loop(0, N_GROUPS, group, 0)
    x2 = x1_ref[...] + g2_ref[...] * o_ref[...]
    if final_norm:
        x2 = x2 * lax.rsqrt(jnp.mean(x2 * x2, axis=-1, keepdims=True) + EPS) * fw_ref[...]
    o_ref[...] = x2


def _moe_call(h2, gate8, grp, x1, g2, wgu, wd, tm, block0, n_tiles, out_rows, final_w=None, into=None):
    d = x1.shape[1]
    cap = tm * 5 // 16
    assert cap % 16 == 0 and tm % MOE_DENSE_ROWS == 0
    tri = (jnp.arange(tm)[:, None] <= jnp.arange(tm)[None, :]).astype(BF16)
    rows = lambda i: (block0 + i, 0)
    const2 = lambda i: (0, 0)
    const3 = lambda i: (0, 0, 0)
    once = pl.Buffered(1)
    in_specs = [pl.BlockSpec((tm, d), rows),
                pl.BlockSpec((N_GROUPS, tm, 2 * EXPERTS_PER_GROUP), lambda i: (0, block0 + i, 0)),
                pl.BlockSpec((1, tm), lambda i: (0, block0 + i)),
                pl.BlockSpec((tm, d), rows),
                pl.BlockSpec((1, d), const2),
                pl.BlockSpec((tm, tm), const2, pipeline_mode=once),
                pl.BlockSpec(wgu.shape, const3, pipeline_mode=once),
                pl.BlockSpec(wd.shape, const3, pipeline_mode=once)]
    args = [h2, gate8, grp, x1, g2, tri, wgu, wd]
    aliases = {}
    if final_w is not None:
        in_specs.append(pl.BlockSpec((1, d), const2))
        args.append(final_w)
    if into is not None:
        in_specs.append(pl.BlockSpec(memory_space=pl.ANY))
        aliases = {len(args): 0}
        args.append(into)
    return pl.pallas_call(
        functools.partial(_moe_kernel, cap=cap, final_norm=final_w is not None, into=into is not None),
        grid=(n_tiles,),
        in_specs=in_specs,
        out_specs=pl.BlockSpec((tm, d), rows),
        out_shape=jax.ShapeDtypeStruct((out_rows, d), F32),
        scratch_shapes=[pltpu.VMEM((SUBLANES, tm), F32)],
        input_output_aliases=aliases,
        compiler_params=_cparams(("arbitrary",)),
        name="experts",
    )(*args)


_SWAP32 = np.concatenate([np.arange(8, 16), np.arange(0, 8), np.arange(24, 32), np.arange(16, 24)])


def _pad_cols(w, width):
    return jnp.pad(w, ((0, 0), (0, width - w.shape[1])))


def _in_weight(w_in):
    sizes = (2 * CONV_W, RET_QK_W, RET_QK_W, RET_W, RET_W, RET_W, MLA_Q_RANK, MLA_KV_RANK, MLA_D_ROPE)
    offs = np.concatenate([[0], np.cumsum(sizes)])
    conv, rq, rk, rv, gf, gb, cq, ckv, kr = [w_in[:, offs[i]:offs[i + 1]] for i in range(len(sizes))]
    swap192 = np.concatenate([h * ROPE_DIM + _SWAP32 for h in range(RET_HEADS)])
    d = w_in.shape[0]

    def place_rope(w):
        return jnp.concatenate([jnp.zeros((d, MLA_D_NOPE), w.dtype), w,
                                jnp.zeros((d, LANES - MLA_D_NOPE - MLA_D_ROPE), w.dtype)], axis=1)

    ext = jnp.concatenate([
        conv, _pad_cols(rq, 256), _pad_cols(rq[:, swap192], 256), _pad_cols(rk, 256), _pad_cols(rk[:, swap192], 256),
        rv, gf, gb, _pad_cols(cq, 256), ckv, place_rope(kr), place_rope(kr[:, _SWAP32])], axis=1)
    assert ext.shape[1] == IN_EXT
    return ext.astype(BF16)


def _uq_weight(w_uq):
    r = w_uq.shape[0]
    w = w_uq.reshape(r, MLA_HEADS, MLA_D_NOPE + MLA_D_ROPE)
    nope, rope = w[..., :MLA_D_NOPE], w[..., MLA_D_NOPE:]
    zpad = jnp.zeros((r, MLA_HEADS, HEAD_PAD - MLA_D_NOPE - MLA_D_ROPE), w.dtype)
    main = jnp.concatenate([nope, rope, zpad], axis=-1).reshape(r, ATT_W)
    part = jnp.concatenate([jnp.zeros_like(nope), rope[..., _SWAP32], zpad], axis=-1).reshape(r, ATT_W)
    return jnp.concatenate([main, part], axis=1).astype(BF16)


def _ukv_weight(w_ukv):
    r = w_ukv.shape[0]
    w = w_ukv.reshape(r, MLA_HEADS, MLA_D_NOPE + MLA_D_V)
    zpad = jnp.zeros((r, MLA_HEADS, HEAD_PAD - MLA_D_NOPE), w.dtype)
    kpart = jnp.concatenate([w[..., :MLA_D_NOPE], zpad], axis=-1).reshape(r, ATT_W)
    vpart = jnp.concatenate([w[..., MLA_D_NOPE:], zpad], axis=-1).reshape(r, ATT_W)
    return jnp.concatenate([kpart, vpart], axis=1).astype(BF16)


def _out_weight(w_out):
    d = w_out.shape[1]
    conv, ret = w_out[:CONV_W], w_out[CONV_W:CONV_W + RET_W]
    att = w_out[CONV_W + RET_W:].reshape(MLA_HEADS, MLA_D_V, d)
    att = jnp.pad(att, ((0, 0), (0, HEAD_PAD - MLA_D_V), (0, 0))).reshape(ATT_W, d)
    return jnp.concatenate([conv, ret, att], axis=0).astype(BF16)


def _rope_tables(n_ctx, seq):
    pos = jnp.arange(seq)
    r = (pos // GRID_W).astype(F32)
    cl = (pos % GRID_W).astype(F32)
    inv = ROPE_BASE ** (-jnp.arange(ROPE_PAIRS, dtype=F32) / ROPE_PAIRS)
    ar, ac = r[:, None] * inv, cl[:, None] * inv
    cs32 = jnp.concatenate([jnp.cos(ar), jnp.cos(ar), jnp.cos(ac), jnp.cos(ac)], axis=1)
    sn32 = jnp.concatenate([-jnp.sin(ar), jnp.sin(ar), -jnp.sin(ac), jnp.sin(ac)], axis=1)
    cs = jnp.concatenate([cs32, jnp.ones((n_ctx, ROPE_DIM), F32)], axis=0)
    sn = jnp.concatenate([sn32, jnp.zeros((n_ctx, ROPE_DIM), F32)], axis=0)
    return jnp.tile(cs, (1, LANES // ROPE_DIM)), jnp.tile(sn, (1, LANES // ROPE_DIM))


def _router_weight(router_w):
    hi = router_w.astype(BF16)
    lo = (router_w - hi.astype(F32)).astype(BF16)
    return jnp.pad(jnp.concatenate([hi, lo], axis=1), ((0, 0), (0, LANES - 2 * N_EXPERTS)))


def _group_gates(gates_t):
    t = gates_t.shape[1]
    g4 = jnp.transpose(gates_t.reshape(N_GROUPS, EXPERTS_PER_GROUP, t), (0, 2, 1))
    hi = g4.astype(BF16)
    lo = (g4 - hi.astype(F32)).astype(BF16)
    return jnp.concatenate([hi, lo], axis=-1)


def _group_weights(w_gate, w_up, w_down):
    e, d, f = w_gate.shape
    gu = jnp.concatenate([w_gate, w_up], axis=-1).astype(BF16)
    gu = gu.reshape(N_GROUPS, EXPERTS_PER_GROUP, d, 2 * f)
    gu = jnp.transpose(gu, (0, 2, 1, 3)).reshape(N_GROUPS, d, EXPERTS_PER_GROUP * 2 * f)
    wd = w_down.astype(BF16).reshape(N_GROUPS, EXPERTS_PER_GROUP * f, d)
    return gu, wd


def kernel(x, c, ctx, c_ctx, ada_w, ada_b, norm1_w, norm2_w, w_in, conv_dw, conv_b, conv_ln_w, conv_ln_b,
           mla_q_norm_w, mla_w_uq, mla_kv_norm_w, mla_w_ukv, w_out, router_w, router_bias,
           moe_w_gate, moe_w_up, moe_w_down, final_norm_w):
    batch, seq, d = x.shape
    n_ctx = ctx.shape[1]
    depth = ada_w.shape[0]
    t_all = seq + n_ctx
    assert batch == 1 and d == D_MODEL
    assert n_ctx % ROW_TILE == 0 and seq % ROW_TILE == 0 and n_ctx <= MOE_TILE and seq % MOE_TILE == 0
    assert seq % ATT_TQ == 0 and t_all % ATT_TK == 0 and n_ctx % RET_CHUNK == 0 and seq % n_ctx == 0
    n_lat_tiles = seq // ROW_TILE
    n_all_tiles = t_all // ROW_TILE

    cond = jnp.zeros((8, d), F32).at[0].set(c_ctx).at[1].set(c[0])
    mod = _ada_call(cond, ada_w, ada_b)[:, :2, :].reshape(depth, 2, 6, d)
    mod8 = jnp.pad(mod, ((0, 0), (0, 0), (0, 2), (0, 0)))
    cs, sn = _rope_tables(n_ctx, seq)
    ret_tabs = _ret_tables()
    rw3 = _router_weight(router_w)
    rb = router_bias.reshape(N_EXPERTS, 1)

    x_all = jnp.concatenate([x[0], ctx[0]], axis=0)
    for l in range(depth):
        last = l == depth - 1
        u, rq, rk, rv, gf, gb, q_att, k_att, v_att = _inproj_call(
            x_all, norm1_w[l][None, :], mod8[l][:, :2, :], _in_weight(w_in[l]), cs, sn,
            mla_q_norm_w[l][None, :], _uq_weight(mla_w_uq[l]), mla_kv_norm_w[l][None, :], _ukv_weight(mla_w_ukv[l]),
            n_lat_tiles)
        conv = _conv_call(u, conv_dw[l], conv_b[l][None, :], conv_ln_w[l][None, :], conv_ln_b[l][None, :],
                          n_lat_tiles)
        of, ob = _ret_call(rq, rk, rv, gf, gb, ret_tabs, seq // RET_CHUNK)
        att = _attn_call(q_att, k_att, v_att, ATT_TQ, ATT_TK, seq // ATT_TQ, seq if last else t_all)
        if not last:
            tqc = min(ATT_TQ, n_ctx)
            att = _attn_call(q_att[:, seq:], k_att[:, seq:], v_att[:, seq:], tqc, n_ctx, n_ctx // tqc, t_all,
                             into=att, out_block0=seq // tqc)
        n_tiles = n_lat_tiles if last else n_all_tiles
        x1, h2, gates_t, grp = _outproj_call(x_all, conv, of, ob, att, _out_weight(w_out[l]), mod8[l],
                                             norm2_w[l][None, :], rw3, rb, n_lat_tiles, n_tiles)
        gate8 = _group_gates(gates_t)
        wgu, wd = _group_weights(moe_w_gate[l], moe_w_up[l], moe_w_down[l])
        x_all = _moe_call(h2, gate8, grp, x1, mod[l, 1, 5][None, :], wgu, wd, MOE_TILE, 0, seq // MOE_TILE,
                          seq if last else t_all, final_w=final_norm_w[None, :] if last else None)
        if not last:
            x_all = _moe_call(h2, gate8, grp, x1, mod[l, 0, 5][None, :], wgu, wd, n_ctx, seq // n_ctx, 1, t_all,
                              into=x_all)
    return x_all[None]
```

```python
import functools
import math

import numpy as np
import jax
import jax.numpy as jnp
from jax import lax
from jax.experimental import pallas as pl
from jax.experimental.pallas import tpu as pltpu

F32 = jnp.float32
BF16 = jnp.bfloat16

D_MODEL = 1024
GRID_W = 64
CONV_W = 256
CONV_K = 31
RET_HEADS = 6
RET_DK = 32
RET_DV = 64
RET_QK_W = RET_HEADS * RET_DK
RET_W = RET_HEADS * RET_DV
RET_CHUNK = 128
RET_GN_EPS = 1e-5
MLA_HEADS = 6
MLA_Q_RANK = 192
MLA_KV_RANK = 128
MLA_D_NOPE = 64
MLA_D_ROPE = 32
MLA_D_V = 64
MLA_SCALE = (MLA_D_NOPE + MLA_D_ROPE) ** -0.5
ROPE_DIM = 32
ROPE_PAIRS = ROPE_DIM // 4
ROPE_BASE = 10000.0
N_EXPERTS = 16
N_GROUPS = 4
EXPERTS_PER_GROUP = N_EXPERTS // N_GROUPS
D_EXPERT = 256
EPS = 1e-6

LANES = 128
HEAD_PAD = LANES
ATT_W = MLA_HEADS * HEAD_PAD
LOG2E = math.log2(math.e)

C_CONV = 0
C_RQ = 512
C_RQS = 768
C_RK = 1024
C_RKS = 1280
C_RV = 1536
C_RGF = 1920
C_RGB = 2304
C_CQ = 2688
C_CKV = 2944
C_KR = 3072
C_KRS = 3200
IN_EXT = 3328

ROW_TILE = 256
MOE_TILE = 512
ATT_TQ = 1024
ATT_TK = 1280
VMEM_LIMIT = 48 * 1024 * 1024


def _cparams(sem):
    return pltpu.CompilerParams(dimension_semantics=sem, vmem_limit_bytes=VMEM_LIMIT)


def _silu(x):
    return x * jax.nn.sigmoid(x)


def _ada_kernel(c_ref, w_ref, b_ref, o_ref):
    s = _silu(c_ref[...])
    o_ref[...] = jnp.dot(s, w_ref[...], preferred_element_type=F32,
                         precision=lax.Precision.HIGHEST) + b_ref[...]


def _ada_call(cond, ada_w, ada_b):
    depth, d, n = ada_w.shape
    tn = 1536
    return pl.pallas_call(
        _ada_kernel,
        grid=(depth, n // tn),
        in_specs=[pl.BlockSpec((8, d), lambda l, j: (0, 0)),
                  pl.BlockSpec((None, d, tn), lambda l, j: (l, 0, j)),
                  pl.BlockSpec((None, 1, tn), lambda l, j: (l, 0, j))],
        out_specs=pl.BlockSpec((None, 8, tn), lambda l, j: (l, 0, j)),
        out_shape=jax.ShapeDtypeStruct((depth, 8, n), F32),
        compiler_params=_cparams(("arbitrary", "arbitrary")),
        name="adaln",
    )(cond, ada_w, ada_b.reshape(depth, 1, n))


def _inproj_kernel(x_ref, xc_ref, nw_ref, mod_ref, w_ref, cs_ref, sn_ref, qnw_ref, wuq_ref, kvnw_ref, wukv_ref,
                   u_ref, rq_ref, rk_ref, rv_ref, gf_ref, gb_ref, q_ref, k_ref, v_ref, *, n_lat_tiles):
    x = jnp.where(pl.program_id(0) < n_lat_tiles, x_ref[...], xc_ref[...])
    y = x * lax.rsqrt(jnp.mean(x * x, axis=-1, keepdims=True) + EPS) * nw_ref[...]
    h = (y * (1.0 + mod_ref[1:2, :]) + mod_ref[0:1, :]).astype(BF16)

    def proj(c0, width):
        return jnp.dot(h, w_ref[:, c0:c0 + width], preferred_element_type=F32)

    cs = cs_ref[...]
    sn = sn_ref[...]
    lane = lax.broadcasted_iota(jnp.int32, (1, LANES), 1)
    rope_lanes = (lane >= MLA_D_NOPE) & (lane < MLA_D_NOPE + MLA_D_ROPE)
    csq = jnp.where(rope_lanes, cs, 1.0)
    snq = jnp.where(rope_lanes, sn, 0.0)

    ag = proj(C_CONV, 2 * CONV_W)
    u_ref[...] = ag[:, :CONV_W] * jax.nn.sigmoid(ag[:, CONV_W:])

    cs192 = jnp.concatenate([cs, cs[:, :RET_QK_W - LANES]], axis=1)
    sn192 = jnp.concatenate([sn, sn[:, :RET_QK_W - LANES]], axis=1)
    rq = proj(C_RQ, 256)[:, :RET_QK_W] * cs192 + proj(C_RQS, 256)[:, :RET_QK_W] * sn192
    rq_ref[...] = rq.astype(BF16)
    rk = proj(C_RK, 256)[:, :RET_QK_W] * cs192 + proj(C_RKS, 256)[:, :RET_QK_W] * sn192
    rk_ref[...] = (rk * (RET_DK ** -0.5)).astype(BF16)
    rv_ref[...] = proj(C_RV, RET_W).astype(BF16)
    gf_ref[...] = _silu(proj(C_RGF, RET_W))
    gb_ref[...] = _silu(proj(C_RGB, RET_W))

    cq = proj(C_CQ, 256)[:, :MLA_Q_RANK]
    cqn = (cq * lax.rsqrt(jnp.mean(cq * cq, axis=-1, keepdims=True) + EPS) * qnw_ref[...]).astype(BF16)
    qa = jnp.dot(cqn, wuq_ref[:, :ATT_W], preferred_element_type=F32)
    qb = jnp.dot(cqn, wuq_ref[:, ATT_W:], preferred_element_type=F32)
    ckv = proj(C_CKV, MLA_KV_RANK)
    ckvn = (ckv * lax.rsqrt(jnp.mean(ckv * ckv, axis=-1, keepdims=True) + EPS) * kvnw_ref[...]).astype(BF16)
    ka = jnp.dot(ckvn, wukv_ref[:, :ATT_W], preferred_element_type=F32)
    va = jnp.dot(ckvn, wukv_ref[:, ATT_W:], preferred_element_type=F32)
    kr = proj(C_KR, LANES) * csq + proj(C_KRS, LANES) * snq
    ones_lane = (lane == MLA_D_V).astype(F32)
    for hd in range(MLA_HEADS):
        sl = slice(hd * HEAD_PAD, (hd + 1) * HEAD_PAD)
        q_ref[hd] = ((qa[:, sl] * csq + qb[:, sl] * snq) * (MLA_SCALE * LOG2E)).astype(BF16)
        k_ref[hd] = (ka[:, sl] + kr).astype(BF16)
        v_ref[hd] = (va[:, sl] + ones_lane).astype(BF16)


def _inproj_call(x_lat, x_ctx, ctx_block0, t_all, nw, mod, w_ext, cs, sn, qnw, wuq, kvnw, wukv, n_lat_tiles):
    d = x_lat.shape[1]
    tm = ROW_TILE
    row = lambda i: (i, 0)
    const2 = lambda i: (0, 0)
    head = lambda i: (0, i, 0)
    cls = lambda i: (jnp.where(i < n_lat_tiles, 1, 0), 0, 0)
    out_shape = (
        jax.ShapeDtypeStruct((t_all, CONV_W), F32),
        jax.ShapeDtypeStruct((t_all, RET_QK_W), BF16),
        jax.ShapeDtypeStruct((t_all, RET_QK_W), BF16),
        jax.ShapeDtypeStruct((t_all, RET_W), BF16),
        jax.ShapeDtypeStruct((t_all, RET_W), F32),
        jax.ShapeDtypeStruct((t_all, RET_W), F32),
        jax.ShapeDtypeStruct((MLA_HEADS, t_all, HEAD_PAD), BF16),
        jax.ShapeDtypeStruct((MLA_HEADS, t_all, HEAD_PAD), BF16),
        jax.ShapeDtypeStruct((MLA_HEADS, t_all, HEAD_PAD), BF16),
    )
    out_specs = (
        pl.BlockSpec((tm, CONV_W), row),
        pl.BlockSpec((tm, RET_QK_W), row),
        pl.BlockSpec((tm, RET_QK_W), row),
        pl.BlockSpec((tm, RET_W), row),
        pl.BlockSpec((tm, RET_W), row),
        pl.BlockSpec((tm, RET_W), row),
        pl.BlockSpec((MLA_HEADS, tm, HEAD_PAD), head),
        pl.BlockSpec((MLA_HEADS, tm, HEAD_PAD), head),
        pl.BlockSpec((MLA_HEADS, tm, HEAD_PAD), head),
    )
    return pl.pallas_call(
        functools.partial(_inproj_kernel, n_lat_tiles=n_lat_tiles),
        grid=(t_all // tm,),
        in_specs=[pl.BlockSpec((tm, d), lambda i: (jnp.minimum(i, n_lat_tiles - 1), 0)),
                  pl.BlockSpec((tm, d), lambda i: (ctx_block0 + jnp.maximum(i - n_lat_tiles, 0), 0)),
                  pl.BlockSpec((1, d), const2),
                  pl.BlockSpec((None, 2, d), cls),
                  pl.BlockSpec((d, IN_EXT), const2),
                  pl.BlockSpec((tm, LANES), row),
                  pl.BlockSpec((tm, LANES), row),
                  pl.BlockSpec((1, MLA_Q_RANK), const2),
                  pl.BlockSpec((MLA_Q_RANK, 2 * ATT_W), const2),
                  pl.BlockSpec((1, MLA_KV_RANK), const2),
                  pl.BlockSpec((MLA_KV_RANK, 2 * ATT_W), const2)],
        out_specs=out_specs,
        out_shape=out_shape,
        compiler_params=_cparams(("arbitrary",)),
        name="inproj",
    )(x_lat, x_ctx, nw, mod, w_ext, cs, sn, qnw, wuq, kvnw, wukv)


CONV_HALO = 16
CONV_SUB = 64
SUBLANES = 8


def _conv_kernel(prev_ref, cur_ref, next_ref, dw_ref, b_ref, lnw_ref, lnb_ref, o_ref, ext_ref, sh_ref, *,
                 n_lat_tiles):
    i = pl.program_id(0)
    n = pl.num_programs(0)
    tm = cur_ref.shape[0]
    seq_start = (i == 0) | (i == n_lat_tiles)
    seq_end = (i == n_lat_tiles - 1) | (i == n - 1)
    ext_ref[0:CONV_HALO, :] = jnp.where(seq_start, 0.0, prev_ref[...])
    ext_ref[CONV_HALO:CONV_HALO + tm, :] = cur_ref[...]
    ext_ref[CONV_HALO + tm:, :] = jnp.where(seq_end, 0.0, next_ref[...])
    span = sh_ref.shape[1]
    for ph in range(SUBLANES):
        sh_ref[ph] = ext_ref[pl.ds(ph, span), :]
    base = CONV_HALO - CONV_K // 2
    for r in range(tm // CONV_SUB):
        acc = jnp.zeros((CONV_SUB, CONV_W), F32) + b_ref[...]
        for k in range(CONV_K):
            off = base + k
            acc = acc + (sh_ref[off % SUBLANES, pl.ds(r * CONV_SUB + off - off % SUBLANES, CONV_SUB), :]
                         * dw_ref[k:k + 1, :])
        mu = jnp.mean(acc, axis=-1, keepdims=True)
        dlt = acc - mu
        var = jnp.mean(dlt * dlt, axis=-1, keepdims=True)
        y = dlt * lax.rsqrt(var + EPS) * lnw_ref[...] + lnb_ref[...]
        o_ref[r * CONV_SUB:(r + 1) * CONV_SUB, :] = _silu(y).astype(BF16)


def _conv_call(u, dw, b, lnw, lnb, n_lat_tiles):
    t_all = u.shape[0]
    tm = ROW_TILE
    hpt = tm // CONV_HALO
    n_halo = t_all // CONV_HALO
    const2 = lambda i: (0, 0)
    return pl.pallas_call(
        functools.partial(_conv_kernel, n_lat_tiles=n_lat_tiles),
        grid=(t_all // tm,),
        in_specs=[pl.BlockSpec((CONV_HALO, CONV_W), lambda i: (jnp.maximum(i * hpt - 1, 0), 0)),
                  pl.BlockSpec((tm, CONV_W), lambda i: (i, 0)),
                  pl.BlockSpec((CONV_HALO, CONV_W), lambda i: (jnp.minimum((i + 1) * hpt, n_halo - 1), 0)),
                  pl.BlockSpec((CONV_K, CONV_W), const2),
                  pl.BlockSpec((1, CONV_W), const2),
                  pl.BlockSpec((1, CONV_W), const2),
                  pl.BlockSpec((1, CONV_W), const2)],
        out_specs=pl.BlockSpec((tm, CONV_W), lambda i: (i, 0)),
        out_shape=jax.ShapeDtypeStruct((t_all, CONV_W), BF16),
        scratch_shapes=[pltpu.VMEM((tm + 2 * CONV_HALO, CONV_W), F32),
                        pltpu.VMEM((SUBLANES, tm + 2 * CONV_HALO - SUBLANES, CONV_W), F32)],
        compiler_params=_cparams(("arbitrary",)),
        name="conv",
    )(u, u, u, dw, b, lnw, lnb)


def _split_dot(x, a):
    hi = x.astype(BF16)
    lo = (x - hi.astype(F32)).astype(BF16)
    return (jnp.dot(hi, a, preferred_element_type=F32) + jnp.dot(lo, a, preferred_element_type=F32))


def _ret_direction(q, k, v, gate, r_ref, dmask, xi, zeta, gchunk, bdmask, avg, o_ref):
    lane_k = lax.broadcasted_iota(jnp.int32, (1, RET_QK_W), 1) // RET_DK
    lane_v = lax.broadcasted_iota(jnp.int32, (1, RET_W), 1) // RET_DV
    zero_k = jnp.zeros_like(k)
    zero_v = jnp.zeros_like(v)
    k_bd = jnp.concatenate([jnp.where(lane_k == hd, k, zero_k) for hd in range(RET_HEADS)], axis=0)
    v_bd = jnp.concatenate([jnp.where(lane_v == hd, v, zero_v) for hd in range(RET_HEADS)], axis=0)
    s = lax.dot_general(q, k_bd, (((1,), (1,)), ((), ())), preferred_element_type=F32)
    s = (s * dmask).astype(BF16)
    inner = jnp.dot(s, v_bd, preferred_element_type=F32)
    r = r_ref[...]
    cross = jnp.dot(q, r.astype(BF16), preferred_element_type=F32) * xi
    o = inner + cross
    kz = (k.astype(F32) * zeta).astype(BF16)
    ds = lax.dot_general(kz, v, (((0,), (0,)), ((), ())), preferred_element_type=F32)
    r_ref[...] = gchunk * r + ds * bdmask
    mu = jnp.dot(o.astype(BF16), avg, preferred_element_type=F32)
    dlt = o - mu
    var = jnp.dot((dlt * dlt).astype(BF16), avg, preferred_element_type=F32)
    o_ref[...] = (gate * (dlt * lax.rsqrt(var + RET_GN_EPS))).astype(BF16)


def _ret_kernel(qf_ref, kf_ref, vf_ref, gf_ref, qb_ref, kb_ref, vb_ref, gb_ref,
                dmf_ref, dmb_ref, xif_ref, xib_ref, ztf_ref, ztb_ref, gcf_ref, gcb_ref, bdm_ref, avg_ref,
                of_ref, ob_ref, rf_ref, rb_ref):
    @pl.when(pl.program_id(0) == 0)
    def _():
        rf_ref[...] = jnp.zeros_like(rf_ref)
        rb_ref[...] = jnp.zeros_like(rb_ref)

    bdm = bdm_ref[...]
    avg = avg_ref[...]
    _ret_direction(qf_ref[...], kf_ref[...], vf_ref[...], gf_ref[...], rf_ref, dmf_ref[...], xif_ref[...],
                   ztf_ref[...], gcf_ref[...], bdm, avg, of_ref)
    _ret_direction(qb_ref[...], kb_ref[...], vb_ref[...], gb_ref[...], rb_ref, dmb_ref[...], xib_ref[...],
                   ztb_ref[...], gcb_ref[...], bdm, avg, ob_ref)


def _ret_tables():
    c = RET_CHUNK
    gamma_f = 1.0 - 2.0 ** (-5.0 - jnp.arange(RET_HEADS, dtype=F32))
    gamma_b = gamma_f[::-1]
    idx = jnp.arange(c, dtype=F32)
    diff = idx[:, None] - idx[None, :]

    def tables(gamma, reverse):
        lg = jnp.log(gamma)
        d = -diff if reverse else diff
        dm = jnp.where(d[None] >= 0, jnp.exp(jnp.maximum(d, 0.0)[None] * lg[:, None, None]), 0.0)
        dm = jnp.transpose(dm, (1, 0, 2)).reshape(c, RET_HEADS * c)
        xi_e = (c - idx) if reverse else (idx + 1.0)
        zt_e = idx if reverse else (c - 1.0 - idx)
        xi = jnp.repeat(jnp.exp(xi_e[:, None] * lg[None, :]), RET_DV, axis=1)
        zt = jnp.repeat(jnp.exp(zt_e[:, None] * lg[None, :]), RET_DK, axis=1)
        gc = jnp.repeat(jnp.exp(c * lg), RET_DV)[None, :]
        return dm, xi, zt, gc

    dmf, xif, ztf, gcf = tables(gamma_f, False)
    dmb, xib, ztb, gcb = tables(gamma_b, True)
    hk = jnp.arange(RET_QK_W) // RET_DK
    hv = jnp.arange(RET_W) // RET_DV
    bdm = (hk[:, None] == hv[None, :]).astype(F32)
    avg = ((hv[:, None] == hv[None, :]).astype(F32) / RET_DV).astype(BF16)
    return (dmf, dmb, xif, xib, ztf, ztb, gcf, gcb, bdm, avg)


def _ret_call(rq, rk, rv, gf, gb, tabs, n_lat_chunks):
    t_all = rq.shape[0]
    c = RET_CHUNK
    n = t_all // c
    n_ctx_chunks = n - n_lat_chunks

    def fwd(i):
        return (jnp.where(i < n_ctx_chunks, n_lat_chunks + i, i - n_ctx_chunks), 0)

    def bwd(i):
        return (n - 1 - i, 0)

    const2 = lambda i: (0, 0)
    tab_specs = [pl.BlockSpec(t.shape, const2) for t in tabs]
    return pl.pallas_call(
        _ret_kernel,
        grid=(n,),
        in_specs=[pl.BlockSpec((c, RET_QK_W), fwd), pl.BlockSpec((c, RET_QK_W), fwd),
                  pl.BlockSpec((c, RET_W), fwd), pl.BlockSpec((c, RET_W), fwd),
                  pl.BlockSpec((c, RET_QK_W), bwd), pl.BlockSpec((c, RET_QK_W), bwd),
                  pl.BlockSpec((c, RET_W), bwd), pl.BlockSpec((c, RET_W), bwd)] + tab_specs,
        out_specs=(pl.BlockSpec((c, RET_W), fwd), pl.BlockSpec((c, RET_W), bwd)),
        out_shape=(jax.ShapeDtypeStruct((t_all, RET_W), BF16), jax.ShapeDtypeStruct((t_all, RET_W), BF16)),
        scratch_shapes=[pltpu.VMEM((RET_QK_W, RET_W), F32), pltpu.VMEM((RET_QK_W, RET_W), F32)],
        compiler_params=_cparams(("arbitrary",)),
        name="retention",
    )(rq, rk, rv, gf, rq, rk, rv, gb, *tabs)


ATT_SUB = 128
VT_ROWS = HEAD_PAD


def _attn_kernel(q_ref, k_ref, vt_ref, o_ref, s0_scr, s1_scr, p0_scr, p1_scr, acc_scr):
    q = q_ref[...]
    tq = q.shape[0]
    nkc, tk, _ = k_ref.shape
    nsub = tk // ATT_SUB

    s_bufs = (s0_scr, s1_scr)
    p_bufs = (p0_scr, p1_scr)

    def pv(c, par, alpha):
        part = jnp.dot(vt_ref[c], p_bufs[par][...], preferred_element_type=F32)
        acc_scr[...] = alpha * acc_scr[...] + part

    def step(c, par, m_old, m_blk, alpha_prev, with_scores, with_pv):
        if with_pv:
            pv(c - 1, 1 - par, alpha_prev)
        m_new = jnp.maximum(m_old, m_blk)
        alpha = jnp.exp2(m_old - m_new)
        mx = jnp.full((8, tq), -jnp.inf, F32)
        for j in range(nsub):
            rows = pl.ds(j * ATT_SUB, ATT_SUB)
            p_bufs[par][rows, :] = jnp.exp2(s_bufs[par][rows, :] - m_new).astype(BF16)
            if with_scores:
                mx = jnp.maximum(mx, score_rows(c + 1, 1 - par, rows))
        return m_new, jnp.max(mx, axis=0, keepdims=True), alpha

    def score_rows(c, par, rows):
        s = jnp.dot(k_ref[c, rows, :], qt, preferred_element_type=F32)
        s_bufs[par][rows, :] = s
        return jnp.max(s.reshape(ATT_SUB // 8, 8, tq), axis=0)

    def scores(c, par):
        mx = jnp.full((8, tq), -jnp.inf, F32)
        for j in range(nsub):
            mx = jnp.maximum(mx, score_rows(c, par, pl.ds(j * ATT_SUB, ATT_SUB)))
        return jnp.max(mx, axis=0, keepdims=True)

    acc_scr[...] = jnp.zeros_like(acc_scr)
    qt = q.astype(F32).T.astype(BF16)
    m = jnp.full((1, tq), -jnp.inf, F32)
    m_blk = scores(0, 0)
    alpha = jnp.ones((1, tq), F32)
    if nkc > 1:
        m, m_blk, alpha = step(0, 0, m, m_blk, alpha, True, False)
        def body(c, carry):
            return lax.cond(c % 2 == 1,
                            lambda cr: step(c, 1, *cr, True, True),
                            lambda cr: step(c, 0, *cr, True, True), carry)

        m, m_blk, alpha = lax.fori_loop(1, nkc - 1, body, (m, m_blk, alpha))
    last = nkc - 1
    m, _, alpha = step(last, last % 2, m, m_blk, alpha, False, nkc > 1)
    pv(last, last % 2, alpha)
    acc = acc_scr[...]
    out_t = acc / acc[MLA_D_V:MLA_D_V + 1, :]
    if VT_ROWS < HEAD_PAD:
        out_t = jnp.concatenate([out_t, jnp.zeros((HEAD_PAD - VT_ROWS, tq), F32)], axis=0)
    o_ref[...] = out_t.T.astype(BF16)


def _attn_call(q, k, v, tq, tk, n_q_blocks):
    nh = q.shape[0]
    t_k = k.shape[1]
    nkc = t_k // tk
    k4 = k.reshape(nh, nkc, tk, HEAD_PAD)
    vt4 = jnp.swapaxes(v[:, :, :VT_ROWS].reshape(nh, nkc, tk, VT_ROWS), 2, 3)
    return pl.pallas_call(
        _attn_kernel,
        grid=(nh, n_q_blocks),
        in_specs=[pl.BlockSpec((None, tq, HEAD_PAD), lambda h, j: (h, j, 0)),
                  pl.BlockSpec((None, nkc, tk, HEAD_PAD), lambda h, j: (h, 0, 0, 0)),
                  pl.BlockSpec((None, nkc, VT_ROWS, tk), lambda h, j: (h, 0, 0, 0))],
        out_specs=pl.BlockSpec((tq, HEAD_PAD), lambda h, j: (j, h)),
        out_shape=jax.ShapeDtypeStruct((n_q_blocks * tq, nh * HEAD_PAD), BF16),
        scratch_shapes=[pltpu.VMEM((tk, tq), F32), pltpu.VMEM((tk, tq), F32),
                        pltpu.VMEM((tk, tq), BF16), pltpu.VMEM((tk, tq), BF16), pltpu.VMEM((VT_ROWS, tq), F32)],
        compiler_params=_cparams(("arbitrary", "arbitrary")),
        name="attention",
    )(q, k4, vt4)


def _top2_sum(a, b, c, d):
    hi1, lo1 = jnp.maximum(a, b), jnp.minimum(a, b)
    hi2, lo2 = jnp.maximum(c, d), jnp.minimum(c, d)
    return jnp.maximum(hi1, hi2) + jnp.maximum(jnp.minimum(hi1, hi2), jnp.maximum(lo1, lo2))


def _gates_t(aff, sel, gt_ref, grp_ref):
    rows = [sel[e:e + 1, :] for e in range(N_EXPERTS)]
    g_score = [_top2_sum(*rows[g * EXPERTS_PER_GROUP:(g + 1) * EXPERTS_PER_GROUP]) for g in range(N_GROUPS)]
    best = g_score[0]
    best_g = jnp.zeros_like(best, dtype=jnp.int32)
    for g in range(1, N_GROUPS):
        better = g_score[g] > best
        best = jnp.where(better, g_score[g], best)
        best_g = jnp.where(better, g, best_g)
    picked = []
    for e in range(N_EXPERTS):
        g = e // EXPERTS_PER_GROUP
        rank = jnp.zeros_like(best_g)
        for o in range(g * EXPERTS_PER_GROUP, (g + 1) * EXPERTS_PER_GROUP):
            if o == e:
                continue
            ahead = (rows[o] >= rows[e]) if o < e else (rows[o] > rows[e])
            rank = rank + jnp.where(ahead, 1, 0)
        picked.append(jnp.where(best_g == g, rank, 2) < 2)
    w = [jnp.where(picked[e], aff[e:e + 1, :], 0.0) for e in range(N_EXPERTS)]
    total = w[0]
    for e in range(1, N_EXPERTS):
        total = total + w[e]
    for e in range(N_EXPERTS):
        gt_ref[e:e + 1, :] = w[e] / total
    grp_ref[...] = best_g


OUT_HALF = D_MODEL // 2


def _outproj_kernel(x_ref, xc_ref, conv_ref, of_ref, ob_ref, att_ref, attc_ref, w_ref, mod_ref, nw_ref, rw_ref,
                    rb_ref, x1_ref, h2_ref, gt_ref, grp_ref, *, n_lat_tiles):
    is_lat = pl.program_id(0) < n_lat_tiles
    ret = (of_ref[...].astype(F32) + ob_ref[...].astype(F32)).astype(BF16)
    att = jnp.where(is_lat, att_ref[...], attc_ref[...])
    mix = jnp.concatenate([conv_ref[...], ret, att], axis=1)
    halves = []
    ssq = 0.0
    for hf in range(2):
        cols = slice(hf * OUT_HALF, (hf + 1) * OUT_HALF)
        o = jnp.dot(mix, w_ref[:, cols], preferred_element_type=F32)
        x1 = jnp.where(is_lat, x_ref[:, cols], xc_ref[:, cols]) + mod_ref[2:3, cols] * o
        x1_ref[:, cols] = x1
        ssq = ssq + jnp.sum(x1 * x1, axis=-1, keepdims=True)
        halves.append(x1)
    inv = lax.rsqrt(ssq * (1.0 / D_MODEL) + EPS)
    logit_parts = 0.0
    for hf in range(2):
        cols = slice(hf * OUT_HALF, (hf + 1) * OUT_HALF)
        h2 = (halves[hf] * inv * nw_ref[:, cols]) * (1.0 + mod_ref[4:5, cols]) + mod_ref[3:4, cols]
        hi = h2.astype(BF16)
        h2_ref[:, cols] = hi
        lo = (h2 - hi.astype(F32)).astype(BF16)
        logit_parts = (logit_parts + jnp.dot(hi, rw_ref[cols, :], preferred_element_type=F32)
                       + jnp.dot(lo, rw_ref[cols, :], preferred_element_type=F32))
    lt = logit_parts.T
    logits = lt[0:N_EXPERTS, :] + lt[N_EXPERTS:2 * N_EXPERTS, :]
    aff = jax.nn.sigmoid(logits)
    _gates_t(aff, aff + rb_ref[...], gt_ref, grp_ref)


def _outproj_call(x_lat, x_ctx, ctx_block0, conv, of, ob, att, att_c, w_ext, mod, nw, rw3, rb, n_lat_tiles,
                  n_tiles):
    d = x_lat.shape[1]
    tm = ROW_TILE
    row = lambda i: (i, 0)
    const2 = lambda i: (0, 0)
    lat = lambda i: (jnp.minimum(i, n_lat_tiles - 1), 0)
    cls = lambda i: (jnp.where(i < n_lat_tiles, 1, 0), 0, 0)
    return pl.pallas_call(
        functools.partial(_outproj_kernel, n_lat_tiles=n_lat_tiles),
        grid=(n_tiles,),
        in_specs=[pl.BlockSpec((tm, d), lat),
                  pl.BlockSpec((tm, d), lambda i: (ctx_block0 + jnp.maximum(i - n_lat_tiles, 0), 0)),
                  pl.BlockSpec((tm, CONV_W), row),
                  pl.BlockSpec((tm, RET_W), row),
                  pl.BlockSpec((tm, RET_W), row),
                  pl.BlockSpec((tm, ATT_W), lat),
                  pl.BlockSpec((tm, ATT_W), lambda i: (jnp.maximum(i - n_lat_tiles, 0), 0)),
                  pl.BlockSpec(w_ext.shape, const2),
                  pl.BlockSpec((None, 8, d), cls),
                  pl.BlockSpec((1, d), const2),
                  pl.BlockSpec((d, LANES), const2),
                  pl.BlockSpec((N_EXPERTS, 1), const2)],
        out_specs=(pl.BlockSpec((tm, d), row), pl.BlockSpec((tm, d), row),
                   pl.BlockSpec((N_EXPERTS, tm), lambda i: (0, i)), pl.BlockSpec((1, tm), lambda i: (0, i))),
        out_shape=(jax.ShapeDtypeStruct((n_tiles * tm, d), F32), jax.ShapeDtypeStruct((n_tiles * tm, d), BF16),
                   jax.ShapeDtypeStruct((N_EXPERTS, n_tiles * tm), F32),
                   jax.ShapeDtypeStruct((1, n_tiles * tm), jnp.int32)),
        compiler_params=_cparams(("arbitrary",)),
        name="outproj",
    )(x_lat, x_ctx, conv, of, ob, att, att_c, w_ext, mod, nw, rw3, rb)


MOE_DENSE_ROWS = 256


def _group_mlp(xb, gate4, wgu_ref, wd_ref):
    parts = []
    for k in range(EXPERTS_PER_GROUP):
        gu = jnp.dot(xb, wgu_ref[k], preferred_element_type=F32)
        parts.append((_silu(gu[:, :D_EXPERT]) * gu[:, D_EXPERT:] * gate4[:, k:k + 1]).astype(BF16))
    return jnp.dot(jnp.concatenate(parts, axis=1), wd_ref[...], preferred_element_type=F32)


def _moe_kernel(h_ref, gate8_ref, grp_ref, x1_ref, g2_ref, tri_ref, wgu_ref, wd_ref, *rest, cap, final_norm):
    rest = list(rest)
    fw_ref = rest.pop(0) if final_norm else None
    o_ref, cnt_scr = rest
    tm = h_ref.shape[0]
    grp = grp_ref[...]
    o_ref[...] = jnp.zeros_like(o_ref)
    rows8 = lax.broadcasted_iota(jnp.int32, (SUBLANES, tm), 0)
    member8 = jnp.where(rows8 == grp, 1.0, 0.0).astype(BF16)
    cnt_scr[...] = jnp.dot(member8, tri_ref[...], preferred_element_type=F32)

    def group(g, carry):
        incl = cnt_scr[pl.ds(g, 1), :]
        count = jnp.max(incl)
        gate8_g, wgu_g, wd_g = gate8_ref.at[g], wgu_ref.at[g], wd_ref.at[g]

        @pl.when(count <= cap)
        def _():
            pos = jnp.where(grp == g, incl.astype(jnp.int32) - 1, -1)
            slot = lax.broadcasted_iota(jnp.int32, (cap, tm), 0)
            onehot = jnp.where(slot == pos, 1.0, 0.0).astype(BF16)
            xg = jnp.dot(onehot, h_ref[...], preferred_element_type=F32).astype(BF16)
            g8 = jnp.dot(onehot, gate8_g[...], preferred_element_type=F32)
            y = _group_mlp(xg, g8[:, :EXPERTS_PER_GROUP] + g8[:, EXPERTS_PER_GROUP:], wgu_g, wd_g)
            o_ref[...] += lax.dot_general(onehot, y.astype(BF16), (((0,), (0,)), ((), ())),
                                          preferred_element_type=F32)

        @pl.when(count > cap)
        def _():
            def chunk(ci, inner):
                rows = pl.ds(pl.multiple_of(ci * MOE_DENSE_ROWS, MOE_DENSE_ROWS), MOE_DENSE_ROWS)
                g8 = gate8_g[rows, :].astype(F32)
                o_ref[rows, :] += _group_mlp(h_ref[rows, :],
                                             g8[:, :EXPERTS_PER_GROUP] + g8[:, EXPERTS_PER_GROUP:], wgu_g, wd_g)
                return inner

            lax.fori_loop(0, tm // MOE_DENSE_ROWS, chunk, 0)

        return carry

    lax.fori_loop(0, N_GROUPS, group, 0)
    x2 = x1_ref[...] + g2_ref[...] * o_ref[...]
    if final_norm:
        x2 = x2 * lax.rsqrt(jnp.mean(x2 * x2, axis=-1, keepdims=True) + EPS) * fw_ref[...]
    o_ref[...] = x2


def _moe_call(h2, gate8, grp, x1, g2, wgu, wd, tm, block0, n_tiles, final_w=None):
    out_rows, d = x1.shape
    cap = tm * 5 // 16
    assert cap % 16 == 0 and tm % MOE_DENSE_ROWS == 0
    tri = (jnp.arange(tm)[:, None] <= jnp.arange(tm)[None, :]).astype(BF16)
    rows = lambda i: (block0 + i, 0)
    const2 = lambda i: (0, 0)
    const3 = lambda i: (0, 0, 0)
    once = pl.Buffered(1)
    in_specs = [pl.BlockSpec((tm, d), rows),
                pl.BlockSpec((N_GROUPS, tm, 2 * EXPERTS_PER_GROUP), lambda i: (0, block0 + i, 0)),
                pl.BlockSpec((1, tm), lambda i: (0, block0 + i)),
                pl.BlockSpec((tm, d), rows),
                pl.BlockSpec((1, d), const2),
                pl.BlockSpec((tm, tm), const2, pipeline_mode=once),
                pl.BlockSpec(wgu.shape, lambda i: (0, 0, 0, 0), pipeline_mode=once),
                pl.BlockSpec(wd.shape, const3, pipeline_mode=once)]
    args = [h2, gate8, grp, x1, g2, tri, wgu, wd]
    if final_w is not None:
        in_specs.append(pl.BlockSpec((1, d), const2))
        args.append(final_w)
    return pl.pallas_call(
        functools.partial(_moe_kernel, cap=cap, final_norm=final_w is not None),
        grid=(n_tiles,),
        in_specs=in_specs,
        out_specs=pl.BlockSpec((tm, d), rows),
        out_shape=jax.ShapeDtypeStruct((out_rows, d), F32),
        scratch_shapes=[pltpu.VMEM((SUBLANES, tm), F32)],
        input_output_aliases={3: 0},
        compiler_params=_cparams(("arbitrary",)),
        name="experts",
    )(*args)


_SWAP32 = np.concatenate([np.arange(8, 16), np.arange(0, 8), np.arange(24, 32), np.arange(16, 24)])


def _pad_cols(w, width):
    return jnp.pad(w, ((0, 0), (0, width - w.shape[1])))


def _in_weight(w_in):
    sizes = (2 * CONV_W, RET_QK_W, RET_QK_W, RET_W, RET_W, RET_W, MLA_Q_RANK, MLA_KV_RANK, MLA_D_ROPE)
    offs = np.concatenate([[0], np.cumsum(sizes)])
    conv, rq, rk, rv, gf, gb, cq, ckv, kr = [w_in[:, offs[i]:offs[i + 1]] for i in range(len(sizes))]
    swap192 = np.concatenate([h * ROPE_DIM + _SWAP32 for h in range(RET_HEADS)])
    d = w_in.shape[0]

    def place_rope(w):
        return jnp.concatenate([jnp.zeros((d, MLA_D_NOPE), w.dtype), w,
                                jnp.zeros((d, LANES - MLA_D_NOPE - MLA_D_ROPE), w.dtype)], axis=1)

    ext = jnp.concatenate([
        conv, _pad_cols(rq, 256), _pad_cols(rq[:, swap192], 256), _pad_cols(rk, 256), _pad_cols(rk[:, swap192], 256),
        rv, gf, gb, _pad_cols(cq, 256), ckv, place_rope(kr), place_rope(kr[:, _SWAP32])], axis=1)
    assert ext.shape[1] == IN_EXT
    return ext.astype(BF16)


def _uq_weight(w_uq):
    r = w_uq.shape[0]
    w = w_uq.reshape(r, MLA_HEADS, MLA_D_NOPE + MLA_D_ROPE)
    nope, rope = w[..., :MLA_D_NOPE], w[..., MLA_D_NOPE:]
    zpad = jnp.zeros((r, MLA_HEADS, HEAD_PAD - MLA_D_NOPE - MLA_D_ROPE), w.dtype)
    main = jnp.concatenate([nope, rope, zpad], axis=-1).reshape(r, ATT_W)
    part = jnp.concatenate([jnp.zeros_like(nope), rope[..., _SWAP32], zpad], axis=-1).reshape(r, ATT_W)
    return jnp.concatenate([main, part], axis=1).astype(BF16)


def _ukv_weight(w_ukv):
    r = w_ukv.shape[0]
    w = w_ukv.reshape(r, MLA_HEADS, MLA_D_NOPE + MLA_D_V)
    zpad = jnp.zeros((r, MLA_HEADS, HEAD_PAD - MLA_D_NOPE), w.dtype)
    kpart = jnp.concatenate([w[..., :MLA_D_NOPE], zpad], axis=-1).reshape(r, ATT_W)
    vpart = jnp.concatenate([w[..., MLA_D_NOPE:], zpad], axis=-1).reshape(r, ATT_W)
    return jnp.concatenate([kpart, vpart], axis=1).astype(BF16)


def _out_weight(w_out):
    d = w_out.shape[1]
    conv, ret = w_out[:CONV_W], w_out[CONV_W:CONV_W + RET_W]
    att = w_out[CONV_W + RET_W:].reshape(MLA_HEADS, MLA_D_V, d)
    att = jnp.pad(att, ((0, 0), (0, HEAD_PAD - MLA_D_V), (0, 0))).reshape(ATT_W, d)
    return jnp.concatenate([conv, ret, att], axis=0).astype(BF16)


def _rope_tables(n_ctx, seq):
    pos = jnp.arange(seq)
    r = (pos // GRID_W).astype(F32)
    cl = (pos % GRID_W).astype(F32)
    inv = ROPE_BASE ** (-jnp.arange(ROPE_PAIRS, dtype=F32) / ROPE_PAIRS)
    ar, ac = r[:, None] * inv, cl[:, None] * inv
    cs32 = jnp.concatenate([jnp.cos(ar), jnp.cos(ar), jnp.cos(ac), jnp.cos(ac)], axis=1)
    sn32 = jnp.concatenate([-jnp.sin(ar), jnp.sin(ar), -jnp.sin(ac), jnp.sin(ac)], axis=1)
    cs = jnp.concatenate([cs32, jnp.ones((n_ctx, ROPE_DIM), F32)], axis=0)
    sn = jnp.concatenate([sn32, jnp.zeros((n_ctx, ROPE_DIM), F32)], axis=0)
    return jnp.tile(cs, (1, LANES // ROPE_DIM)), jnp.tile(sn, (1, LANES // ROPE_DIM))


def _router_weight(router_w):
    hi = router_w.astype(BF16)
    lo = (router_w - hi.astype(F32)).astype(BF16)
    return jnp.pad(jnp.concatenate([hi, lo], axis=1), ((0, 0), (0, LANES - 2 * N_EXPERTS)))


def _group_gates(gates_t):
    t = gates_t.shape[1]
    g4 = jnp.transpose(gates_t.reshape(N_GROUPS, EXPERTS_PER_GROUP, t), (0, 2, 1))
    hi = g4.astype(BF16)
    lo = (g4 - hi.astype(F32)).astype(BF16)
    return jnp.concatenate([hi, lo], axis=-1)


def _group_weights(w_gate, w_up, w_down):
    e, d, f = w_gate.shape
    gu = jnp.concatenate([w_gate, w_up], axis=-1).astype(BF16)
    gu = gu.reshape(N_GROUPS, EXPERTS_PER_GROUP, d, 2 * f)
    wd = w_down.astype(BF16).reshape(N_GROUPS, EXPERTS_PER_GROUP * f, d)
    return gu, wd


def kernel(x, c, ctx, c_ctx, ada_w, ada_b, norm1_w, norm2_w, w_in, conv_dw, conv_b, conv_ln_w, conv_ln_b,
           mla_q_norm_w, mla_w_uq, mla_kv_norm_w, mla_w_ukv, w_out, router_w, router_bias,
           moe_w_gate, moe_w_up, moe_w_down, final_norm_w):
    batch, seq, d = x.shape
    n_ctx = ctx.shape[1]
    depth = ada_w.shape[0]
    t_all = seq + n_ctx
    assert batch == 1 and d == D_MODEL
    assert n_ctx % ROW_TILE == 0 and seq % ROW_TILE == 0 and n_ctx <= MOE_TILE and seq % MOE_TILE == 0
    assert seq % ATT_TQ == 0 and t_all % ATT_TK == 0 and n_ctx % RET_CHUNK == 0 and seq % n_ctx == 0
    n_lat_tiles = seq // ROW_TILE
    n_all_tiles = t_all // ROW_TILE

    cond = jnp.zeros((8, d), F32).at[0].set(c_ctx).at[1].set(c[0])
    mod = _ada_call(cond, ada_w, ada_b)[:, :2, :].reshape(depth, 2, 6, d)
    mod8 = jnp.pad(mod, ((0, 0), (0, 0), (0, 2), (0, 0)))
    cs, sn = _rope_tables(n_ctx, seq)
    ret_tabs = _ret_tables()
    rw3 = _router_weight(router_w)
    rb = router_bias.reshape(N_EXPERTS, 1)

    x_lat, x_ctx, ctx_block0 = x[0], ctx[0], 0
    for l in range(depth):
        last = l == depth - 1
        u, rq, rk, rv, gf, gb, q_att, k_att, v_att = _inproj_call(
            x_lat, x_ctx, ctx_block0, t_all, norm1_w[l][None, :], mod8[l][:, :2, :], _in_weight(w_in[l]), cs, sn,
            mla_q_norm_w[l][None, :], _uq_weight(mla_w_uq[l]), mla_kv_norm_w[l][None, :], _ukv_weight(mla_w_ukv[l]),
            n_lat_tiles)
        conv = _conv_call(u, conv_dw[l], conv_b[l][None, :], conv_ln_w[l][None, :], conv_ln_b[l][None, :],
                          n_lat_tiles)
        of, ob = _ret_call(rq, rk, rv, gf, gb, ret_tabs, seq // RET_CHUNK)
        att = _attn_call(q_att, k_att, v_att, ATT_TQ, ATT_TK, seq // ATT_TQ)
        att_c = att
        if not last:
            tqc = min(ATT_TQ, n_ctx)
            att_c = _attn_call(q_att[:, seq:], k_att[:, seq:], v_att[:, seq:], tqc, n_ctx, n_ctx // tqc)
        n_tiles = n_lat_tiles if last else n_all_tiles
        x1, h2, gates_t, grp = _outproj_call(x_lat, x_ctx, ctx_block0, conv, of, ob, att, att_c,
                                             _out_weight(w_out[l]), mod8[l], norm2_w[l][None, :], rw3, rb,
                                             n_lat_tiles, n_tiles)
        gate8 = _group_gates(gates_t)
        wgu, wd = _group_weights(moe_w_gate[l], moe_w_up[l], moe_w_down[l])
        x_all = _moe_call(h2, gate8, grp, x1, mod[l, 1, 5][None, :], wgu, wd, MOE_TILE, 0, seq // MOE_TILE,
                          final_w=final_norm_w[None, :] if last else None)
        if not last:
            x_all = _moe_call(h2, gate8, grp, x_all, mod[l, 0, 5][None, :], wgu, wd, n_ctx, seq // n_ctx, 1)
        x_lat, x_ctx, ctx_block0 = x_all, x_all, n_lat_tiles
    return x_all[None]
```

```python
import functools
import math

import numpy as np
import jax
import jax.numpy as jnp
from jax import lax
from jax.experimental import pallas as pl
from jax.experimental.pallas import tpu as pltpu

F32 = jnp.float32
BF16 = jnp.bfloat16

D_MODEL = 1024
GRID_W = 64
CONV_W = 256
CONV_K = 31
RET_HEADS = 6
RET_DK = 32
RET_DV = 64
RET_QK_W = RET_HEADS * RET_DK
RET_W = RET_HEADS * RET_DV
RET_CHUNK = 128
RET_GN_EPS = 1e-5
MLA_HEADS = 6
MLA_Q_RANK = 192
MLA_KV_RANK = 128
MLA_D_NOPE = 64
MLA_D_ROPE = 32
MLA_D_V = 64
MLA_SCALE = (MLA_D_NOPE + MLA_D_ROPE) ** -0.5
ROPE_DIM = 32
ROPE_PAIRS = ROPE_DIM // 4
ROPE_BASE = 10000.0
N_EXPERTS = 16
N_GROUPS = 4
EXPERTS_PER_GROUP = N_EXPERTS // N_GROUPS
D_EXPERT = 256
EPS = 1e-6

LANES = 128
HEAD_PAD = LANES
ATT_W = MLA_HEADS * HEAD_PAD
LOG2E = math.log2(math.e)

C_CONV = 0
C_RQ = 512
C_RQS = 768
C_RK = 1024
C_RKS = 1280
C_RV = 1536
C_RGF = 1920
C_RGB = 2304
C_CQ = 2688
C_CKV = 2944
C_KR = 3072
C_KRS = 3200
IN_EXT = 3328

ROW_TILE = 256
MOE_TILE = 512
ATT_TQ = 1024
ATT_TK = 1280
VMEM_LIMIT = 48 * 1024 * 1024


def _cparams(sem):
    return pltpu.CompilerParams(dimension_semantics=sem, vmem_limit_bytes=VMEM_LIMIT)


def _silu(x):
    return x * jax.nn.sigmoid(x)


def _ada_kernel(c_ref, w_ref, b_ref, o_ref):
    s = _silu(c_ref[...])
    o_ref[...] = jnp.dot(s, w_ref[...], preferred_element_type=F32,
                         precision=lax.Precision.HIGHEST) + b_ref[...]


def _ada_call(cond, ada_w, ada_b):
    depth, d, n = ada_w.shape
    tn = 1536
    return pl.pallas_call(
        _ada_kernel,
        grid=(depth, n // tn),
        in_specs=[pl.BlockSpec((8, d), lambda l, j: (0, 0)),
                  pl.BlockSpec((None, d, tn), lambda l, j: (l, 0, j)),
                  pl.BlockSpec((None, 1, tn), lambda l, j: (l, 0, j))],
        out_specs=pl.BlockSpec((None, 8, tn), lambda l, j: (l, 0, j)),
        out_shape=jax.ShapeDtypeStruct((depth, 8, n), F32),
        compiler_params=_cparams(("arbitrary", "arbitrary")),
        name="adaln",
    )(cond, ada_w, ada_b.reshape(depth, 1, n))


def _inproj_kernel(x_ref, xc_ref, nw_ref, mod_ref, w_ref, cs_ref, sn_ref, qnw_ref, wuq_ref, kvnw_ref, wukv_ref,
                   u_ref, rq_ref, rk_ref, rv_ref, gf_ref, gb_ref, q_ref, k_ref, v_ref, *, n_lat_tiles):
    x = jnp.where(pl.program_id(0) < n_lat_tiles, x_ref[...], xc_ref[...])
    y = x * lax.rsqrt(jnp.mean(x * x, axis=-1, keepdims=True) + EPS) * nw_ref[...]
    h = (y * (1.0 + mod_ref[1:2, :]) + mod_ref[0:1, :]).astype(BF16)

    def proj(c0, width):
        return jnp.dot(h, w_ref[:, c0:c0 + width], preferred_element_type=F32)

    cs = cs_ref[...]
    sn = sn_ref[...]
    lane = lax.broadcasted_iota(jnp.int32, (1, LANES), 1)
    rope_lanes = (lane >= MLA_D_NOPE) & (lane < MLA_D_NOPE + MLA_D_ROPE)
    csq = jnp.where(rope_lanes, cs, 1.0)
    snq = jnp.where(rope_lanes, sn, 0.0)

    ag = proj(C_CONV, 2 * CONV_W)
    u_ref[...] = ag[:, :CONV_W] * jax.nn.sigmoid(ag[:, CONV_W:])

    cs192 = jnp.concatenate([cs, cs[:, :RET_QK_W - LANES]], axis=1)
    sn192 = jnp.concatenate([sn, sn[:, :RET_QK_W - LANES]], axis=1)
    rq = proj(C_RQ, 256)[:, :RET_QK_W] * cs192 + proj(C_RQS, 256)[:, :RET_QK_W] * sn192
    rq_ref[...] = rq.astype(BF16)
    rk = proj(C_RK, 256)[:, :RET_QK_W] * cs192 + proj(C_RKS, 256)[:, :RET_QK_W] * sn192
    rk_ref[...] = (rk * (RET_DK ** -0.5)).astype(BF16)
    rv_ref[...] = proj(C_RV, RET_W).astype(BF16)
    gf_ref[...] = _silu(proj(C_RGF, RET_W))
    gb_ref[...] = _silu(proj(C_RGB, RET_W))

    cq = proj(C_CQ, 256)[:, :MLA_Q_RANK]
    cqn = (cq * lax.rsqrt(jnp.mean(cq * cq, axis=-1, keepdims=True) + EPS) * qnw_ref[...]).astype(BF16)
    qa = jnp.dot(cqn, wuq_ref[:, :ATT_W], preferred_element_type=F32)
    qb = jnp.dot(cqn, wuq_ref[:, ATT_W:], preferred_element_type=F32)
    ckv = proj(C_CKV, MLA_KV_RANK)
    ckvn = (ckv * lax.rsqrt(jnp.mean(ckv * ckv, axis=-1, keepdims=True) + EPS) * kvnw_ref[...]).astype(BF16)
    ka = jnp.dot(ckvn, wukv_ref[:, :ATT_W], preferred_element_type=F32)
    va = jnp.dot(ckvn, wukv_ref[:, ATT_W:], preferred_element_type=F32)
    kr = proj(C_KR, LANES) * csq + proj(C_KRS, LANES) * snq
    ones_lane = (lane == MLA_D_V).astype(F32)
    for hd in range(MLA_HEADS):
        sl = slice(hd * HEAD_PAD, (hd + 1) * HEAD_PAD)
        q_ref[hd] = ((qa[:, sl] * csq + qb[:, sl] * snq) * (MLA_SCALE * LOG2E)).astype(BF16)
        k_ref[hd] = (ka[:, sl] + kr).astype(BF16)
        v_ref[hd] = (va[:, sl] + ones_lane).astype(BF16)


def _inproj_call(x_lat, x_ctx, ctx_block0, t_all, nw, mod, w_ext, cs, sn, qnw, wuq, kvnw, wukv, n_lat_tiles):
    d = x_lat.shape[1]
    tm = ROW_TILE
    row = lambda i: (i, 0)
    const2 = lambda i: (0, 0)
    head = lambda i: (0, i, 0)
    cls = lambda i: (jnp.where(i < n_lat_tiles, 1, 0), 0, 0)
    out_shape = (
        jax.ShapeDtypeStruct((t_all, CONV_W), F32),
        jax.ShapeDtypeStruct((t_all, RET_QK_W), BF16),
        jax.ShapeDtypeStruct((t_all, RET_QK_W), BF16),
        jax.ShapeDtypeStruct((t_all, RET_W), BF16),
        jax.ShapeDtypeStruct((t_all, RET_W), F32),
        jax.ShapeDtypeStruct((t_all, RET_W), F32),
        jax.ShapeDtypeStruct((MLA_HEADS, t_all, HEAD_PAD), BF16),
        jax.ShapeDtypeStruct((MLA_HEADS, t_all, HEAD_PAD), BF16),
        jax.ShapeDtypeStruct((MLA_HEADS, t_all, HEAD_PAD), BF16),
    )
    out_specs = (
        pl.BlockSpec((tm, CONV_W), row),
        pl.BlockSpec((tm, RET_QK_W), row),
        pl.BlockSpec((tm, RET_QK_W), row),
        pl.BlockSpec((tm, RET_W), row),
        pl.BlockSpec((tm, RET_W), row),
        pl.BlockSpec((tm, RET_W), row),
        pl.BlockSpec((MLA_HEADS, tm, HEAD_PAD), head),
        pl.BlockSpec((MLA_HEADS, tm, HEAD_PAD), head),
        pl.BlockSpec((MLA_HEADS, tm, HEAD_PAD), head),
    )
    return pl.pallas_call(
        functools.partial(_inproj_kernel, n_lat_tiles=n_lat_tiles),
        grid=(t_all // tm,),
        in_specs=[pl.BlockSpec((tm, d), lambda i: (jnp.minimum(i, n_lat_tiles - 1), 0)),
                  pl.BlockSpec((tm, d), lambda i: (ctx_block0 + jnp.maximum(i - n_lat_tiles, 0), 0)),
                  pl.BlockSpec((1, d), const2),
                  pl.BlockSpec((None, 2, d), cls),
                  pl.BlockSpec((d, IN_EXT), const2),
                  pl.BlockSpec((tm, LANES), row),
                  pl.BlockSpec((tm, LANES), row),
                  pl.BlockSpec((1, MLA_Q_RANK), const2),
                  pl.BlockSpec((MLA_Q_RANK, 2 * ATT_W), const2),
                  pl.BlockSpec((1, MLA_KV_RANK), const2),
                  pl.BlockSpec((MLA_KV_RANK, 2 * ATT_W), const2)],
        out_specs=out_specs,
        out_shape=out_shape,
        compiler_params=_cparams(("arbitrary",)),
        name="inproj",
    )(x_lat, x_ctx, nw, mod, w_ext, cs, sn, qnw, wuq, kvnw, wukv)


CONV_HALO = 16
CONV_SUB = 64
SUBLANES = 8


def _conv_kernel(prev_ref, cur_ref, next_ref, dw_ref, b_ref, lnw_ref, lnb_ref, o_ref, ext_ref, sh_ref, *,
                 n_lat_tiles):
    i = pl.program_id(0)
    n = pl.num_programs(0)
    tm = cur_ref.shape[0]
    seq_start = (i == 0) | (i == n_lat_tiles)
    seq_end = (i == n_lat_tiles - 1) | (i == n - 1)
    ext_ref[0:CONV_HALO, :] = jnp.where(seq_start, 0.0, prev_ref[...])
    ext_ref[CONV_HALO:CONV_HALO + tm, :] = cur_ref[...]
    ext_ref[CONV_HALO + tm:, :] = jnp.where(seq_end, 0.0, next_ref[...])
    span = sh_ref.shape[1]
    for ph in range(SUBLANES):
        sh_ref[ph] = ext_ref[pl.ds(ph, span), :]
    base = CONV_HALO - CONV_K // 2
    for r in range(tm // CONV_SUB):
        acc = jnp.zeros((CONV_SUB, CONV_W), F32) + b_ref[...]
        for k in range(CONV_K):
            off = base + k
            acc = acc + (sh_ref[off % SUBLANES, pl.ds(r * CONV_SUB + off - off % SUBLANES, CONV_SUB), :]
                         * dw_ref[k:k + 1, :])
        mu = jnp.mean(acc, axis=-1, keepdims=True)
        dlt = acc - mu
        var = jnp.mean(dlt * dlt, axis=-1, keepdims=True)
        y = dlt * lax.rsqrt(var + EPS) * lnw_ref[...] + lnb_ref[...]
        o_ref[r * CONV_SUB:(r + 1) * CONV_SUB, :] = _silu(y).astype(BF16)


def _conv_call(u, dw, b, lnw, lnb, n_lat_tiles):
    t_all = u.shape[0]
    tm = ROW_TILE
    hpt = tm // CONV_HALO
    n_halo = t_all // CONV_HALO
    const2 = lambda i: (0, 0)
    return pl.pallas_call(
        functools.partial(_conv_kernel, n_lat_tiles=n_lat_tiles),
        grid=(t_all // tm,),
        in_specs=[pl.BlockSpec((CONV_HALO, CONV_W), lambda i: (jnp.maximum(i * hpt - 1, 0), 0)),
                  pl.BlockSpec((tm, CONV_W), lambda i: (i, 0)),
                  pl.BlockSpec((CONV_HALO, CONV_W), lambda i: (jnp.minimum((i + 1) * hpt, n_halo - 1), 0)),
                  pl.BlockSpec((CONV_K, CONV_W), const2),
                  pl.BlockSpec((1, CONV_W), const2),
                  pl.BlockSpec((1, CONV_W), const2),
                  pl.BlockSpec((1, CONV_W), const2)],
        out_specs=pl.BlockSpec((tm, CONV_W), lambda i: (i, 0)),
        out_shape=jax.ShapeDtypeStruct((t_all, CONV_W), BF16),
        scratch_shapes=[pltpu.VMEM((tm + 2 * CONV_HALO, CONV_W), F32),
                        pltpu.VMEM((SUBLANES, tm + 2 * CONV_HALO - SUBLANES, CONV_W), F32)],
        compiler_params=_cparams(("arbitrary",)),
        name="conv",
    )(u, u, u, dw, b, lnw, lnb)


def _split_dot(x, a):
    hi = x.astype(BF16)
    lo = (x - hi.astype(F32)).astype(BF16)
    return (jnp.dot(hi, a, preferred_element_type=F32) + jnp.dot(lo, a, preferred_element_type=F32))


def _ret_direction(q, k, v, gate, r_ref, dmask, xi, zeta, gchunk, bdmask, avg, o_ref):
    lane_k = lax.broadcasted_iota(jnp.int32, (1, RET_QK_W), 1) // RET_DK
    lane_v = lax.broadcasted_iota(jnp.int32, (1, RET_W), 1) // RET_DV
    zero_k = jnp.zeros_like(k)
    zero_v = jnp.zeros_like(v)
    k_bd = jnp.concatenate([jnp.where(lane_k == hd, k, zero_k) for hd in range(RET_HEADS)], axis=0)
    v_bd = jnp.concatenate([jnp.where(lane_v == hd, v, zero_v) for hd in range(RET_HEADS)], axis=0)
    s = lax.dot_general(q, k_bd, (((1,), (1,)), ((), ())), preferred_element_type=F32)
    s = (s * dmask).astype(BF16)
    inner = jnp.dot(s, v_bd, preferred_element_type=F32)
    r = r_ref[...]
    cross = jnp.dot(q, r.astype(BF16), preferred_element_type=F32) * xi
    o = inner + cross
    kz = (k.astype(F32) * zeta).astype(BF16)
    ds = lax.dot_general(kz, v, (((0,), (0,)), ((), ())), preferred_element_type=F32)
    r_ref[...] = gchunk * r + ds * bdmask
    mu = jnp.dot(o.astype(BF16), avg, preferred_element_type=F32)
    dlt = o - mu
    var = jnp.dot((dlt * dlt).astype(BF16), avg, preferred_element_type=F32)
    o_ref[...] = (gate * (dlt * lax.rsqrt(var + RET_GN_EPS))).astype(BF16)


def _ret_kernel(qf_ref, kf_ref, vf_ref, gf_ref, qb_ref, kb_ref, vb_ref, gb_ref,
                dmf_ref, dmb_ref, xif_ref, xib_ref, ztf_ref, ztb_ref, gcf_ref, gcb_ref, bdm_ref, avg_ref,
                of_ref, ob_ref, rf_ref, rb_ref):
    @pl.when(pl.program_id(0) == 0)
    def _():
        rf_ref[...] = jnp.zeros_like(rf_ref)
        rb_ref[...] = jnp.zeros_like(rb_ref)

    bdm = bdm_ref[...]
    avg = avg_ref[...]
    _ret_direction(qf_ref[...], kf_ref[...], vf_ref[...], gf_ref[...], rf_ref, dmf_ref[...], xif_ref[...],
                   ztf_ref[...], gcf_ref[...], bdm, avg, of_ref)
    _ret_direction(qb_ref[...], kb_ref[...], vb_ref[...], gb_ref[...], rb_ref, dmb_ref[...], xib_ref[...],
                   ztb_ref[...], gcb_ref[...], bdm, avg, ob_ref)


def _ret_tables():
    c = RET_CHUNK
    f32 = np.float32
    gamma_f = (1.0 - 2.0 ** (-5.0 - np.arange(RET_HEADS, dtype=f32))).astype(f32)
    gamma_b = gamma_f[::-1]
    idx = np.arange(c, dtype=f32)
    diff = idx[:, None] - idx[None, :]

    def tables(gamma, reverse):
        lg = np.log(gamma).astype(f32)
        d = -diff if reverse else diff
        dm = np.where(d[None] >= 0, np.exp(np.maximum(d, 0.0)[None] * lg[:, None, None]), 0.0)
        dm = np.transpose(dm, (1, 0, 2)).reshape(c, RET_HEADS * c)
        xi_e = (c - idx) if reverse else (idx + 1.0)
        zt_e = idx if reverse else (c - 1.0 - idx)
        xi = np.repeat(np.exp(xi_e[:, None] * lg[None, :]), RET_DV, axis=1)
        zt = np.repeat(np.exp(zt_e[:, None] * lg[None, :]), RET_DK, axis=1)
        gc = np.repeat(np.exp(c * lg), RET_DV)[None, :]
        return [t.astype(f32) for t in (dm, xi, zt, gc)]

    dmf, xif, ztf, gcf = tables(gamma_f, False)
    dmb, xib, ztb, gcb = tables(gamma_b, True)
    hk = np.arange(RET_QK_W) // RET_DK
    hv = np.arange(RET_W) // RET_DV
    bdm = (hk[:, None] == hv[None, :]).astype(f32)
    avg = jnp.asarray((hv[:, None] == hv[None, :]).astype(f32) / RET_DV, dtype=BF16)
    return tuple(jnp.asarray(t) for t in (dmf, dmb, xif, xib, ztf, ztb, gcf, gcb, bdm)) + (avg,)


def _ret_call(rq, rk, rv, gf, gb, tabs, n_lat_chunks):
    t_all = rq.shape[0]
    c = RET_CHUNK
    n = t_all // c
    n_ctx_chunks = n - n_lat_chunks

    def fwd(i):
        return (jnp.where(i < n_ctx_chunks, n_lat_chunks + i, i - n_ctx_chunks), 0)

    def bwd(i):
        return (n - 1 - i, 0)

    const2 = lambda i: (0, 0)
    tab_specs = [pl.BlockSpec(t.shape, const2) for t in tabs]
    return pl.pallas_call(
        _ret_kernel,
        grid=(n,),
        in_specs=[pl.BlockSpec((c, RET_QK_W), fwd), pl.BlockSpec((c, RET_QK_W), fwd),
                  pl.BlockSpec((c, RET_W), fwd), pl.BlockSpec((c, RET_W), fwd),
                  pl.BlockSpec((c, RET_QK_W), bwd), pl.BlockSpec((c, RET_QK_W), bwd),
                  pl.BlockSpec((c, RET_W), bwd), pl.BlockSpec((c, RET_W), bwd)] + tab_specs,
        out_specs=(pl.BlockSpec((c, RET_W), fwd), pl.BlockSpec((c, RET_W), bwd)),
        out_shape=(jax.ShapeDtypeStruct((t_all, RET_W), BF16), jax.ShapeDtypeStruct((t_all, RET_W), BF16)),
        scratch_shapes=[pltpu.VMEM((RET_QK_W, RET_W), F32), pltpu.VMEM((RET_QK_W, RET_W), F32)],
        compiler_params=_cparams(("arbitrary",)),
        name="retention",
    )(rq, rk, rv, gf, rq, rk, rv, gb, *tabs)


ATT_SUB = 128
VT_ROWS = HEAD_PAD


def _attn_kernel(q_ref, k_ref, vt_ref, o_ref, s0_scr, s1_scr, p0_scr, p1_scr, acc_scr):
    q = q_ref[...]
    tq = q.shape[0]
    nkc, tk, _ = k_ref.shape
    nsub = tk // ATT_SUB

    s_bufs = (s0_scr, s1_scr)
    p_bufs = (p0_scr, p1_scr)

    def pv(c, par, alpha):
        part = jnp.dot(vt_ref[c], p_bufs[par][...], preferred_element_type=F32)
        acc_scr[...] = alpha * acc_scr[...] + part

    def step(c, par, m_old, m_blk, alpha_prev, with_scores, with_pv):
        if with_pv:
            pv(c - 1, 1 - par, alpha_prev)
        m_new = jnp.maximum(m_old, m_blk)
        alpha = jnp.exp2(m_old - m_new)
        mx = jnp.full((8, tq), -jnp.inf, F32)
        for j in range(nsub):
            rows = pl.ds(j * ATT_SUB, ATT_SUB)
            p_bufs[par][rows, :] = jnp.exp2(s_bufs[par][rows, :] - m_new).astype(BF16)
            if with_scores:
                mx = jnp.maximum(mx, score_rows(c + 1, 1 - par, rows))
        return m_new, jnp.max(mx, axis=0, keepdims=True), alpha

    def score_rows(c, par, rows):
        s = jnp.dot(k_ref[c, rows, :], qt, preferred_element_type=F32)
        s_bufs[par][rows, :] = s
        return jnp.max(s.reshape(ATT_SUB // 8, 8, tq), axis=0)

    def scores(c, par):
        mx = jnp.full((8, tq), -jnp.inf, F32)
        for j in range(nsub):
            mx = jnp.maximum(mx, score_rows(c, par, pl.ds(j * ATT_SUB, ATT_SUB)))
        return jnp.max(mx, axis=0, keepdims=True)

    acc_scr[...] = jnp.zeros_like(acc_scr)
    qt = q.astype(F32).T.astype(BF16)
    m = jnp.full((1, tq), -jnp.inf, F32)
    m_blk = scores(0, 0)
    alpha = jnp.ones((1, tq), F32)
    if nkc > 1:
        m, m_blk, alpha = step(0, 0, m, m_blk, alpha, True, False)
        def body(c, carry):
            return lax.cond(c % 2 == 1,
                            lambda cr: step(c, 1, *cr, True, True),
                            lambda cr: step(c, 0, *cr, True, True), carry)

        m, m_blk, alpha = lax.fori_loop(1, nkc - 1, body, (m, m_blk, alpha))
    last = nkc - 1
    m, _, alpha = step(last, last % 2, m, m_blk, alpha, False, nkc > 1)
    pv(last, last % 2, alpha)
    acc = acc_scr[...]
    out_t = acc / acc[MLA_D_V:MLA_D_V + 1, :]
    if VT_ROWS < HEAD_PAD:
        out_t = jnp.concatenate([out_t, jnp.zeros((HEAD_PAD - VT_ROWS, tq), F32)], axis=0)
    o_ref[...] = out_t.T.astype(BF16)


def _attn_call(q, k, v, tq, tk, n_q_blocks):
    nh = q.shape[0]
    t_k = k.shape[1]
    nkc = t_k // tk
    k4 = k.reshape(nh, nkc, tk, HEAD_PAD)
    vt4 = jnp.swapaxes(v[:, :, :VT_ROWS].reshape(nh, nkc, tk, VT_ROWS), 2, 3)
    return pl.pallas_call(
        _attn_kernel,
        grid=(nh, n_q_blocks),
        in_specs=[pl.BlockSpec((None, tq, HEAD_PAD), lambda h, j: (h, j, 0)),
                  pl.BlockSpec((None, nkc, tk, HEAD_PAD), lambda h, j: (h, 0, 0, 0)),
                  pl.BlockSpec((None, nkc, VT_ROWS, tk), lambda h, j: (h, 0, 0, 0))],
        out_specs=pl.BlockSpec((tq, HEAD_PAD), lambda h, j: (j, h)),
        out_shape=jax.ShapeDtypeStruct((n_q_blocks * tq, nh * HEAD_PAD), BF16),
        scratch_shapes=[pltpu.VMEM((tk, tq), F32), pltpu.VMEM((tk, tq), F32),
                        pltpu.VMEM((tk, tq), BF16), pltpu.VMEM((tk, tq), BF16), pltpu.VMEM((VT_ROWS, tq), F32)],
        compiler_params=_cparams(("arbitrary", "arbitrary")),
        name="attention",
    )(q, k4, vt4)


def _top2_sum(a, b, c, d):
    hi1, lo1 = jnp.maximum(a, b), jnp.minimum(a, b)
    hi2, lo2 = jnp.maximum(c, d), jnp.minimum(c, d)
    return jnp.maximum(hi1, hi2) + jnp.maximum(jnp.minimum(hi1, hi2), jnp.maximum(lo1, lo2))


def _gates_t(aff, sel, gt_ref, grp_ref):
    rows = [sel[e:e + 1, :] for e in range(N_EXPERTS)]
    g_score = [_top2_sum(*rows[g * EXPERTS_PER_GROUP:(g + 1) * EXPERTS_PER_GROUP]) for g in range(N_GROUPS)]
    best = g_score[0]
    best_g = jnp.zeros_like(best, dtype=jnp.int32)
    for g in range(1, N_GROUPS):
        better = g_score[g] > best
        best = jnp.where(better, g_score[g], best)
        best_g = jnp.where(better, g, best_g)
    picked = []
    for e in range(N_EXPERTS):
        g = e // EXPERTS_PER_GROUP
        rank = jnp.zeros_like(best_g)
        for o in range(g * EXPERTS_PER_GROUP, (g + 1) * EXPERTS_PER_GROUP):
            if o == e:
                continue
            ahead = (rows[o] >= rows[e]) if o < e else (rows[o] > rows[e])
            rank = rank + jnp.where(ahead, 1, 0)
        picked.append(jnp.where(best_g == g, rank, 2) < 2)
    w = [jnp.where(picked[e], aff[e:e + 1, :], 0.0) for e in range(N_EXPERTS)]
    total = w[0]
    for e in range(1, N_EXPERTS):
        total = total + w[e]
    for e in range(N_EXPERTS):
        gt_ref[e:e + 1, :] = w[e] / total
    grp_ref[...] = best_g


OUT_HALF = D_MODEL // 2


def _outproj_kernel(x_ref, xc_ref, conv_ref, of_ref, ob_ref, att_ref, attc_ref, w_ref, mod_ref, nw_ref, rw_ref,
                    rb_ref, x1_ref, h2_ref, gt_ref, grp_ref, *, n_lat_tiles):
    is_lat = pl.program_id(0) < n_lat_tiles
    ret = (of_ref[...].astype(F32) + ob_ref[...].astype(F32)).astype(BF16)
    att = jnp.where(is_lat, att_ref[...], attc_ref[...])
    mix = jnp.concatenate([conv_ref[...], ret, att], axis=1)
    halves = []
    ssq = 0.0
    for hf in range(2):
        cols = slice(hf * OUT_HALF, (hf + 1) * OUT_HALF)
        o = jnp.dot(mix, w_ref[:, cols], preferred_element_type=F32)
        x1 = jnp.where(is_lat, x_ref[:, cols], xc_ref[:, cols]) + mod_ref[2:3, cols] * o
        x1_ref[:, cols] = x1
        ssq = ssq + jnp.sum(x1 * x1, axis=-1, keepdims=True)
        halves.append(x1)
    inv = lax.rsqrt(ssq * (1.0 / D_MODEL) + EPS)
    logit_parts = 0.0
    for hf in range(2):
        cols = slice(hf * OUT_HALF, (hf + 1) * OUT_HALF)
        h2 = (halves[hf] * inv * nw_ref[:, cols]) * (1.0 + mod_ref[4:5, cols]) + mod_ref[3:4, cols]
        hi = h2.astype(BF16)
        h2_ref[:, cols] = hi
        lo = (h2 - hi.astype(F32)).astype(BF16)
        logit_parts = (logit_parts + jnp.dot(hi, rw_ref[cols, :], preferred_element_type=F32)
                       + jnp.dot(lo, rw_ref[cols, :], preferred_element_type=F32))
    lt = logit_parts.T
    logits = lt[0:N_EXPERTS, :] + lt[N_EXPERTS:2 * N_EXPERTS, :]
    aff = jax.nn.sigmoid(logits)
    _gates_t(aff, aff + rb_ref[...], gt_ref, grp_ref)


def _outproj_call(x_lat, x_ctx, ctx_block0, conv, of, ob, att, att_c, w_ext, mod, nw, rw3, rb, n_lat_tiles,
                  n_tiles):
    d = x_lat.shape[1]
    tm = ROW_TILE
    row = lambda i: (i, 0)
    const2 = lambda i: (0, 0)
    lat = lambda i: (jnp.minimum(i, n_lat_tiles - 1), 0)
    cls = lambda i: (jnp.where(i < n_lat_tiles, 1, 0), 0, 0)
    return pl.pallas_call(
        functools.partial(_outproj_kernel, n_lat_tiles=n_lat_tiles),
        grid=(n_tiles,),
        in_specs=[pl.BlockSpec((tm, d), lat),
                  pl.BlockSpec((tm, d), lambda i: (ctx_block0 + jnp.maximum(i - n_lat_tiles, 0), 0)),
                  pl.BlockSpec((tm, CONV_W), row),
                  pl.BlockSpec((tm, RET_W), row),
                  pl.BlockSpec((tm, RET_W), row),
                  pl.BlockSpec((tm, ATT_W), lat),
                  pl.BlockSpec((tm, ATT_W), lambda i: (jnp.maximum(i - n_lat_tiles, 0), 0)),
                  pl.BlockSpec(w_ext.shape, const2),
                  pl.BlockSpec((None, 8, d), cls),
                  pl.BlockSpec((1, d), const2),
                  pl.BlockSpec((d, LANES), const2),
                  pl.BlockSpec((N_EXPERTS, 1), const2)],
        out_specs=(pl.BlockSpec((tm, d), row), pl.BlockSpec((tm, d), row),
                   pl.BlockSpec((N_EXPERTS, tm), lambda i: (0, i)), pl.BlockSpec((1, tm), lambda i: (0, i))),
        out_shape=(jax.ShapeDtypeStruct((n_tiles * tm, d), F32), jax.ShapeDtypeStruct((n_tiles * tm, d), BF16),
                   jax.ShapeDtypeStruct((N_EXPERTS, n_tiles * tm), F32),
                   jax.ShapeDtypeStruct((1, n_tiles * tm), jnp.int32)),
        compiler_params=_cparams(("arbitrary",)),
        name="outproj",
    )(x_lat, x_ctx, conv, of, ob, att, att_c, w_ext, mod, nw, rw3, rb)


MOE_DENSE_ROWS = 256


def _group_mlp(xb, gate4, wgu_ref, wd_ref):
    parts = []
    for k in range(EXPERTS_PER_GROUP):
        gu = jnp.dot(xb, wgu_ref[k], preferred_element_type=F32)
        parts.append((_silu(gu[:, :D_EXPERT]) * gu[:, D_EXPERT:] * gate4[:, k:k + 1]).astype(BF16))
    return jnp.dot(jnp.concatenate(parts, axis=1), wd_ref[...], preferred_element_type=F32)


def _moe_kernel(h_ref, gate8_ref, grp_ref, x1_ref, g2_ref, tri_ref, wgu_ref, wd_ref, *rest, cap, final_norm):
    rest = list(rest)
    fw_ref = rest.pop(0) if final_norm else None
    (o_ref,) = rest
    tm = h_ref.shape[0]
    grp = grp_ref[...]
    rows8 = lax.broadcasted_iota(jnp.int32, (SUBLANES, tm), 0)
    member8 = jnp.where(rows8 == grp, 1.0, 0.0).astype(BF16)
    incl8 = jnp.dot(member8, tri_ref[...], preferred_element_type=F32)
    largest = jnp.max(incl8)

    def finish(moe):
        x2 = x1_ref[...] + g2_ref[...] * moe
        if final_norm:
            x2 = x2 * lax.rsqrt(jnp.mean(x2 * x2, axis=-1, keepdims=True) + EPS) * fw_ref[...]
        o_ref[...] = x2

    def gates_of(g8):
        return g8[:, :EXPERTS_PER_GROUP] + g8[:, EXPERTS_PER_GROUP:]

    @pl.when(largest <= cap)
    def _():
        slot = lax.broadcasted_iota(jnp.int32, (cap, tm), 0)
        onehots = []
        for g in range(N_GROUPS):
            pos = jnp.where(grp == g, incl8[g:g + 1, :].astype(jnp.int32) - 1, -1)
            onehots.append(jnp.where(slot == pos, 1.0, 0.0).astype(BF16))
        onehot = jnp.concatenate(onehots, axis=0)
        xg = jnp.dot(onehot, h_ref[...], preferred_element_type=F32).astype(BF16)
        ys = []
        for g in range(N_GROUPS):
            g8 = jnp.dot(onehots[g], gate8_ref[g], preferred_element_type=F32)
            ys.append(_group_mlp(xg[g * cap:(g + 1) * cap, :], gates_of(g8), wgu_ref.at[g], wd_ref.at[g])
                      .astype(BF16))
        finish(lax.dot_general(onehot, jnp.concatenate(ys, axis=0), (((0,), (0,)), ((), ())),
                               preferred_element_type=F32))

    @pl.when(largest > cap)
    def _():
        def chunk(ci, carry):
            rows = pl.ds(pl.multiple_of(ci * MOE_DENSE_ROWS, MOE_DENSE_ROWS), MOE_DENSE_ROWS)
            moe = jnp.zeros((MOE_DENSE_ROWS, o_ref.shape[1]), F32)
            for g in range(N_GROUPS):
                moe = moe + _group_mlp(h_ref[rows, :], gates_of(gate8_ref[g, rows, :].astype(F32)),
                                       wgu_ref.at[g], wd_ref.at[g])
            o_ref[rows, :] = moe
            return carry

        lax.fori_loop(0, tm // MOE_DENSE_ROWS, chunk, 0)
        finish(o_ref[...])


def _moe_call(h2, gate8, grp, x1, g2, wgu, wd, tm, block0, n_tiles, final_w=None):
    out_rows, d = x1.shape
    cap = tm * 5 // 16
    assert cap % 16 == 0 and tm % MOE_DENSE_ROWS == 0
    tri = jnp.asarray(np.triu(np.ones((tm, tm), np.float32)), dtype=BF16)
    rows = lambda i: (block0 + i, 0)
    const2 = lambda i: (0, 0)
    const3 = lambda i: (0, 0, 0)
    once = pl.Buffered(1)
    in_specs = [pl.BlockSpec((tm, d), rows),
                pl.BlockSpec((N_GROUPS, tm, 2 * EXPERTS_PER_GROUP), lambda i: (0, block0 + i, 0)),
                pl.BlockSpec((1, tm), lambda i: (0, block0 + i)),
                pl.BlockSpec((tm, d), rows),
                pl.BlockSpec((1, d), const2),
                pl.BlockSpec((tm, tm), const2, pipeline_mode=once),
                pl.BlockSpec(wgu.shape, lambda i: (0, 0, 0, 0), pipeline_mode=once),
                pl.BlockSpec(wd.shape, const3, pipeline_mode=once)]
    args = [h2, gate8, grp, x1, g2, tri, wgu, wd]
    if final_w is not None:
        in_specs.append(pl.BlockSpec((1, d), const2))
        args.append(final_w)
    return pl.pallas_call(
        functools.partial(_moe_kernel, cap=cap, final_norm=final_w is not None),
        grid=(n_tiles,),
        in_specs=in_specs,
        out_specs=pl.BlockSpec((tm, d), rows),
        out_shape=jax.ShapeDtypeStruct((out_rows, d), F32),
        input_output_aliases={3: 0},
        compiler_params=_cparams(("arbitrary",)),
        name="experts",
    )(*args)


_SWAP32 = np.concatenate([np.arange(8, 16), np.arange(0, 8), np.arange(24, 32), np.arange(16, 24)])


def _pad_cols(w, width):
    return jnp.pad(w, ((0, 0), (0, width - w.shape[1])))


def _in_weight(w_in):
    sizes = (2 * CONV_W, RET_QK_W, RET_QK_W, RET_W, RET_W, RET_W, MLA_Q_RANK, MLA_KV_RANK, MLA_D_ROPE)
    offs = np.concatenate([[0], np.cumsum(sizes)])
    conv, rq, rk, rv, gf, gb, cq, ckv, kr = [w_in[:, offs[i]:offs[i + 1]] for i in range(len(sizes))]
    swap192 = np.concatenate([h * ROPE_DIM + _SWAP32 for h in range(RET_HEADS)])
    d = w_in.shape[0]

    def place_rope(w):
        return jnp.concatenate([jnp.zeros((d, MLA_D_NOPE), w.dtype), w,
                                jnp.zeros((d, LANES - MLA_D_NOPE - MLA_D_ROPE), w.dtype)], axis=1)

    ext = jnp.concatenate([
        conv, _pad_cols(rq, 256), _pad_cols(rq[:, swap192], 256), _pad_cols(rk, 256), _pad_cols(rk[:, swap192], 256),
        rv, gf, gb, _pad_cols(cq, 256), ckv, place_rope(kr), place_rope(kr[:, _SWAP32])], axis=1)
    assert ext.shape[1] == IN_EXT
    return ext.astype(BF16)


def _uq_weight(w_uq):
    r = w_uq.shape[0]
    w = w_uq.reshape(r, MLA_HEADS, MLA_D_NOPE + MLA_D_ROPE)
    nope, rope = w[..., :MLA_D_NOPE], w[..., MLA_D_NOPE:]
    zpad = jnp.zeros((r, MLA_HEADS, HEAD_PAD - MLA_D_NOPE - MLA_D_ROPE), w.dtype)
    main = jnp.concatenate([nope, rope, zpad], axis=-1).reshape(r, ATT_W)
    part = jnp.concatenate([jnp.zeros_like(nope), rope[..., _SWAP32], zpad], axis=-1).reshape(r, ATT_W)
    return jnp.concatenate([main, part], axis=1).astype(BF16)


def _ukv_weight(w_ukv):
    r = w_ukv.shape[0]
    w = w_ukv.reshape(r, MLA_HEADS, MLA_D_NOPE + MLA_D_V)
    zpad = jnp.zeros((r, MLA_HEADS, HEAD_PAD - MLA_D_NOPE), w.dtype)
    kpart = jnp.concatenate([w[..., :MLA_D_NOPE], zpad], axis=-1).reshape(r, ATT_W)
    vpart = jnp.concatenate([w[..., MLA_D_NOPE:], zpad], axis=-1).reshape(r, ATT_W)
    return jnp.concatenate([kpart, vpart], axis=1).astype(BF16)


def _out_weight(w_out):
    d = w_out.shape[1]
    conv, ret = w_out[:CONV_W], w_out[CONV_W:CONV_W + RET_W]
    att = w_out[CONV_W + RET_W:].reshape(MLA_HEADS, MLA_D_V, d)
    att = jnp.pad(att, ((0, 0), (0, HEAD_PAD - MLA_D_V), (0, 0))).reshape(ATT_W, d)
    return jnp.concatenate([conv, ret, att], axis=0).astype(BF16)


def _rope_tables(n_ctx, seq):
    f32 = np.float32
    inv = (f32(ROPE_BASE) ** (-np.arange(ROPE_PAIRS, dtype=f32) / f32(ROPE_PAIRS))).astype(f32)
    ar = (np.arange(seq // GRID_W, dtype=f32)[:, None] * inv).astype(f32)
    ac = (np.arange(GRID_W, dtype=f32)[:, None] * inv).astype(f32)
    row_cs = np.concatenate([np.cos(ar), np.cos(ar)], axis=1).astype(f32)
    row_sn = np.concatenate([-np.sin(ar), np.sin(ar)], axis=1).astype(f32)
    col_cs = np.concatenate([np.cos(ac), np.cos(ac)], axis=1).astype(f32)
    col_sn = np.concatenate([-np.sin(ac), np.sin(ac)], axis=1).astype(f32)

    def expand(row_t, col_t, ctx_value):
        rows = seq // GRID_W
        lat = jnp.concatenate([jnp.broadcast_to(jnp.asarray(row_t)[:, None, :], (rows, GRID_W, 16)),
                               jnp.broadcast_to(jnp.asarray(col_t)[None, :, :], (rows, GRID_W, 16))],
                              axis=-1).reshape(seq, ROPE_DIM)
        full = jnp.concatenate([lat, jnp.full((n_ctx, ROPE_DIM), ctx_value, F32)], axis=0)
        return jnp.tile(full, (1, LANES // ROPE_DIM))

    return expand(row_cs, col_cs, 1.0), expand(row_sn, col_sn, 0.0)


def _router_weight(router_w):
    hi = router_w.astype(BF16)
    lo = (router_w - hi.astype(F32)).astype(BF16)
    return jnp.pad(jnp.concatenate([hi, lo], axis=1), ((0, 0), (0, LANES - 2 * N_EXPERTS)))


def _group_gates(gates_t):
    t = gates_t.shape[1]
    g4 = jnp.transpose(gates_t.reshape(N_GROUPS, EXPERTS_PER_GROUP, t), (0, 2, 1))
    hi = g4.astype(BF16)
    lo = (g4 - hi.astype(F32)).astype(BF16)
    return jnp.concatenate([hi, lo], axis=-1)


def _group_weights(w_gate, w_up, w_down):
    e, d, f = w_gate.shape
    gu = jnp.concatenate([w_gate, w_up], axis=-1).astype(BF16)
    gu = gu.reshape(N_GROUPS, EXPERTS_PER_GROUP, d, 2 * f)
    wd = w_down.astype(BF16).reshape(N_GROUPS, EXPERTS_PER_GROUP * f, d)
    return gu, wd


def kernel(x, c, ctx, c_ctx, ada_w, ada_b, norm1_w, norm2_w, w_in, conv_dw, conv_b, conv_ln_w, conv_ln_b,
           mla_q_norm_w, mla_w_uq, mla_kv_norm_w, mla_w_ukv, w_out, router_w, router_bias,
           moe_w_gate, moe_w_up, moe_w_down, final_norm_w):
    batch, seq, d = x.shape
    n_ctx = ctx.shape[1]
    depth = ada_w.shape[0]
    t_all = seq + n_ctx
    assert batch == 1 and d == D_MODEL
    assert n_ctx % ROW_TILE == 0 and seq % ROW_TILE == 0 and n_ctx <= MOE_TILE and seq % MOE_TILE == 0
    assert seq % ATT_TQ == 0 and t_all % ATT_TK == 0 and n_ctx % RET_CHUNK == 0 and seq % n_ctx == 0
    n_lat_tiles = seq // ROW_TILE
    n_all_tiles = t_all // ROW_TILE

    cond = jnp.zeros((8, d), F32).at[0].set(c_ctx).at[1].set(c[0])
    mod = _ada_call(cond, ada_w, ada_b)[:, :2, :].reshape(depth, 2, 6, d)
    mod8 = jnp.pad(mod, ((0, 0), (0, 0), (0, 2), (0, 0)))
    cs, sn = _rope_tables(n_ctx, seq)
    ret_tabs = _ret_tables()
    rw3 = _router_weight(router_w)
    rb = router_bias.reshape(N_EXPERTS, 1)

    x_lat, x_ctx, ctx_block0 = x[0], ctx[0], 0
    for l in range(depth):
        last = l == depth - 1
        u, rq, rk, rv, gf, gb, q_att, k_att, v_att = _inproj_call(
            x_lat, x_ctx, ctx_block0, t_all, norm1_w[l][None, :], mod8[l][:, :2, :], _in_weight(w_in[l]), cs, sn,
            mla_q_norm_w[l][None, :], _uq_weight(mla_w_uq[l]), mla_kv_norm_w[l][None, :], _ukv_weight(mla_w_ukv[l]),
            n_lat_tiles)
        conv = _conv_call(u, conv_dw[l], conv_b[l][None, :], conv_ln_w[l][None, :], conv_ln_b[l][None, :],
                          n_lat_tiles)
        of, ob = _ret_call(rq, rk, rv, gf, gb, ret_tabs, seq // RET_CHUNK)
        att = _attn_call(q_att, k_att, v_att, ATT_TQ, ATT_TK, seq // ATT_TQ)
        att_c = att
        if not last:
            tqc = min(ATT_TQ, n_ctx)
            att_c = _attn_call(q_att[:, seq:], k_att[:, seq:], v_att[:, seq:], tqc, n_ctx, n_ctx // tqc)
        n_tiles = n_lat_tiles if last else n_all_tiles
        x1, h2, gates_t, grp = _outproj_call(x_lat, x_ctx, ctx_block0, conv, of, ob, att, att_c,
                                             _out_weight(w_out[l]), mod8[l], norm2_w[l][None, :], rw3, rb,
                                             n_lat_tiles, n_tiles)
        gate8 = _group_gates(gates_t)
        wgu, wd = _group_weights(moe_w_gate[l], moe_w_up[l], moe_w_down[l])
        x_all = _moe_call(h2, gate8, grp, x1, mod[l, 1, 5][None, :], wgu, wd, MOE_TILE, 0, seq // MOE_TILE,
                          final_w=final_norm_w[None, :] if last else None)
        if not last:
            x_all = _moe_call(h2, gate8, grp, x_all, mod[l, 0, 5][None, :], wgu, wd, n_ctx, seq // n_ctx, 1)
        x_lat, x_ctx, ctx_block0 = x_all, x_all, n_lat_tiles
    return x_all[None]
```

```python
import functools
import math

import numpy as np
import jax
import jax.numpy as jnp
from jax import lax
from jax.experimental import pallas as pl
from jax.experimental.pallas import tpu as pltpu

F32 = jnp.float32
BF16 = jnp.bfloat16

D_MODEL = 1024
GRID_W = 64
CONV_W = 256
CONV_K = 31
RET_HEADS = 6
RET_DK = 32
RET_DV = 64
RET_QK_W = RET_HEADS * RET_DK
RET_W = RET_HEADS * RET_DV
RET_CHUNK = 128
RET_GN_EPS = 1e-5
MLA_HEADS = 6
MLA_Q_RANK = 192
MLA_KV_RANK = 128
MLA_D_NOPE = 64
MLA_D_ROPE = 32
MLA_D_V = 64
MLA_SCALE = (MLA_D_NOPE + MLA_D_ROPE) ** -0.5
ROPE_DIM = 32
ROPE_PAIRS = ROPE_DIM // 4
ROPE_BASE = 10000.0
N_EXPERTS = 16
N_GROUPS = 4
EXPERTS_PER_GROUP = N_EXPERTS // N_GROUPS
D_EXPERT = 256
EPS = 1e-6

LANES = 128
HEAD_PAD = LANES
ATT_W = MLA_HEADS * HEAD_PAD
LOG2E = math.log2(math.e)

C_CONV = 0
C_RQ = 512
C_RQS = 768
C_RK = 1024
C_RKS = 1280
C_RV = 1536
C_RGF = 1920
C_RGB = 2304
C_CQ = 2688
C_CKV = 2944
C_KR = 3072
C_KRS = 3200
IN_EXT = 3328

ROW_TILE = 256
MOE_TILE = 512
ATT_TQ = 1024
ATT_TK = 1280
VMEM_LIMIT = 48 * 1024 * 1024


def _cparams(sem):
    return pltpu.CompilerParams(dimension_semantics=sem, vmem_limit_bytes=VMEM_LIMIT)


def _silu(x):
    return x * jax.nn.sigmoid(x)


def _ada_kernel(c_ref, w_ref, b_ref, o_ref):
    s = _silu(c_ref[...])
    o_ref[...] = jnp.dot(s, w_ref[...], preferred_element_type=F32,
                         precision=lax.Precision.HIGHEST) + b_ref[...]


def _ada_call(cond, ada_w, ada_b):
    depth, d, n = ada_w.shape
    tn = 1536
    return pl.pallas_call(
        _ada_kernel,
        grid=(depth, n // tn),
        in_specs=[pl.BlockSpec((8, d), lambda l, j: (0, 0)),
                  pl.BlockSpec((None, d, tn), lambda l, j: (l, 0, j)),
                  pl.BlockSpec((None, 1, tn), lambda l, j: (l, 0, j))],
        out_specs=pl.BlockSpec((None, 8, tn), lambda l, j: (l, 0, j)),
        out_shape=jax.ShapeDtypeStruct((depth, 8, n), F32),
        compiler_params=_cparams(("arbitrary", "arbitrary")),
        name="adaln",
    )(cond, ada_w, ada_b.reshape(depth, 1, n))


def _inproj_kernel(x_ref, xc_ref, nw_ref, mod_ref, w_ref, cs_ref, sn_ref, qnw_ref, wuq_ref, kvnw_ref, wukv_ref,
                   u_ref, rq_ref, rk_ref, rv_ref, gf_ref, gb_ref, q_ref, k_ref, v_ref, *, n_lat_tiles):
    x = jnp.where(pl.program_id(0) < n_lat_tiles, x_ref[...], xc_ref[...])
    y = x * lax.rsqrt(jnp.mean(x * x, axis=-1, keepdims=True) + EPS) * nw_ref[...]
    h = (y * (1.0 + mod_ref[1:2, :]) + mod_ref[0:1, :]).astype(BF16)

    def proj(c0, width):
        return jnp.dot(h, w_ref[:, c0:c0 + width], preferred_element_type=F32)

    cs = cs_ref[...]
    sn = sn_ref[...]
    lane = lax.broadcasted_iota(jnp.int32, (1, LANES), 1)
    rope_lanes = (lane >= MLA_D_NOPE) & (lane < MLA_D_NOPE + MLA_D_ROPE)
    csq = jnp.where(rope_lanes, cs, 1.0)
    snq = jnp.where(rope_lanes, sn, 0.0)

    ag = proj(C_CONV, 2 * CONV_W)
    u_ref[...] = ag[:, :CONV_W] * jax.nn.sigmoid(ag[:, CONV_W:])

    cs192 = jnp.concatenate([cs, cs[:, :RET_QK_W - LANES]], axis=1)
    sn192 = jnp.concatenate([sn, sn[:, :RET_QK_W - LANES]], axis=1)
    rq = proj(C_RQ, 256)[:, :RET_QK_W] * cs192 + proj(C_RQS, 256)[:, :RET_QK_W] * sn192
    rq_ref[...] = rq.astype(BF16)
    rk = proj(C_RK, 256)[:, :RET_QK_W] * cs192 + proj(C_RKS, 256)[:, :RET_QK_W] * sn192
    rk_ref[...] = (rk * (RET_DK ** -0.5)).astype(BF16)
    rv_ref[...] = proj(C_RV, RET_W).astype(BF16)
    gf_ref[...] = _silu(proj(C_RGF, RET_W))
    gb_ref[...] = _silu(proj(C_RGB, RET_W))

    cq = proj(C_CQ, 256)[:, :MLA_Q_RANK]
    cqn = (cq * lax.rsqrt(jnp.mean(cq * cq, axis=-1, keepdims=True) + EPS) * qnw_ref[...]).astype(BF16)
    qa = jnp.dot(cqn, wuq_ref[:, :ATT_W], preferred_element_type=F32)
    qb = jnp.dot(cqn, wuq_ref[:, ATT_W:], preferred_element_type=F32)
    ckv = proj(C_CKV, MLA_KV_RANK)
    ckvn = (ckv * lax.rsqrt(jnp.mean(ckv * ckv, axis=-1, keepdims=True) + EPS) * kvnw_ref[...]).astype(BF16)
    ka = jnp.dot(ckvn, wukv_ref[:, :ATT_W], preferred_element_type=F32)
    va = jnp.dot(ckvn, wukv_ref[:, ATT_W:], preferred_element_type=F32)
    kr = proj(C_KR, LANES) * csq + proj(C_KRS, LANES) * snq
    ones_lane = (lane == MLA_D_V).astype(F32)
    for hd in range(MLA_HEADS):
        sl = slice(hd * HEAD_PAD, (hd + 1) * HEAD_PAD)
        q_ref[hd] = ((qa[:, sl] * csq + qb[:, sl] * snq) * (MLA_SCALE * LOG2E)).astype(BF16)
        k_ref[hd] = (ka[:, sl] + kr).astype(BF16)
        v_ref[hd] = (va[:, sl] + ones_lane).astype(BF16)


def _inproj_call(x_lat, x_ctx, ctx_block0, t_all, nw, mod, w_ext, cs, sn, qnw, wuq, kvnw, wukv, n_lat_tiles):
    d = x_lat.shape[1]
    tm = ROW_TILE
    row = lambda i: (i, 0)
    const2 = lambda i: (0, 0)
    head = lambda i: (0, i, 0)
    cls = lambda i: (jnp.where(i < n_lat_tiles, 1, 0), 0, 0)
    out_shape = (
        jax.ShapeDtypeStruct((t_all, CONV_W), F32),
        jax.ShapeDtypeStruct((t_all, RET_QK_W), BF16),
        jax.ShapeDtypeStruct((t_all, RET_QK_W), BF16),
        jax.ShapeDtypeStruct((t_all, RET_W), BF16),
        jax.ShapeDtypeStruct((t_all, RET_W), F32),
        jax.ShapeDtypeStruct((t_all, RET_W), F32),
        jax.ShapeDtypeStruct((MLA_HEADS, t_all, HEAD_PAD), BF16),
        jax.ShapeDtypeStruct((MLA_HEADS, t_all, HEAD_PAD), BF16),
        jax.ShapeDtypeStruct((MLA_HEADS, t_all, HEAD_PAD), BF16),
    )
    out_specs = (
        pl.BlockSpec((tm, CONV_W), row),
        pl.BlockSpec((tm, RET_QK_W), row),
        pl.BlockSpec((tm, RET_QK_W), row),
        pl.BlockSpec((tm, RET_W), row),
        pl.BlockSpec((tm, RET_W), row),
        pl.BlockSpec((tm, RET_W), row),
        pl.BlockSpec((MLA_HEADS, tm, HEAD_PAD), head),
        pl.BlockSpec((MLA_HEADS, tm, HEAD_PAD), head),
        pl.BlockSpec((MLA_HEADS, tm, HEAD_PAD), head),
    )
    return pl.pallas_call(
        functools.partial(_inproj_kernel, n_lat_tiles=n_lat_tiles),
        grid=(t_all // tm,),
        in_specs=[pl.BlockSpec((tm, d), lambda i: (jnp.minimum(i, n_lat_tiles - 1), 0)),
                  pl.BlockSpec((tm, d), lambda i: (ctx_block0 + jnp.maximum(i - n_lat_tiles, 0), 0)),
                  pl.BlockSpec((1, d), const2),
                  pl.BlockSpec((None, 2, d), cls),
                  pl.BlockSpec((d, IN_EXT), const2),
                  pl.BlockSpec((tm, LANES), row),
                  pl.BlockSpec((tm, LANES), row),
                  pl.BlockSpec((1, MLA_Q_RANK), const2),
                  pl.BlockSpec((MLA_Q_RANK, 2 * ATT_W), const2),
                  pl.BlockSpec((1, MLA_KV_RANK), const2),
                  pl.BlockSpec((MLA_KV_RANK, 2 * ATT_W), const2)],
        out_specs=out_specs,
        out_shape=out_shape,
        compiler_params=_cparams(("arbitrary",)),
        name="inproj",
    )(x_lat, x_ctx, nw, mod, w_ext, cs, sn, qnw, wuq, kvnw, wukv)


CONV_HALO = 16
CONV_SUB = 64
SUBLANES = 8


def _conv_kernel(prev_ref, cur_ref, next_ref, dw_ref, b_ref, lnw_ref, lnb_ref, o_ref, ext_ref, sh_ref, *,
                 n_lat_tiles):
    i = pl.program_id(0)
    n = pl.num_programs(0)
    tm = cur_ref.shape[0]
    seq_start = (i == 0) | (i == n_lat_tiles)
    seq_end = (i == n_lat_tiles - 1) | (i == n - 1)
    ext_ref[0:CONV_HALO, :] = jnp.where(seq_start, 0.0, prev_ref[...])
    ext_ref[CONV_HALO:CONV_HALO + tm, :] = cur_ref[...]
    ext_ref[CONV_HALO + tm:, :] = jnp.where(seq_end, 0.0, next_ref[...])
    span = sh_ref.shape[1]
    for ph in range(SUBLANES):
        sh_ref[ph] = ext_ref[pl.ds(ph, span), :]
    base = CONV_HALO - CONV_K // 2
    for r in range(tm // CONV_SUB):
        acc = jnp.zeros((CONV_SUB, CONV_W), F32) + b_ref[...]
        for k in range(CONV_K):
            off = base + k
            acc = acc + (sh_ref[off % SUBLANES, pl.ds(r * CONV_SUB + off - off % SUBLANES, CONV_SUB), :]
                         * dw_ref[k:k + 1, :])
        mu = jnp.mean(acc, axis=-1, keepdims=True)
        dlt = acc - mu
        var = jnp.mean(dlt * dlt, axis=-1, keepdims=True)
        y = dlt * lax.rsqrt(var + EPS) * lnw_ref[...] + lnb_ref[...]
        o_ref[r * CONV_SUB:(r + 1) * CONV_SUB, :] = _silu(y).astype(BF16)


def _conv_call(u, dw, b, lnw, lnb, n_lat_tiles):
    t_all = u.shape[0]
    tm = ROW_TILE
    hpt = tm // CONV_HALO
    n_halo = t_all // CONV_HALO
    const2 = lambda i: (0, 0)
    return pl.pallas_call(
        functools.partial(_conv_kernel, n_lat_tiles=n_lat_tiles),
        grid=(t_all // tm,),
        in_specs=[pl.BlockSpec((CONV_HALO, CONV_W), lambda i: (jnp.maximum(i * hpt - 1, 0), 0)),
                  pl.BlockSpec((tm, CONV_W), lambda i: (i, 0)),
                  pl.BlockSpec((CONV_HALO, CONV_W), lambda i: (jnp.minimum((i + 1) * hpt, n_halo - 1), 0)),
                  pl.BlockSpec((CONV_K, CONV_W), const2),
                  pl.BlockSpec((1, CONV_W), const2),
                  pl.BlockSpec((1, CONV_W), const2),
                  pl.BlockSpec((1, CONV_W), const2)],
        out_specs=pl.BlockSpec((tm, CONV_W), lambda i: (i, 0)),
        out_shape=jax.ShapeDtypeStruct((t_all, CONV_W), BF16),
        scratch_shapes=[pltpu.VMEM((tm + 2 * CONV_HALO, CONV_W), F32),
                        pltpu.VMEM((SUBLANES, tm + 2 * CONV_HALO - SUBLANES, CONV_W), F32)],
        compiler_params=_cparams(("arbitrary",)),
        name="conv",
    )(u, u, u, dw, b, lnw, lnb)


def _split_dot(x, a):
    hi = x.astype(BF16)
    lo = (x - hi.astype(F32)).astype(BF16)
    return (jnp.dot(hi, a, preferred_element_type=F32) + jnp.dot(lo, a, preferred_element_type=F32))


def _ret_direction(q, k, v, gate, r_ref, dmask, xi, zeta, gchunk, bdmask, avg, o_ref):
    lane_k = lax.broadcasted_iota(jnp.int32, (1, RET_QK_W), 1) // RET_DK
    lane_v = lax.broadcasted_iota(jnp.int32, (1, RET_W), 1) // RET_DV
    zero_k = jnp.zeros_like(k)
    zero_v = jnp.zeros_like(v)
    k_bd = jnp.concatenate([jnp.where(lane_k == hd, k, zero_k) for hd in range(RET_HEADS)], axis=0)
    v_bd = jnp.concatenate([jnp.where(lane_v == hd, v, zero_v) for hd in range(RET_HEADS)], axis=0)
    s = lax.dot_general(q, k_bd, (((1,), (1,)), ((), ())), preferred_element_type=F32)
    s = (s * dmask).astype(BF16)
    inner = jnp.dot(s, v_bd, preferred_element_type=F32)
    r = r_ref[...]
    cross = jnp.dot(q, r.astype(BF16), preferred_element_type=F32) * xi
    o = inner + cross
    kz = (k.astype(F32) * zeta).astype(BF16)
    ds = lax.dot_general(kz, v, (((0,), (0,)), ((), ())), preferred_element_type=F32)
    r_ref[...] = gchunk * r + ds * bdmask
    mu = jnp.dot(o.astype(BF16), avg, preferred_element_type=F32)
    dlt = o - mu
    var = jnp.dot((dlt * dlt).astype(BF16), avg, preferred_element_type=F32)
    o_ref[...] = (gate * (dlt * lax.rsqrt(var + RET_GN_EPS))).astype(BF16)


def _ret_kernel(qf_ref, kf_ref, vf_ref, gf_ref, qb_ref, kb_ref, vb_ref, gb_ref,
                dmf_ref, dmb_ref, xif_ref, xib_ref, ztf_ref, ztb_ref, gcf_ref, gcb_ref, bdm_ref, avg_ref,
                of_ref, ob_ref, rf_ref, rb_ref):
    @pl.when(pl.program_id(0) == 0)
    def _():
        rf_ref[...] = jnp.zeros_like(rf_ref)
        rb_ref[...] = jnp.zeros_like(rb_ref)

    bdm = bdm_ref[...]
    avg = avg_ref[...]
    _ret_direction(qf_ref[...], kf_ref[...], vf_ref[...], gf_ref[...], rf_ref, dmf_ref[...], xif_ref[...],
                   ztf_ref[...], gcf_ref[...], bdm, avg, of_ref)
    _ret_direction(qb_ref[...], kb_ref[...], vb_ref[...], gb_ref[...], rb_ref, dmb_ref[...], xib_ref[...],
                   ztb_ref[...], gcb_ref[...], bdm, avg, ob_ref)


def _ret_tables():
    c = RET_CHUNK
    f32 = np.float32
    gamma_f = (1.0 - 2.0 ** (-5.0 - np.arange(RET_HEADS, dtype=f32))).astype(f32)
    gamma_b = gamma_f[::-1]
    idx = np.arange(c, dtype=f32)
    diff = idx[:, None] - idx[None, :]

    def tables(gamma, reverse):
        lg = np.log(gamma).astype(f32)
        d = -diff if reverse else diff
        dm = np.where(d[None] >= 0, np.exp(np.maximum(d, 0.0)[None] * lg[:, None, None]), 0.0)
        dm = np.transpose(dm, (1, 0, 2)).reshape(c, RET_HEADS * c)
        xi_e = (c - idx) if reverse else (idx + 1.0)
        zt_e = idx if reverse else (c - 1.0 - idx)
        xi = np.repeat(np.exp(xi_e[:, None] * lg[None, :]), RET_DV, axis=1)
        zt = np.repeat(np.exp(zt_e[:, None] * lg[None, :]), RET_DK, axis=1)
        gc = np.repeat(np.exp(c * lg), RET_DV)[None, :]
        return [t.astype(f32) for t in (dm, xi, zt, gc)]

    dmf, xif, ztf, gcf = tables(gamma_f, False)
    dmb, xib, ztb, gcb = tables(gamma_b, True)
    hk = np.arange(RET_QK_W) // RET_DK
    hv = np.arange(RET_W) // RET_DV
    bdm = (hk[:, None] == hv[None, :]).astype(f32)
    avg = jnp.asarray((hv[:, None] == hv[None, :]).astype(f32) / RET_DV, dtype=BF16)
    return tuple(jnp.asarray(t) for t in (dmf, dmb, xif, xib, ztf, ztb, gcf, gcb, bdm)) + (avg,)


def _ret_call(rq, rk, rv, gf, gb, tabs, n_lat_chunks):
    t_all = rq.shape[0]
    c = RET_CHUNK
    n = t_all // c
    n_ctx_chunks = n - n_lat_chunks

    def fwd(i):
        return (jnp.where(i < n_ctx_chunks, n_lat_chunks + i, i - n_ctx_chunks), 0)

    def bwd(i):
        return (n - 1 - i, 0)

    const2 = lambda i: (0, 0)
    tab_specs = [pl.BlockSpec(t.shape, const2) for t in tabs]
    return pl.pallas_call(
        _ret_kernel,
        grid=(n,),
        in_specs=[pl.BlockSpec((c, RET_QK_W), fwd), pl.BlockSpec((c, RET_QK_W), fwd),
                  pl.BlockSpec((c, RET_W), fwd), pl.BlockSpec((c, RET_W), fwd),
                  pl.BlockSpec((c, RET_QK_W), bwd), pl.BlockSpec((c, RET_QK_W), bwd),
                  pl.BlockSpec((c, RET_W), bwd), pl.BlockSpec((c, RET_W), bwd)] + tab_specs,
        out_specs=(pl.BlockSpec((c, RET_W), fwd), pl.BlockSpec((c, RET_W), bwd)),
        out_shape=(jax.ShapeDtypeStruct((t_all, RET_W), BF16), jax.ShapeDtypeStruct((t_all, RET_W), BF16)),
        scratch_shapes=[pltpu.VMEM((RET_QK_W, RET_W), F32), pltpu.VMEM((RET_QK_W, RET_W), F32)],
        compiler_params=_cparams(("arbitrary",)),
        name="retention",
    )(rq, rk, rv, gf, rq, rk, rv, gb, *tabs)


ATT_SUB = 128
VT_ROWS = HEAD_PAD


def _attn_kernel(q_ref, k_ref, vt_ref, o_ref, s0_scr, s1_scr, p0_scr, p1_scr, acc_scr):
    q = q_ref[...]
    tq = q.shape[0]
    nkc, tk, _ = k_ref.shape
    nsub = tk // ATT_SUB

    s_bufs = (s0_scr, s1_scr)
    p_bufs = (p0_scr, p1_scr)

    def pv(c, par, alpha):
        part = jnp.dot(vt_ref[c], p_bufs[par][...], preferred_element_type=F32)
        acc_scr[...] = alpha * acc_scr[...] + part

    def step(c, par, m_old, m_blk, alpha_prev, with_scores, with_pv):
        if with_pv:
            pv(c - 1, 1 - par, alpha_prev)
        m_new = jnp.maximum(m_old, m_blk)
        alpha = jnp.exp2(m_old - m_new)
        mx = jnp.full((8, tq), -jnp.inf, F32)
        for j in range(nsub):
            rows = pl.ds(j * ATT_SUB, ATT_SUB)
            p_bufs[par][rows, :] = jnp.exp2(s_bufs[par][rows, :] - m_new).astype(BF16)
            if with_scores:
                mx = jnp.maximum(mx, score_rows(c + 1, 1 - par, rows))
        return m_new, jnp.max(mx, axis=0, keepdims=True), alpha

    def score_rows(c, par, rows):
        s = jnp.dot(k_ref[c, rows, :], qt, preferred_element_type=F32)
        s_bufs[par][rows, :] = s
        return jnp.max(s.reshape(ATT_SUB // 8, 8, tq), axis=0)

    def scores(c, par):
        mx = jnp.full((8, tq), -jnp.inf, F32)
        for j in range(nsub):
            mx = jnp.maximum(mx, score_rows(c, par, pl.ds(j * ATT_SUB, ATT_SUB)))
        return jnp.max(mx, axis=0, keepdims=True)

    acc_scr[...] = jnp.zeros_like(acc_scr)
    qt = q.astype(F32).T.astype(BF16)
    m = jnp.full((1, tq), -jnp.inf, F32)
    m_blk = scores(0, 0)
    alpha = jnp.ones((1, tq), F32)
    if nkc > 1:
        m, m_blk, alpha = step(0, 0, m, m_blk, alpha, True, False)
        def body(c, carry):
            return lax.cond(c % 2 == 1,
                            lambda cr: step(c, 1, *cr, True, True),
                            lambda cr: step(c, 0, *cr, True, True), carry)

        m, m_blk, alpha = lax.fori_loop(1, nkc - 1, body, (m, m_blk, alpha))
    last = nkc - 1
    m, _, alpha = step(last, last % 2, m, m_blk, alpha, False, nkc > 1)
    pv(last, last % 2, alpha)
    acc = acc_scr[...]
    out_t = acc / acc[MLA_D_V:MLA_D_V + 1, :]
    if VT_ROWS < HEAD_PAD:
        out_t = jnp.concatenate([out_t, jnp.zeros((HEAD_PAD - VT_ROWS, tq), F32)], axis=0)
    o_ref[...] = out_t.T.astype(BF16)


def _attn_call(q, k, v, tq, tk, n_q_blocks):
    nh = q.shape[0]
    t_k = k.shape[1]
    nkc = t_k // tk
    k4 = k.reshape(nh, nkc, tk, HEAD_PAD)
    vt4 = jnp.swapaxes(v[:, :, :VT_ROWS].reshape(nh, nkc, tk, VT_ROWS), 2, 3)
    return pl.pallas_call(
        _attn_kernel,
        grid=(nh, n_q_blocks),
        in_specs=[pl.BlockSpec((None, tq, HEAD_PAD), lambda h, j: (h, j, 0)),
                  pl.BlockSpec((None, nkc, tk, HEAD_PAD), lambda h, j: (h, 0, 0, 0)),
                  pl.BlockSpec((None, nkc, VT_ROWS, tk), lambda h, j: (h, 0, 0, 0))],
        out_specs=pl.BlockSpec((tq, HEAD_PAD), lambda h, j: (j, h)),
        out_shape=jax.ShapeDtypeStruct((n_q_blocks * tq, nh * HEAD_PAD), BF16),
        scratch_shapes=[pltpu.VMEM((tk, tq), F32), pltpu.VMEM((tk, tq), F32),
                        pltpu.VMEM((tk, tq), BF16), pltpu.VMEM((tk, tq), BF16), pltpu.VMEM((VT_ROWS, tq), F32)],
        compiler_params=_cparams(("arbitrary", "arbitrary")),
        name="attention",
    )(q, k4, vt4)


def _top2_sum(a, b, c, d):
    hi1, lo1 = jnp.maximum(a, b), jnp.minimum(a, b)
    hi2, lo2 = jnp.maximum(c, d), jnp.minimum(c, d)
    return jnp.maximum(hi1, hi2) + jnp.maximum(jnp.minimum(hi1, hi2), jnp.maximum(lo1, lo2))


def _gates_t(aff, sel, gt_ref, grp_ref):
    rows = [sel[e:e + 1, :] for e in range(N_EXPERTS)]
    g_score = [_top2_sum(*rows[g * EXPERTS_PER_GROUP:(g + 1) * EXPERTS_PER_GROUP]) for g in range(N_GROUPS)]
    best = g_score[0]
    best_g = jnp.zeros_like(best, dtype=jnp.int32)
    for g in range(1, N_GROUPS):
        better = g_score[g] > best
        best = jnp.where(better, g_score[g], best)
        best_g = jnp.where(better, g, best_g)
    picked = []
    for e in range(N_EXPERTS):
        g = e // EXPERTS_PER_GROUP
        rank = jnp.zeros_like(best_g)
        for o in range(g * EXPERTS_PER_GROUP, (g + 1) * EXPERTS_PER_GROUP):
            if o == e:
                continue
            ahead = (rows[o] >= rows[e]) if o < e else (rows[o] > rows[e])
            rank = rank + jnp.where(ahead, 1, 0)
        picked.append(jnp.where(best_g == g, rank, 2) < 2)
    w = [jnp.where(picked[e], aff[e:e + 1, :], 0.0) for e in range(N_EXPERTS)]
    total = w[0]
    for e in range(1, N_EXPERTS):
        total = total + w[e]
    for e in range(N_EXPERTS):
        gt_ref[e:e + 1, :] = w[e] / total
    grp_ref[...] = best_g


OUT_HALF = D_MODEL // 2


def _outproj_kernel(x_ref, xc_ref, conv_ref, of_ref, ob_ref, att_ref, attc_ref, w_ref, mod_ref, nw_ref, rw_ref,
                    rb_ref, x1_ref, h2_ref, gt_ref, grp_ref, *, n_lat_tiles):
    is_lat = pl.program_id(0) < n_lat_tiles
    ret = (of_ref[...].astype(F32) + ob_ref[...].astype(F32)).astype(BF16)
    att = jnp.where(is_lat, att_ref[...], attc_ref[...])
    mix = jnp.concatenate([conv_ref[...], ret, att], axis=1)
    halves = []
    ssq = 0.0
    for hf in range(2):
        cols = slice(hf * OUT_HALF, (hf + 1) * OUT_HALF)
        o = jnp.dot(mix, w_ref[:, cols], preferred_element_type=F32)
        x1 = jnp.where(is_lat, x_ref[:, cols], xc_ref[:, cols]) + mod_ref[2:3, cols] * o
        x1_ref[:, cols] = x1
        ssq = ssq + jnp.sum(x1 * x1, axis=-1, keepdims=True)
        halves.append(x1)
    inv = lax.rsqrt(ssq * (1.0 / D_MODEL) + EPS)
    logit_parts = 0.0
    for hf in range(2):
        cols = slice(hf * OUT_HALF, (hf + 1) * OUT_HALF)
        h2 = (halves[hf] * inv * nw_ref[:, cols]) * (1.0 + mod_ref[4:5, cols]) + mod_ref[3:4, cols]
        hi = h2.astype(BF16)
        h2_ref[:, cols] = hi
        lo = (h2 - hi.astype(F32)).astype(BF16)
        logit_parts = (logit_parts + jnp.dot(hi, rw_ref[cols, :], preferred_element_type=F32)
                       + jnp.dot(lo, rw_ref[cols, :], preferred_element_type=F32))
    lt = logit_parts.T
    logits = lt[0:N_EXPERTS, :] + lt[N_EXPERTS:2 * N_EXPERTS, :]
    aff = jax.nn.sigmoid(logits)
    _gates_t(aff, aff + rb_ref[...], gt_ref, grp_ref)


def _outproj_call(x_lat, x_ctx, ctx_block0, conv, of, ob, att, att_c, w_ext, mod, nw, rw3, rb, n_lat_tiles,
                  n_tiles):
    d = x_lat.shape[1]
    tm = ROW_TILE
    row = lambda i: (i, 0)
    const2 = lambda i: (0, 0)
    lat = lambda i: (jnp.minimum(i, n_lat_tiles - 1), 0)
    cls = lambda i: (jnp.where(i < n_lat_tiles, 1, 0), 0, 0)
    return pl.pallas_call(
        functools.partial(_outproj_kernel, n_lat_tiles=n_lat_tiles),
        grid=(n_tiles,),
        in_specs=[pl.BlockSpec((tm, d), lat),
                  pl.BlockSpec((tm, d), lambda i: (ctx_block0 + jnp.maximum(i - n_lat_tiles, 0), 0)),
                  pl.BlockSpec((tm, CONV_W), row),
                  pl.BlockSpec((tm, RET_W), row),
                  pl.BlockSpec((tm, RET_W), row),
                  pl.BlockSpec((tm, ATT_W), lat),
                  pl.BlockSpec((tm, ATT_W), lambda i: (jnp.maximum(i - n_lat_tiles, 0), 0)),
                  pl.BlockSpec(w_ext.shape, const2),
                  pl.BlockSpec((None, 8, d), cls),
                  pl.BlockSpec((1, d), const2),
                  pl.BlockSpec((d, LANES), const2),
                  pl.BlockSpec((N_EXPERTS, 1), const2)],
        out_specs=(pl.BlockSpec((tm, d), row), pl.BlockSpec((tm, d), row),
                   pl.BlockSpec((N_EXPERTS, tm), lambda i: (0, i)), pl.BlockSpec((1, tm), lambda i: (0, i))),
        out_shape=(jax.ShapeDtypeStruct((n_tiles * tm, d), F32), jax.ShapeDtypeStruct((n_tiles * tm, d), BF16),
                   jax.ShapeDtypeStruct((N_EXPERTS, n_tiles * tm), F32),
                   jax.ShapeDtypeStruct((1, n_tiles * tm), jnp.int32)),
        compiler_params=_cparams(("arbitrary",)),
        name="outproj",
    )(x_lat, x_ctx, conv, of, ob, att, att_c, w_ext, mod, nw, rw3, rb)


MOE_DENSE_ROWS = 256


def _group_mlp(xb, gate4, wgu_ref, wd_ref):
    parts = []
    for k in range(EXPERTS_PER_GROUP):
        gu = jnp.dot(xb, wgu_ref[k], preferred_element_type=F32)
        parts.append((_silu(gu[:, :D_EXPERT]) * gu[:, D_EXPERT:] * gate4[:, k:k + 1]).astype(BF16))
    return jnp.dot(jnp.concatenate(parts, axis=1), wd_ref[...], preferred_element_type=F32)


def _moe_kernel(h_ref, gate8_ref, grp_ref, x1_ref, g2_ref, tri_ref, wgu_ref, wd_ref, *rest, caps, final_norm):
    rest = list(rest)
    fw_ref = rest.pop(0) if final_norm else None
    o_ref, cnt_scr = rest
    tm = h_ref.shape[0]
    grp = grp_ref[...]
    o_ref[...] = jnp.zeros_like(o_ref)
    rows8 = lax.broadcasted_iota(jnp.int32, (SUBLANES, tm), 0)
    member8 = jnp.where(rows8 == grp, 1.0, 0.0).astype(BF16)
    cnt_scr[...] = jnp.dot(member8, tri_ref[...], preferred_element_type=F32)

    def gates_of(g8):
        return g8[:, :EXPERTS_PER_GROUP] + g8[:, EXPERTS_PER_GROUP:]

    def group(g, carry):
        incl = cnt_scr[pl.ds(g, 1), :]
        count = jnp.max(incl)
        gate8_g, wgu_g, wd_g = gate8_ref.at[g], wgu_ref.at[g], wd_ref.at[g]

        def compacted(cap):
            pos = jnp.where(grp == g, incl.astype(jnp.int32) - 1, -1)
            slot = lax.broadcasted_iota(jnp.int32, (cap, tm), 0)
            onehot = jnp.where(slot == pos, 1.0, 0.0).astype(BF16)
            xg = jnp.dot(onehot, h_ref[...], preferred_element_type=F32).astype(BF16)
            g8 = jnp.dot(onehot, gate8_g[...], preferred_element_type=F32)
            y = _group_mlp(xg, gates_of(g8), wgu_g, wd_g)
            o_ref[...] += lax.dot_general(onehot, y.astype(BF16), (((0,), (0,)), ((), ())),
                                          preferred_element_type=F32)

        lower = 0
        for cap in caps:
            pl.when((count > lower) & (count <= cap))(functools.partial(compacted, cap))
            lower = cap

        @pl.when(count > caps[-1])
        def _():
            def chunk(ci, inner):
                rows = pl.ds(pl.multiple_of(ci * MOE_DENSE_ROWS, MOE_DENSE_ROWS), MOE_DENSE_ROWS)
                o_ref[rows, :] += _group_mlp(h_ref[rows, :], gates_of(gate8_g[rows, :].astype(F32)), wgu_g, wd_g)
                return inner

            lax.fori_loop(0, tm // MOE_DENSE_ROWS, chunk, 0)

        return carry

    lax.fori_loop(0, N_GROUPS, group, 0)
    x2 = x1_ref[...] + g2_ref[...] * o_ref[...]
    if final_norm:
        x2 = x2 * lax.rsqrt(jnp.mean(x2 * x2, axis=-1, keepdims=True) + EPS) * fw_ref[...]
    o_ref[...] = x2


def _moe_call(h2, gate8, grp, x1, g2, wgu, wd, tm, block0, n_tiles, final_w=None):
    out_rows, d = x1.shape
    caps = (tm // 4, 3 * tm // 8, tm // 2)
    assert all(cap % 16 == 0 for cap in caps) and tm % MOE_DENSE_ROWS == 0
    tri = jnp.asarray(np.triu(np.ones((tm, tm), np.float32)), dtype=BF16)
    rows = lambda i: (block0 + i, 0)
    const2 = lambda i: (0, 0)
    const3 = lambda i: (0, 0, 0)
    once = pl.Buffered(1)
    in_specs = [pl.BlockSpec((tm, d), rows),
                pl.BlockSpec((N_GROUPS, tm, 2 * EXPERTS_PER_GROUP), lambda i: (0, block0 + i, 0)),
                pl.BlockSpec((1, tm), lambda i: (0, block0 + i)),
                pl.BlockSpec((tm, d), rows),
                pl.BlockSpec((1, d), const2),
                pl.BlockSpec((tm, tm), const2, pipeline_mode=once),
                pl.BlockSpec(wgu.shape, lambda i: (0, 0, 0, 0), pipeline_mode=once),
                pl.BlockSpec(wd.shape, const3, pipeline_mode=once)]
    args = [h2, gate8, grp, x1, g2, tri, wgu, wd]
    if final_w is not None:
        in_specs.append(pl.BlockSpec((1, d), const2))
        args.append(final_w)
    return pl.pallas_call(
        functools.partial(_moe_kernel, caps=caps, final_norm=final_w is not None),
        grid=(n_tiles,),
        in_specs=in_specs,
        out_specs=pl.BlockSpec((tm, d), rows),
        out_shape=jax.ShapeDtypeStruct((out_rows, d), F32),
        scratch_shapes=[pltpu.VMEM((SUBLANES, tm), F32)],
        input_output_aliases={3: 0},
        compiler_params=_cparams(("arbitrary",)),
        name="experts",
    )(*args)


_SWAP32 = np.concatenate([np.arange(8, 16), np.arange(0, 8), np.arange(24, 32), np.arange(16, 24)])


def _pad_cols(w, width):
    return jnp.pad(w, ((0, 0), (0, width - w.shape[1])))


def _in_weight(w_in):
    sizes = (2 * CONV_W, RET_QK_W, RET_QK_W, RET_W, RET_W, RET_W, MLA_Q_RANK, MLA_KV_RANK, MLA_D_ROPE)
    offs = np.concatenate([[0], np.cumsum(sizes)])
    conv, rq, rk, rv, gf, gb, cq, ckv, kr = [w_in[:, offs[i]:offs[i + 1]] for i in range(len(sizes))]
    swap192 = np.concatenate([h * ROPE_DIM + _SWAP32 for h in range(RET_HEADS)])
    d = w_in.shape[0]

    def place_rope(w):
        return jnp.concatenate([jnp.zeros((d, MLA_D_NOPE), w.dtype), w,
                                jnp.zeros((d, LANES - MLA_D_NOPE - MLA_D_ROPE), w.dtype)], axis=1)

    ext = jnp.concatenate([
        conv, _pad_cols(rq, 256), _pad_cols(rq[:, swap192], 256), _pad_cols(rk, 256), _pad_cols(rk[:, swap192], 256),
        rv, gf, gb, _pad_cols(cq, 256), ckv, place_rope(kr), place_rope(kr[:, _SWAP32])], axis=1)
    assert ext.shape[1] == IN_EXT
    return ext.astype(BF16)


def _uq_weight(w_uq):
    r = w_uq.shape[0]
    w = w_uq.reshape(r, MLA_HEADS, MLA_D_NOPE + MLA_D_ROPE)
    nope, rope = w[..., :MLA_D_NOPE], w[..., MLA_D_NOPE:]
    zpad = jnp.zeros((r, MLA_HEADS, HEAD_PAD - MLA_D_NOPE - MLA_D_ROPE), w.dtype)
    main = jnp.concatenate([nope, rope, zpad], axis=-1).reshape(r, ATT_W)
    part = jnp.concatenate([jnp.zeros_like(nope), rope[..., _SWAP32], zpad], axis=-1).reshape(r, ATT_W)
    return jnp.concatenate([main, part], axis=1).astype(BF16)


def _ukv_weight(w_ukv):
    r = w_ukv.shape[0]
    w = w_ukv.reshape(r, MLA_HEADS, MLA_D_NOPE + MLA_D_V)
    zpad = jnp.zeros((r, MLA_HEADS, HEAD_PAD - MLA_D_NOPE), w.dtype)
    kpart = jnp.concatenate([w[..., :MLA_D_NOPE], zpad], axis=-1).reshape(r, ATT_W)
    vpart = jnp.concatenate([w[..., MLA_D_NOPE:], zpad], axis=-1).reshape(r, ATT_W)
    return jnp.concatenate([kpart, vpart], axis=1).astype(BF16)


def _out_weight(w_out):
    d = w_out.shape[1]
    conv, ret = w_out[:CONV_W], w_out[CONV_W:CONV_W + RET_W]
    att = w_out[CONV_W + RET_W:].reshape(MLA_HEADS, MLA_D_V, d)
    att = jnp.pad(att, ((0, 0), (0, HEAD_PAD - MLA_D_V), (0, 0))).reshape(ATT_W, d)
    return jnp.concatenate([conv, ret, att], axis=0).astype(BF16)


def _rope_tables(n_ctx, seq):
    f32 = np.float32
    inv = (f32(ROPE_BASE) ** (-np.arange(ROPE_PAIRS, dtype=f32) / f32(ROPE_PAIRS))).astype(f32)
    ar = (np.arange(seq // GRID_W, dtype=f32)[:, None] * inv).astype(f32)
    ac = (np.arange(GRID_W, dtype=f32)[:, None] * inv).astype(f32)
    row_cs = np.concatenate([np.cos(ar), np.cos(ar)], axis=1).astype(f32)
    row_sn = np.concatenate([-np.sin(ar), np.sin(ar)], axis=1).astype(f32)
    col_cs = np.concatenate([np.cos(ac), np.cos(ac)], axis=1).astype(f32)
    col_sn = np.concatenate([-np.sin(ac), np.sin(ac)], axis=1).astype(f32)

    def expand(row_t, col_t, ctx_value):
        rows = seq // GRID_W
        lat = jnp.concatenate([jnp.broadcast_to(jnp.asarray(row_t)[:, None, :], (rows, GRID_W, 16)),
                               jnp.broadcast_to(jnp.asarray(col_t)[None, :, :], (rows, GRID_W, 16))],
                              axis=-1).reshape(seq, ROPE_DIM)
        full = jnp.concatenate([lat, jnp.full((n_ctx, ROPE_DIM), ctx_value, F32)], axis=0)
        return jnp.tile(full, (1, LANES // ROPE_DIM))

    return expand(row_cs, col_cs, 1.0), expand(row_sn, col_sn, 0.0)


def _router_weight(router_w):
    hi = router_w.astype(BF16)
    lo = (router_w - hi.astype(F32)).astype(BF16)
    return jnp.pad(jnp.concatenate([hi, lo], axis=1), ((0, 0), (0, LANES - 2 * N_EXPERTS)))


def _group_gates(gates_t):
    t = gates_t.shape[1]
    g4 = jnp.transpose(gates_t.reshape(N_GROUPS, EXPERTS_PER_GROUP, t), (0, 2, 1))
    hi = g4.astype(BF16)
    lo = (g4 - hi.astype(F32)).astype(BF16)
    return jnp.concatenate([hi, lo], axis=-1)


def _group_weights(w_gate, w_up, w_down):
    e, d, f = w_gate.shape
    gu = jnp.concatenate([w_gate, w_up], axis=-1).astype(BF16)
    gu = gu.reshape(N_GROUPS, EXPERTS_PER_GROUP, d, 2 * f)
    wd = w_down.astype(BF16).reshape(N_GROUPS, EXPERTS_PER_GROUP * f, d)
    return gu, wd


def kernel(x, c, ctx, c_ctx, ada_w, ada_b, norm1_w, norm2_w, w_in, conv_dw, conv_b, conv_ln_w, conv_ln_b,
           mla_q_norm_w, mla_w_uq, mla_kv_norm_w, mla_w_ukv, w_out, router_w, router_bias,
           moe_w_gate, moe_w_up, moe_w_down, final_norm_w):
    batch, seq, d = x.shape
    n_ctx = ctx.shape[1]
    depth = ada_w.shape[0]
    t_all = seq + n_ctx
    assert batch == 1 and d == D_MODEL
    assert n_ctx % ROW_TILE == 0 and seq % ROW_TILE == 0 and n_ctx <= MOE_TILE and seq % MOE_TILE == 0
    assert seq % ATT_TQ == 0 and t_all % ATT_TK == 0 and n_ctx % RET_CHUNK == 0 and seq % n_ctx == 0
    n_lat_tiles = seq // ROW_TILE
    n_all_tiles = t_all // ROW_TILE

    cond = jnp.zeros((8, d), F32).at[0].set(c_ctx).at[1].set(c[0])
    mod = _ada_call(cond, ada_w, ada_b)[:, :2, :].reshape(depth, 2, 6, d)
    mod8 = jnp.pad(mod, ((0, 0), (0, 0), (0, 2), (0, 0)))
    cs, sn = _rope_tables(n_ctx, seq)
    ret_tabs = _ret_tables()
    rw3 = _router_weight(router_w)
    rb = router_bias.reshape(N_EXPERTS, 1)

    x_lat, x_ctx, ctx_block0 = x[0], ctx[0], 0
    for l in range(depth):
        last = l == depth - 1
        u, rq, rk, rv, gf, gb, q_att, k_att, v_att = _inproj_call(
            x_lat, x_ctx, ctx_block0, t_all, norm1_w[l][None, :], mod8[l][:, :2, :], _in_weight(w_in[l]), cs, sn,
            mla_q_norm_w[l][None, :], _uq_weight(mla_w_uq[l]), mla_kv_norm_w[l][None, :], _ukv_weight(mla_w_ukv[l]),
            n_lat_tiles)
        conv = _conv_call(u, conv_dw[l], conv_b[l][None, :], conv_ln_w[l][None, :], conv_ln_b[l][None, :],
                          n_lat_tiles)
        of, ob = _ret_call(rq, rk, rv, gf, gb, ret_tabs, seq // RET_CHUNK)
        att = _attn_call(q_att, k_att, v_att, ATT_TQ, ATT_TK, seq // ATT_TQ)
        att_c = att
        if not last:
            tqc = min(ATT_TQ, n_ctx)
            att_c = _attn_call(q_att[:, seq:], k_att[:, seq:], v_att[:, seq:], tqc, n_ctx, n_ctx // tqc)
        n_tiles = n_lat_tiles if last else n_all_tiles
        x1, h2, gates_t, grp = _outproj_call(x_lat, x_ctx, ctx_block0, conv, of, ob, att, att_c,
                                             _out_weight(w_out[l]), mod8[l], norm2_w[l][None, :], rw3, rb,
                                             n_lat_tiles, n_tiles)
        gate8 = _group_gates(gates_t)
        wgu, wd = _group_weights(moe_w_gate[l], moe_w_up[l], moe_w_down[l])
        x_all = _moe_call(h2, gate8, grp, x1, mod[l, 1, 5][None, :], wgu, wd, MOE_TILE, 0, seq // MOE_TILE,
                          final_w=final_norm_w[None, :] if last else None)
        if not last:
            x_all = _moe_call(h2, gate8, grp, x_all, mod[l, 0, 5][None, :], wgu, wd, n_ctx, seq // n_ctx, 1)
        x_lat, x_ctx, ctx_block0 = x_all, x_all, n_lat_tiles
    return x_all[None]
```

```python
import functools
import math

import numpy as np
import jax
import jax.numpy as jnp
from jax import lax
from jax.experimental import pallas as pl
from jax.experimental.pallas import tpu as pltpu

F32 = jnp.float32
BF16 = jnp.bfloat16

D_MODEL = 1024
GRID_W = 64
CONV_W = 256
CONV_K = 31
RET_HEADS = 6
RET_DK = 32
RET_DV = 64
RET_QK_W = RET_HEADS * RET_DK
RET_W = RET_HEADS * RET_DV
RET_CHUNK = 128
RET_GN_EPS = 1e-5
MLA_HEADS = 6
MLA_Q_RANK = 192
MLA_KV_RANK = 128
MLA_D_NOPE = 64
MLA_D_ROPE = 32
MLA_D_V = 64
MLA_SCALE = (MLA_D_NOPE + MLA_D_ROPE) ** -0.5
ROPE_DIM = 32
ROPE_PAIRS = ROPE_DIM // 4
ROPE_BASE = 10000.0
N_EXPERTS = 16
N_GROUPS = 4
EXPERTS_PER_GROUP = N_EXPERTS // N_GROUPS
D_EXPERT = 256
EPS = 1e-6

LANES = 128
HEAD_PAD = LANES
ATT_W = MLA_HEADS * HEAD_PAD
LOG2E = math.log2(math.e)

C_CONV = 0
C_RQ = 512
C_RQS = 768
C_RK = 1024
C_RKS = 1280
C_RV = 1536
C_RGF = 1920
C_RGB = 2304
C_CQ = 2688
C_CKV = 2944
C_KR = 3072
C_KRS = 3200
IN_EXT = 3328

ROW_TILE = 256
MOE_TILE = 512
ATT_TQ = 1024
ATT_TK = 1280
VMEM_LIMIT = 48 * 1024 * 1024


def _cparams(sem):
    return pltpu.CompilerParams(dimension_semantics=sem, vmem_limit_bytes=VMEM_LIMIT)


def _silu(x):
    return x * jax.nn.sigmoid(x)


def _ada_kernel(c_ref, w_ref, b_ref, o_ref):
    s = _silu(c_ref[...])
    o_ref[...] = jnp.dot(s, w_ref[...], preferred_element_type=F32,
                         precision=lax.Precision.HIGHEST) + b_ref[...]


def _ada_call(cond, ada_w, ada_b):
    depth, d, n = ada_w.shape
    tn = 1536
    return pl.pallas_call(
        _ada_kernel,
        grid=(depth, n // tn),
        in_specs=[pl.BlockSpec((8, d), lambda l, j: (0, 0)),
                  pl.BlockSpec((None, d, tn), lambda l, j: (l, 0, j)),
                  pl.BlockSpec((None, 1, tn), lambda l, j: (l, 0, j))],
        out_specs=pl.BlockSpec((None, 8, tn), lambda l, j: (l, 0, j)),
        out_shape=jax.ShapeDtypeStruct((depth, 8, n), F32),
        compiler_params=_cparams(("arbitrary", "arbitrary")),
        name="adaln",
    )(cond, ada_w, ada_b.reshape(depth, 1, n))


def _inproj_kernel(x_ref, xc_ref, nw_ref, mod_ref, w_ref, cs_ref, sn_ref, qnw_ref, wuq_ref, kvnw_ref, wukv_ref,
                   u_ref, rq_ref, rk_ref, rv_ref, gf_ref, gb_ref, q_ref, k_ref, v_ref, *, n_lat_tiles):
    x = jnp.where(pl.program_id(0) < n_lat_tiles, x_ref[...], xc_ref[...])
    y = x * lax.rsqrt(jnp.mean(x * x, axis=-1, keepdims=True) + EPS) * nw_ref[...]
    h = (y * (1.0 + mod_ref[1:2, :]) + mod_ref[0:1, :]).astype(BF16)

    def proj(c0, width):
        return jnp.dot(h, w_ref[:, c0:c0 + width], preferred_element_type=F32)

    cs = cs_ref[...]
    sn = sn_ref[...]
    lane = lax.broadcasted_iota(jnp.int32, (1, LANES), 1)
    rope_lanes = (lane >= MLA_D_NOPE) & (lane < MLA_D_NOPE + MLA_D_ROPE)
    csq = jnp.where(rope_lanes, cs, 1.0)
    snq = jnp.where(rope_lanes, sn, 0.0)

    ag = proj(C_CONV, 2 * CONV_W)
    u_ref[...] = ag[:, :CONV_W] * jax.nn.sigmoid(ag[:, CONV_W:])

    cs192 = jnp.concatenate([cs, cs[:, :RET_QK_W - LANES]], axis=1)
    sn192 = jnp.concatenate([sn, sn[:, :RET_QK_W - LANES]], axis=1)
    rq = proj(C_RQ, 256)[:, :RET_QK_W] * cs192 + proj(C_RQS, 256)[:, :RET_QK_W] * sn192
    rq_ref[...] = rq.astype(BF16)
    rk = proj(C_RK, 256)[:, :RET_QK_W] * cs192 + proj(C_RKS, 256)[:, :RET_QK_W] * sn192
    rk_ref[...] = (rk * (RET_DK ** -0.5)).astype(BF16)
    rv_ref[...] = proj(C_RV, RET_W).astype(BF16)
    gf_ref[...] = _silu(proj(C_RGF, RET_W))
    gb_ref[...] = _silu(proj(C_RGB, RET_W))

    cq = proj(C_CQ, 256)[:, :MLA_Q_RANK]
    cqn = (cq * lax.rsqrt(jnp.mean(cq * cq, axis=-1, keepdims=True) + EPS) * qnw_ref[...]).astype(BF16)
    qa = jnp.dot(cqn, wuq_ref[:, :ATT_W], preferred_element_type=F32)
    qb = jnp.dot(cqn, wuq_ref[:, ATT_W:], preferred_element_type=F32)
    ckv = proj(C_CKV, MLA_KV_RANK)
    ckvn = (ckv * lax.rsqrt(jnp.mean(ckv * ckv, axis=-1, keepdims=True) + EPS) * kvnw_ref[...]).astype(BF16)
    ka = jnp.dot(ckvn, wukv_ref[:, :ATT_W], preferred_element_type=F32)
    va = jnp.dot(ckvn, wukv_ref[:, ATT_W:], preferred_element_type=F32)
    kr = proj(C_KR, LANES) * csq + proj(C_KRS, LANES) * snq
    ones_lane = (lane == MLA_D_V).astype(F32)
    for hd in range(MLA_HEADS):
        sl = slice(hd * HEAD_PAD, (hd + 1) * HEAD_PAD)
        q_ref[hd] = ((qa[:, sl] * csq + qb[:, sl] * snq) * (MLA_SCALE * LOG2E)).astype(BF16)
        k_ref[hd] = (ka[:, sl] + kr).astype(BF16)
        v_ref[hd] = (va[:, sl] + ones_lane).astype(BF16)


def _inproj_call(x_lat, x_ctx, ctx_block0, t_all, nw, mod, w_ext, cs, sn, qnw, wuq, kvnw, wukv, n_lat_tiles):
    d = x_lat.shape[1]
    tm = ROW_TILE
    row = lambda i: (i, 0)
    const2 = lambda i: (0, 0)
    head = lambda i: (0, i, 0)
    cls = lambda i: (jnp.where(i < n_lat_tiles, 1, 0), 0, 0)
    out_shape = (
        jax.ShapeDtypeStruct((t_all, CONV_W), F32),
        jax.ShapeDtypeStruct((t_all, RET_QK_W), BF16),
        jax.ShapeDtypeStruct((t_all, RET_QK_W), BF16),
        jax.ShapeDtypeStruct((t_all, RET_W), BF16),
        jax.ShapeDtypeStruct((t_all, RET_W), F32),
        jax.ShapeDtypeStruct((t_all, RET_W), F32),
        jax.ShapeDtypeStruct((MLA_HEADS, t_all, HEAD_PAD), BF16),
        jax.ShapeDtypeStruct((MLA_HEADS, t_all, HEAD_PAD), BF16),
        jax.ShapeDtypeStruct((MLA_HEADS, t_all, HEAD_PAD), BF16),
    )
    out_specs = (
        pl.BlockSpec((tm, CONV_W), row),
        pl.BlockSpec((tm, RET_QK_W), row),
        pl.BlockSpec((tm, RET_QK_W), row),
        pl.BlockSpec((tm, RET_W), row),
        pl.BlockSpec((tm, RET_W), row),
        pl.BlockSpec((tm, RET_W), row),
        pl.BlockSpec((MLA_HEADS, tm, HEAD_PAD), head),
        pl.BlockSpec((MLA_HEADS, tm, HEAD_PAD), head),
        pl.BlockSpec((MLA_HEADS, tm, HEAD_PAD), head),
    )
    return pl.pallas_call(
        functools.partial(_inproj_kernel, n_lat_tiles=n_lat_tiles),
        grid=(t_all // tm,),
        in_specs=[pl.BlockSpec((tm, d), lambda i: (jnp.minimum(i, n_lat_tiles - 1), 0)),
                  pl.BlockSpec((tm, d), lambda i: (ctx_block0 + jnp.maximum(i - n_lat_tiles, 0), 0)),
                  pl.BlockSpec((1, d), const2),
                  pl.BlockSpec((None, 2, d), cls),
                  pl.BlockSpec((d, IN_EXT), const2),
                  pl.BlockSpec((tm, LANES), row),
                  pl.BlockSpec((tm, LANES), row),
                  pl.BlockSpec((1, MLA_Q_RANK), const2),
                  pl.BlockSpec((MLA_Q_RANK, 2 * ATT_W), const2),
                  pl.BlockSpec((1, MLA_KV_RANK), const2),
                  pl.BlockSpec((MLA_KV_RANK, 2 * ATT_W), const2)],
        out_specs=out_specs,
        out_shape=out_shape,
        compiler_params=_cparams(("arbitrary",)),
        name="inproj",
    )(x_lat, x_ctx, nw, mod, w_ext, cs, sn, qnw, wuq, kvnw, wukv)


CONV_HALO = 16
CONV_SUB = 64
SUBLANES = 8


def _conv_kernel(prev_ref, cur_ref, next_ref, dw_ref, b_ref, lnw_ref, lnb_ref, o_ref, ext_ref, sh_ref, *,
                 n_lat_tiles):
    i = pl.program_id(0)
    n = pl.num_programs(0)
    tm = cur_ref.shape[0]
    seq_start = (i == 0) | (i == n_lat_tiles)
    seq_end = (i == n_lat_tiles - 1) | (i == n - 1)
    ext_ref[0:CONV_HALO, :] = jnp.where(seq_start, 0.0, prev_ref[...])
    ext_ref[CONV_HALO:CONV_HALO + tm, :] = cur_ref[...]
    ext_ref[CONV_HALO + tm:, :] = jnp.where(seq_end, 0.0, next_ref[...])
    span = sh_ref.shape[1]
    for ph in range(SUBLANES):
        sh_ref[ph] = ext_ref[pl.ds(ph, span), :]
    base = CONV_HALO - CONV_K // 2
    for r in range(tm // CONV_SUB):
        acc = jnp.zeros((CONV_SUB, CONV_W), F32) + b_ref[...]
        for k in range(CONV_K):
            off = base + k
            acc = acc + (sh_ref[off % SUBLANES, pl.ds(r * CONV_SUB + off - off % SUBLANES, CONV_SUB), :]
                         * dw_ref[k:k + 1, :])
        mu = jnp.mean(acc, axis=-1, keepdims=True)
        dlt = acc - mu
        var = jnp.mean(dlt * dlt, axis=-1, keepdims=True)
        y = dlt * lax.rsqrt(var + EPS) * lnw_ref[...] + lnb_ref[...]
        o_ref[r * CONV_SUB:(r + 1) * CONV_SUB, :] = _silu(y).astype(BF16)


def _conv_call(u, dw, b, lnw, lnb, n_lat_tiles):
    t_all = u.shape[0]
    tm = ROW_TILE
    hpt = tm // CONV_HALO
    n_halo = t_all // CONV_HALO
    const2 = lambda i: (0, 0)
    return pl.pallas_call(
        functools.partial(_conv_kernel, n_lat_tiles=n_lat_tiles),
        grid=(t_all // tm,),
        in_specs=[pl.BlockSpec((CONV_HALO, CONV_W), lambda i: (jnp.maximum(i * hpt - 1, 0), 0)),
                  pl.BlockSpec((tm, CONV_W), lambda i: (i, 0)),
                  pl.BlockSpec((CONV_HALO, CONV_W), lambda i: (jnp.minimum((i + 1) * hpt, n_halo - 1), 0)),
                  pl.BlockSpec((CONV_K, CONV_W), const2),
                  pl.BlockSpec((1, CONV_W), const2),
                  pl.BlockSpec((1, CONV_W), const2),
                  pl.BlockSpec((1, CONV_W), const2)],
        out_specs=pl.BlockSpec((tm, CONV_W), lambda i: (i, 0)),
        out_shape=jax.ShapeDtypeStruct((t_all, CONV_W), BF16),
        scratch_shapes=[pltpu.VMEM((tm + 2 * CONV_HALO, CONV_W), F32),
                        pltpu.VMEM((SUBLANES, tm + 2 * CONV_HALO - SUBLANES, CONV_W), F32)],
        compiler_params=_cparams(("arbitrary",)),
        name="conv",
    )(u, u, u, dw, b, lnw, lnb)


def _split_dot(x, a):
    hi = x.astype(BF16)
    lo = (x - hi.astype(F32)).astype(BF16)
    return (jnp.dot(hi, a, preferred_element_type=F32) + jnp.dot(lo, a, preferred_element_type=F32))


def _ret_direction(q, k, v, gate, r_ref, dmask, xi, zeta, gchunk, bdmask, avg, o_ref):
    lane_k = lax.broadcasted_iota(jnp.int32, (1, RET_QK_W), 1) // RET_DK
    lane_v = lax.broadcasted_iota(jnp.int32, (1, RET_W), 1) // RET_DV
    zero_k = jnp.zeros_like(k)
    zero_v = jnp.zeros_like(v)
    k_bd = jnp.concatenate([jnp.where(lane_k == hd, k, zero_k) for hd in range(RET_HEADS)], axis=0)
    v_bd = jnp.concatenate([jnp.where(lane_v == hd, v, zero_v) for hd in range(RET_HEADS)], axis=0)
    s = lax.dot_general(q, k_bd, (((1,), (1,)), ((), ())), preferred_element_type=F32)
    s = (s * dmask).astype(BF16)
    inner = jnp.dot(s, v_bd, preferred_element_type=F32)
    r = r_ref[...]
    cross = jnp.dot(q, r.astype(BF16), preferred_element_type=F32) * xi
    o = inner + cross
    kz = (k.astype(F32) * zeta).astype(BF16)
    ds = lax.dot_general(kz, v, (((0,), (0,)), ((), ())), preferred_element_type=F32)
    r_ref[...] = gchunk * r + ds * bdmask
    mu = jnp.dot(o.astype(BF16), avg, preferred_element_type=F32)
    dlt = o - mu
    var = jnp.dot((dlt * dlt).astype(BF16), avg, preferred_element_type=F32)
    o_ref[...] = (gate * (dlt * lax.rsqrt(var + RET_GN_EPS))).astype(BF16)


def _ret_kernel(qf_ref, kf_ref, vf_ref, gf_ref, qb_ref, kb_ref, vb_ref, gb_ref,
                dmf_ref, dmb_ref, xif_ref, xib_ref, ztf_ref, ztb_ref, gcf_ref, gcb_ref, bdm_ref, avg_ref,
                of_ref, ob_ref, rf_ref, rb_ref):
    @pl.when(pl.program_id(0) == 0)
    def _():
        rf_ref[...] = jnp.zeros_like(rf_ref)
        rb_ref[...] = jnp.zeros_like(rb_ref)

    bdm = bdm_ref[...]
    avg = avg_ref[...]
    _ret_direction(qf_ref[...], kf_ref[...], vf_ref[...], gf_ref[...], rf_ref, dmf_ref[...], xif_ref[...],
                   ztf_ref[...], gcf_ref[...], bdm, avg, of_ref)
    _ret_direction(qb_ref[...], kb_ref[...], vb_ref[...], gb_ref[...], rb_ref, dmb_ref[...], xib_ref[...],
                   ztb_ref[...], gcb_ref[...], bdm, avg, ob_ref)


def _ret_tables():
    c = RET_CHUNK
    f32 = np.float32
    gamma_f = (1.0 - 2.0 ** (-5.0 - np.arange(RET_HEADS, dtype=f32))).astype(f32)
    gamma_b = gamma_f[::-1]
    idx = np.arange(c, dtype=f32)
    diff = idx[:, None] - idx[None, :]

    def tables(gamma, reverse):
        lg = np.log(gamma).astype(f32)
        d = -diff if reverse else diff
        dm = np.where(d[None] >= 0, np.exp(np.maximum(d, 0.0)[None] * lg[:, None, None]), 0.0)
        dm = np.transpose(dm, (1, 0, 2)).reshape(c, RET_HEADS * c)
        xi_e = (c - idx) if reverse else (idx + 1.0)
        zt_e = idx if reverse else (c - 1.0 - idx)
        xi = np.repeat(np.exp(xi_e[:, None] * lg[None, :]), RET_DV, axis=1)
        zt = np.repeat(np.exp(zt_e[:, None] * lg[None, :]), RET_DK, axis=1)
        gc = np.repeat(np.exp(c * lg), RET_DV)[None, :]
        return [t.astype(f32) for t in (dm, xi, zt, gc)]

    dmf, xif, ztf, gcf = tables(gamma_f, False)
    dmb, xib, ztb, gcb = tables(gamma_b, True)
    hk = np.arange(RET_QK_W) // RET_DK
    hv = np.arange(RET_W) // RET_DV
    bdm = (hk[:, None] == hv[None, :]).astype(f32)
    avg = jnp.asarray((hv[:, None] == hv[None, :]).astype(f32) / RET_DV, dtype=BF16)
    return tuple(jnp.asarray(t) for t in (dmf, dmb, xif, xib, ztf, ztb, gcf, gcb, bdm)) + (avg,)


def _ret_call(rq, rk, rv, gf, gb, tabs, n_lat_chunks):
    t_all = rq.shape[0]
    c = RET_CHUNK
    n = t_all // c
    n_ctx_chunks = n - n_lat_chunks

    def fwd(i):
        return (jnp.where(i < n_ctx_chunks, n_lat_chunks + i, i - n_ctx_chunks), 0)

    def bwd(i):
        return (n - 1 - i, 0)

    const2 = lambda i: (0, 0)
    tab_specs = [pl.BlockSpec(t.shape, const2) for t in tabs]
    return pl.pallas_call(
        _ret_kernel,
        grid=(n,),
        in_specs=[pl.BlockSpec((c, RET_QK_W), fwd), pl.BlockSpec((c, RET_QK_W), fwd),
                  pl.BlockSpec((c, RET_W), fwd), pl.BlockSpec((c, RET_W), fwd),
                  pl.BlockSpec((c, RET_QK_W), bwd), pl.BlockSpec((c, RET_QK_W), bwd),
                  pl.BlockSpec((c, RET_W), bwd), pl.BlockSpec((c, RET_W), bwd)] + tab_specs,
        out_specs=(pl.BlockSpec((c, RET_W), fwd), pl.BlockSpec((c, RET_W), bwd)),
        out_shape=(jax.ShapeDtypeStruct((t_all, RET_W), BF16), jax.ShapeDtypeStruct((t_all, RET_W), BF16)),
        scratch_shapes=[pltpu.VMEM((RET_QK_W, RET_W), F32), pltpu.VMEM((RET_QK_W, RET_W), F32)],
        compiler_params=_cparams(("arbitrary",)),
        name="retention",
    )(rq, rk, rv, gf, rq, rk, rv, gb, *tabs)


ATT_SUB = 128
VT_ROWS = 80


def _attn_kernel(q_ref, k_ref, vt_ref, o_ref, s0_scr, s1_scr, p0_scr, p1_scr, acc_scr):
    q = q_ref[...]
    tq = q.shape[0]
    nkc, tk, _ = k_ref.shape
    nsub = tk // ATT_SUB

    s_bufs = (s0_scr, s1_scr)
    p_bufs = (p0_scr, p1_scr)

    def pv(c, par, alpha):
        part = jnp.dot(vt_ref[c], p_bufs[par][...], preferred_element_type=F32)
        acc_scr[...] = alpha * acc_scr[...] + part

    def step(c, par, m_old, m_blk, alpha_prev, with_scores, with_pv):
        if with_pv:
            pv(c - 1, 1 - par, alpha_prev)
        m_new = jnp.maximum(m_old, m_blk)
        alpha = jnp.exp2(m_old - m_new)
        mx = jnp.full((8, tq), -jnp.inf, F32)
        for j in range(nsub):
            rows = pl.ds(j * ATT_SUB, ATT_SUB)
            p_bufs[par][rows, :] = jnp.exp2(s_bufs[par][rows, :] - m_new).astype(BF16)
            if with_scores:
                mx = jnp.maximum(mx, score_rows(c + 1, 1 - par, rows))
        return m_new, jnp.max(mx, axis=0, keepdims=True), alpha

    def score_rows(c, par, rows):
        s = jnp.dot(k_ref[c, rows, :], qt, preferred_element_type=F32)
        s_bufs[par][rows, :] = s
        return jnp.max(s.reshape(ATT_SUB // 8, 8, tq), axis=0)

    def scores(c, par):
        mx = jnp.full((8, tq), -jnp.inf, F32)
        for j in range(nsub):
            mx = jnp.maximum(mx, score_rows(c, par, pl.ds(j * ATT_SUB, ATT_SUB)))
        return jnp.max(mx, axis=0, keepdims=True)

    acc_scr[...] = jnp.zeros_like(acc_scr)
    qt = q.astype(F32).T.astype(BF16)
    m = jnp.full((1, tq), -jnp.inf, F32)
    m_blk = scores(0, 0)
    alpha = jnp.ones((1, tq), F32)
    if nkc > 1:
        m, m_blk, alpha = step(0, 0, m, m_blk, alpha, True, False)
        def body(c, carry):
            return lax.cond(c % 2 == 1,
                            lambda cr: step(c, 1, *cr, True, True),
                            lambda cr: step(c, 0, *cr, True, True), carry)

        m, m_blk, alpha = lax.fori_loop(1, nkc - 1, body, (m, m_blk, alpha))
    last = nkc - 1
    m, _, alpha = step(last, last % 2, m, m_blk, alpha, False, nkc > 1)
    pv(last, last % 2, alpha)
    acc = acc_scr[...]
    out_t = acc / acc[MLA_D_V:MLA_D_V + 1, :]
    if VT_ROWS < HEAD_PAD:
        out_t = jnp.concatenate([out_t, jnp.zeros((HEAD_PAD - VT_ROWS, tq), F32)], axis=0)
    o_ref[...] = out_t.T.astype(BF16)


def _attn_call(q, k, v, tq, tk, n_q_blocks):
    nh = q.shape[0]
    t_k = k.shape[1]
    nkc = t_k // tk
    k4 = k.reshape(nh, nkc, tk, HEAD_PAD)
    vt4 = jnp.swapaxes(v[:, :, :VT_ROWS].reshape(nh, nkc, tk, VT_ROWS), 2, 3)
    return pl.pallas_call(
        _attn_kernel,
        grid=(nh, n_q_blocks),
        in_specs=[pl.BlockSpec((None, tq, HEAD_PAD), lambda h, j: (h, j, 0)),
                  pl.BlockSpec((None, nkc, tk, HEAD_PAD), lambda h, j: (h, 0, 0, 0)),
                  pl.BlockSpec((None, nkc, VT_ROWS, tk), lambda h, j: (h, 0, 0, 0))],
        out_specs=pl.BlockSpec((tq, HEAD_PAD), lambda h, j: (j, h)),
        out_shape=jax.ShapeDtypeStruct((n_q_blocks * tq, nh * HEAD_PAD), BF16),
        scratch_shapes=[pltpu.VMEM((tk, tq), F32), pltpu.VMEM((tk, tq), F32),
                        pltpu.VMEM((tk, tq), BF16), pltpu.VMEM((tk, tq), BF16), pltpu.VMEM((VT_ROWS, tq), F32)],
        compiler_params=_cparams(("arbitrary", "arbitrary")),
        name="attention",
    )(q, k4, vt4)


def _top2_sum(a, b, c, d):
    hi1, lo1 = jnp.maximum(a, b), jnp.minimum(a, b)
    hi2, lo2 = jnp.maximum(c, d), jnp.minimum(c, d)
    return jnp.maximum(hi1, hi2) + jnp.maximum(jnp.minimum(hi1, hi2), jnp.maximum(lo1, lo2))


def _gates_t(aff, sel, gt_ref, grp_ref):
    rows = [sel[e:e + 1, :] for e in range(N_EXPERTS)]
    g_score = [_top2_sum(*rows[g * EXPERTS_PER_GROUP:(g + 1) * EXPERTS_PER_GROUP]) for g in range(N_GROUPS)]
    best = g_score[0]
    best_g = jnp.zeros_like(best, dtype=jnp.int32)
    for g in range(1, N_GROUPS):
        better = g_score[g] > best
        best = jnp.where(better, g_score[g], best)
        best_g = jnp.where(better, g, best_g)
    picked = []
    for e in range(N_EXPERTS):
        g = e // EXPERTS_PER_GROUP
        rank = jnp.zeros_like(best_g)
        for o in range(g * EXPERTS_PER_GROUP, (g + 1) * EXPERTS_PER_GROUP):
            if o == e:
                continue
            ahead = (rows[o] >= rows[e]) if o < e else (rows[o] > rows[e])
            rank = rank + jnp.where(ahead, 1, 0)
        picked.append(jnp.where(best_g == g, rank, 2) < 2)
    w = [jnp.where(picked[e], aff[e:e + 1, :], 0.0) for e in range(N_EXPERTS)]
    total = w[0]
    for e in range(1, N_EXPERTS):
        total = total + w[e]
    for e in range(N_EXPERTS):
        gt_ref[e:e + 1, :] = w[e] / total
    grp_ref[...] = best_g


OUT_HALF = D_MODEL // 2


def _outproj_kernel(x_ref, xc_ref, conv_ref, of_ref, ob_ref, att_ref, attc_ref, w_ref, mod_ref, nw_ref, rw_ref,
                    rb_ref, x1_ref, h2_ref, gt_ref, grp_ref, *, n_lat_tiles):
    is_lat = pl.program_id(0) < n_lat_tiles
    ret = (of_ref[...].astype(F32) + ob_ref[...].astype(F32)).astype(BF16)
    att = jnp.where(is_lat, att_ref[...], attc_ref[...])
    mix = jnp.concatenate([conv_ref[...], ret, att], axis=1)
    halves = []
    ssq = 0.0
    for hf in range(2):
        cols = slice(hf * OUT_HALF, (hf + 1) * OUT_HALF)
        o = jnp.dot(mix, w_ref[:, cols], preferred_element_type=F32)
        x1 = jnp.where(is_lat, x_ref[:, cols], xc_ref[:, cols]) + mod_ref[2:3, cols] * o
        x1_ref[:, cols] = x1
        ssq = ssq + jnp.sum(x1 * x1, axis=-1, keepdims=True)
        halves.append(x1)
    inv = lax.rsqrt(ssq * (1.0 / D_MODEL) + EPS)
    logit_parts = 0.0
    for hf in range(2):
        cols = slice(hf * OUT_HALF, (hf + 1) * OUT_HALF)
        h2 = (halves[hf] * inv * nw_ref[:, cols]) * (1.0 + mod_ref[4:5, cols]) + mod_ref[3:4, cols]
        hi = h2.astype(BF16)
        h2_ref[:, cols] = hi
        lo = (h2 - hi.astype(F32)).astype(BF16)
        logit_parts = (logit_parts + jnp.dot(hi, rw_ref[cols, :], preferred_element_type=F32)
                       + jnp.dot(lo, rw_ref[cols, :], preferred_element_type=F32))
    lt = logit_parts.T
    logits = lt[0:N_EXPERTS, :] + lt[N_EXPERTS:2 * N_EXPERTS, :]
    aff = jax.nn.sigmoid(logits)
    _gates_t(aff, aff + rb_ref[...], gt_ref, grp_ref)


def _outproj_call(x_lat, x_ctx, ctx_block0, conv, of, ob, att, att_c, w_ext, mod, nw, rw3, rb, n_lat_tiles,
                  n_tiles):
    d = x_lat.shape[1]
    tm = ROW_TILE
    row = lambda i: (i, 0)
    const2 = lambda i: (0, 0)
    lat = lambda i: (jnp.minimum(i, n_lat_tiles - 1), 0)
    cls = lambda i: (jnp.where(i < n_lat_tiles, 1, 0), 0, 0)
    return pl.pallas_call(
        functools.partial(_outproj_kernel, n_lat_tiles=n_lat_tiles),
        grid=(n_tiles,),
        in_specs=[pl.BlockSpec((tm, d), lat),
                  pl.BlockSpec((tm, d), lambda i: (ctx_block0 + jnp.maximum(i - n_lat_tiles, 0), 0)),
                  pl.BlockSpec((tm, CONV_W), row),
                  pl.BlockSpec((tm, RET_W), row),
                  pl.BlockSpec((tm, RET_W), row),
                  pl.BlockSpec((tm, ATT_W), lat),
                  pl.BlockSpec((tm, ATT_W), lambda i: (jnp.maximum(i - n_lat_tiles, 0), 0)),
                  pl.BlockSpec(w_ext.shape, const2),
                  pl.BlockSpec((None, 8, d), cls),
                  pl.BlockSpec((1, d), const2),
                  pl.BlockSpec((d, LANES), const2),
                  pl.BlockSpec((N_EXPERTS, 1), const2)],
        out_specs=(pl.BlockSpec((tm, d), row), pl.BlockSpec((tm, d), row),
                   pl.BlockSpec((N_EXPERTS, tm), lambda i: (0, i)), pl.BlockSpec((1, tm), lambda i: (0, i))),
        out_shape=(jax.ShapeDtypeStruct((n_tiles * tm, d), F32), jax.ShapeDtypeStruct((n_tiles * tm, d), BF16),
                   jax.ShapeDtypeStruct((N_EXPERTS, n_tiles * tm), F32),
                   jax.ShapeDtypeStruct((1, n_tiles * tm), jnp.int32)),
        compiler_params=_cparams(("arbitrary",)),
        name="outproj",
    )(x_lat, x_ctx, conv, of, ob, att, att_c, w_ext, mod, nw, rw3, rb)


MOE_DENSE_ROWS = 256


def _group_mlp(xb, gate4, wgu_ref, wd_ref):
    parts = []
    for k in range(EXPERTS_PER_GROUP):
        gu = jnp.dot(xb, wgu_ref[k], preferred_element_type=F32)
        parts.append((_silu(gu[:, :D_EXPERT]) * gu[:, D_EXPERT:] * gate4[:, k:k + 1]).astype(BF16))
    return jnp.dot(jnp.concatenate(parts, axis=1), wd_ref[...], preferred_element_type=F32)


def _moe_kernel(h_ref, gate8_ref, grp_ref, x1_ref, g2_ref, tri_ref, wgu_ref, wd_ref, *rest, caps, final_norm):
    rest = list(rest)
    fw_ref = rest.pop(0) if final_norm else None
    o_ref, cnt_scr = rest
    tm = h_ref.shape[0]
    grp = grp_ref[...]
    o_ref[...] = jnp.zeros_like(o_ref)
    rows8 = lax.broadcasted_iota(jnp.int32, (SUBLANES, tm), 0)
    member8 = jnp.where(rows8 == grp, 1.0, 0.0).astype(BF16)
    cnt_scr[...] = jnp.dot(member8, tri_ref[...], preferred_element_type=F32)

    def gates_of(g8):
        return g8[:, :EXPERTS_PER_GROUP] + g8[:, EXPERTS_PER_GROUP:]

    def group(g, carry):
        incl = cnt_scr[pl.ds(g, 1), :]
        count = jnp.max(incl)
        gate8_g, wgu_g, wd_g = gate8_ref.at[g], wgu_ref.at[g], wd_ref.at[g]

        def compacted(cap):
            pos = jnp.where(grp == g, incl.astype(jnp.int32) - 1, -1)
            slot = lax.broadcasted_iota(jnp.int32, (cap, tm), 0)
            onehot = jnp.where(slot == pos, 1.0, 0.0).astype(BF16)
            xg = jnp.dot(onehot, h_ref[...], preferred_element_type=F32).astype(BF16)
            g8 = jnp.dot(onehot, gate8_g[...], preferred_element_type=F32)
            y = _group_mlp(xg, gates_of(g8), wgu_g, wd_g)
            o_ref[...] += lax.dot_general(onehot, y.astype(BF16), (((0,), (0,)), ((), ())),
                                          preferred_element_type=F32)

        lower = 0
        for cap in caps:
            pl.when((count > lower) & (count <= cap))(functools.partial(compacted, cap))
            lower = cap

        @pl.when(count > caps[-1])
        def _():
            def chunk(ci, inner):
                rows = pl.ds(pl.multiple_of(ci * MOE_DENSE_ROWS, MOE_DENSE_ROWS), MOE_DENSE_ROWS)
                o_ref[rows, :] += _group_mlp(h_ref[rows, :], gates_of(gate8_g[rows, :].astype(F32)), wgu_g, wd_g)
                return inner

            lax.fori_loop(0, tm // MOE_DENSE_ROWS, chunk, 0)

        return carry

    lax.fori_loop(0, N_GROUPS, group, 0)
    x2 = x1_ref[...] + g2_ref[...] * o_ref[...]
    if final_norm:
        x2 = x2 * lax.rsqrt(jnp.mean(x2 * x2, axis=-1, keepdims=True) + EPS) * fw_ref[...]
    o_ref[...] = x2


def _moe_call(h2, gate8, grp, x1, g2, wgu, wd, tm, block0, n_tiles, final_w=None):
    out_rows, d = x1.shape
    caps = (tm // 4, 3 * tm // 8, tm // 2)
    assert all(cap % 16 == 0 for cap in caps) and tm % MOE_DENSE_ROWS == 0
    tri = jnp.asarray(np.triu(np.ones((tm, tm), np.float32)), dtype=BF16)
    rows = lambda i: (block0 + i, 0)
    const2 = lambda i: (0, 0)
    const3 = lambda i: (0, 0, 0)
    once = pl.Buffered(1)
    in_specs = [pl.BlockSpec((tm, d), rows),
                pl.BlockSpec((N_GROUPS, tm, 2 * EXPERTS_PER_GROUP), lambda i: (0, block0 + i, 0)),
                pl.BlockSpec((1, tm), lambda i: (0, block0 + i)),
                pl.BlockSpec((tm, d), rows),
                pl.BlockSpec((1, d), const2),
                pl.BlockSpec((tm, tm), const2, pipeline_mode=once),
                pl.BlockSpec(wgu.shape, lambda i: (0, 0, 0, 0), pipeline_mode=once),
                pl.BlockSpec(wd.shape, const3, pipeline_mode=once)]
    args = [h2, gate8, grp, x1, g2, tri, wgu, wd]
    if final_w is not None:
        in_specs.append(pl.BlockSpec((1, d), const2))
        args.append(final_w)
    return pl.pallas_call(
        functools.partial(_moe_kernel, caps=caps, final_norm=final_w is not None),
        grid=(n_tiles,),
        in_specs=in_specs,
        out_specs=pl.BlockSpec((tm, d), rows),
        out_shape=jax.ShapeDtypeStruct((out_rows, d), F32),
        scratch_shapes=[pltpu.VMEM((SUBLANES, tm), F32)],
        input_output_aliases={3: 0},
        compiler_params=_cparams(("arbitrary",)),
        name="experts",
    )(*args)


_SWAP32 = np.concatenate([np.arange(8, 16), np.arange(0, 8), np.arange(24, 32), np.arange(16, 24)])


def _pad_cols(w, width):
    return jnp.pad(w, ((0, 0), (0, width - w.shape[1])))


def _in_weight(w_in):
    sizes = (2 * CONV_W, RET_QK_W, RET_QK_W, RET_W, RET_W, RET_W, MLA_Q_RANK, MLA_KV_RANK, MLA_D_ROPE)
    offs = np.concatenate([[0], np.cumsum(sizes)])
    conv, rq, rk, rv, gf, gb, cq, ckv, kr = [w_in[:, offs[i]:offs[i + 1]] for i in range(len(sizes))]
    swap192 = np.concatenate([h * ROPE_DIM + _SWAP32 for h in range(RET_HEADS)])
    d = w_in.shape[0]

    def place_rope(w):
        return jnp.concatenate([jnp.zeros((d, MLA_D_NOPE), w.dtype), w,
                                jnp.zeros((d, LANES - MLA_D_NOPE - MLA_D_ROPE), w.dtype)], axis=1)

    ext = jnp.concatenate([
        conv, _pad_cols(rq, 256), _pad_cols(rq[:, swap192], 256), _pad_cols(rk, 256), _pad_cols(rk[:, swap192], 256),
        rv, gf, gb, _pad_cols(cq, 256), ckv, place_rope(kr), place_rope(kr[:, _SWAP32])], axis=1)
    assert ext.shape[1] == IN_EXT
    return ext.astype(BF16)


def _uq_weight(w_uq):
    r = w_uq.shape[0]
    w = w_uq.reshape(r, MLA_HEADS, MLA_D_NOPE + MLA_D_ROPE)
    nope, rope = w[..., :MLA_D_NOPE], w[..., MLA_D_NOPE:]
    zpad = jnp.zeros((r, MLA_HEADS, HEAD_PAD - MLA_D_NOPE - MLA_D_ROPE), w.dtype)
    main = jnp.concatenate([nope, rope, zpad], axis=-1).reshape(r, ATT_W)
    part = jnp.concatenate([jnp.zeros_like(nope), rope[..., _SWAP32], zpad], axis=-1).reshape(r, ATT_W)
    return jnp.concatenate([main, part], axis=1).astype(BF16)


def _ukv_weight(w_ukv):
    r = w_ukv.shape[0]
    w = w_ukv.reshape(r, MLA_HEADS, MLA_D_NOPE + MLA_D_V)
    zpad = jnp.zeros((r, MLA_HEADS, HEAD_PAD - MLA_D_NOPE), w.dtype)
    kpart = jnp.concatenate([w[..., :MLA_D_NOPE], zpad], axis=-1).reshape(r, ATT_W)
    vpart = jnp.concatenate([w[..., MLA_D_NOPE:], zpad], axis=-1).reshape(r, ATT_W)
    return jnp.concatenate([kpart, vpart], axis=1).astype(BF16)


def _out_weight(w_out):
    d = w_out.shape[1]
    conv, ret = w_out[:CONV_W], w_out[CONV_W:CONV_W + RET_W]
    att = w_out[CONV_W + RET_W:].reshape(MLA_HEADS, MLA_D_V, d)
    att = jnp.pad(att, ((0, 0), (0, HEAD_PAD - MLA_D_V), (0, 0))).reshape(ATT_W, d)
    return jnp.concatenate([conv, ret, att], axis=0).astype(BF16)


def _rope_tables(n_ctx, seq):
    f32 = np.float32
    inv = (f32(ROPE_BASE) ** (-np.arange(ROPE_PAIRS, dtype=f32) / f32(ROPE_PAIRS))).astype(f32)
    ar = (np.arange(seq // GRID_W, dtype=f32)[:, None] * inv).astype(f32)
    ac = (np.arange(GRID_W, dtype=f32)[:, None] * inv).astype(f32)
    row_cs = np.concatenate([np.cos(ar), np.cos(ar)], axis=1).astype(f32)
    row_sn = np.concatenate([-np.sin(ar), np.sin(ar)], axis=1).astype(f32)
    col_cs = np.concatenate([np.cos(ac), np.cos(ac)], axis=1).astype(f32)
    col_sn = np.concatenate([-np.sin(ac), np.sin(ac)], axis=1).astype(f32)

    def expand(row_t, col_t, ctx_value):
        rows = seq // GRID_W
        lat = jnp.concatenate([jnp.broadcast_to(jnp.asarray(row_t)[:, None, :], (rows, GRID_W, 16)),
                               jnp.broadcast_to(jnp.asarray(col_t)[None, :, :], (rows, GRID_W, 16))],
                              axis=-1).reshape(seq, ROPE_DIM)
        full = jnp.concatenate([lat, jnp.full((n_ctx, ROPE_DIM), ctx_value, F32)], axis=0)
        return jnp.tile(full, (1, LANES // ROPE_DIM))

    return expand(row_cs, col_cs, 1.0), expand(row_sn, col_sn, 0.0)


def _router_weight(router_w):
    hi = router_w.astype(BF16)
    lo = (router_w - hi.astype(F32)).astype(BF16)
    return jnp.pad(jnp.concatenate([hi, lo], axis=1), ((0, 0), (0, LANES - 2 * N_EXPERTS)))


def _group_gates(gates_t):
    t = gates_t.shape[1]
    g4 = jnp.transpose(gates_t.reshape(N_GROUPS, EXPERTS_PER_GROUP, t), (0, 2, 1))
    hi = g4.astype(BF16)
    lo = (g4 - hi.astype(F32)).astype(BF16)
    return jnp.concatenate([hi, lo], axis=-1)


def _group_weights(w_gate, w_up, w_down):
    e, d, f = w_gate.shape
    gu = jnp.concatenate([w_gate, w_up], axis=-1).astype(BF16)
    gu = gu.reshape(N_GROUPS, EXPERTS_PER_GROUP, d, 2 * f)
    wd = w_down.astype(BF16).reshape(N_GROUPS, EXPERTS_PER_GROUP * f, d)
    return gu, wd


def kernel(x, c, ctx, c_ctx, ada_w, ada_b, norm1_w, norm2_w, w_in, conv_dw, conv_b, conv_ln_w, conv_ln_b,
           mla_q_norm_w, mla_w_uq, mla_kv_norm_w, mla_w_ukv, w_out, router_w, router_bias,
           moe_w_gate, moe_w_up, moe_w_down, final_norm_w):
    batch, seq, d = x.shape
    n_ctx = ctx.shape[1]
    depth = ada_w.shape[0]
    t_all = seq + n_ctx
    assert batch == 1 and d == D_MODEL
    assert n_ctx % ROW_TILE == 0 and seq % ROW_TILE == 0 and n_ctx <= MOE_TILE and seq % MOE_TILE == 0
    assert seq % ATT_TQ == 0 and t_all % ATT_TK == 0 and n_ctx % RET_CHUNK == 0 and seq % n_ctx == 0
    n_lat_tiles = seq // ROW_TILE
    n_all_tiles = t_all // ROW_TILE

    cond = jnp.zeros((8, d), F32).at[0].set(c_ctx).at[1].set(c[0])
    mod = _ada_call(cond, ada_w, ada_b)[:, :2, :].reshape(depth, 2, 6, d)
    mod8 = jnp.pad(mod, ((0, 0), (0, 0), (0, 2), (0, 0)))
    cs, sn = _rope_tables(n_ctx, seq)
    ret_tabs = _ret_tables()
    rw3 = _router_weight(router_w)
    rb = router_bias.reshape(N_EXPERTS, 1)

    x_lat, x_ctx, ctx_block0 = x[0], ctx[0], 0
    for l in range(depth):
        last = l == depth - 1
        u, rq, rk, rv, gf, gb, q_att, k_att, v_att = _inproj_call(
            x_lat, x_ctx, ctx_block0, t_all, norm1_w[l][None, :], mod8[l][:, :2, :], _in_weight(w_in[l]), cs, sn,
            mla_q_norm_w[l][None, :], _uq_weight(mla_w_uq[l]), mla_kv_norm_w[l][None, :], _ukv_weight(mla_w_ukv[l]),
            n_lat_tiles)
        conv = _conv_call(u, conv_dw[l], conv_b[l][None, :], conv_ln_w[l][None, :], conv_ln_b[l][None, :],
                          n_lat_tiles)
        of, ob = _ret_call(rq, rk, rv, gf, gb, ret_tabs, seq // RET_CHUNK)
        att = _attn_call(q_att, k_att, v_att, ATT_TQ, ATT_TK, seq // ATT_TQ)
        att_c = att
        if not last:
            tqc = min(ATT_TQ, n_ctx)
            att_c = _attn_call(q_att[:, seq:], k_att[:, seq:], v_att[:, seq:], tqc, n_ctx, n_ctx // tqc)
        n_tiles = n_lat_tiles if last else n_all_tiles
        x1, h2, gates_t, grp = _outproj_call(x_lat, x_ctx, ctx_block0, conv, of, ob, att, att_c,
                                             _out_weight(w_out[l]), mod8[l], norm2_w[l][None, :], rw3, rb,
                                             n_lat_tiles, n_tiles)
        gate8 = _group_gates(gates_t)
        wgu, wd = _group_weights(moe_w_gate[l], moe_w_up[l], moe_w_down[l])
        x_all = _moe_call(h2, gate8, grp, x1, mod[l, 1, 5][None, :], wgu, wd, MOE_TILE, 0, seq // MOE_TILE,
                          final_w=final_norm_w[None, :] if last else None)
        if not last:
            x_all = _moe_call(h2, gate8, grp, x_all, mod[l, 0, 5][None, :], wgu, wd, n_ctx, seq // n_ctx, 1)
        x_lat, x_ctx, ctx_block0 = x_all, x_all, n_lat_tiles
    return x_all[None]
```

```python
import functools
import math

import numpy as np
import jax
import jax.numpy as jnp
from jax import lax
from jax.experimental import pallas as pl
from jax.experimental.pallas import tpu as pltpu

F32 = jnp.float32
BF16 = jnp.bfloat16

D_MODEL = 1024
GRID_W = 64
CONV_W = 256
CONV_K = 31
RET_HEADS = 6
RET_DK = 32
RET_DV = 64
RET_QK_W = RET_HEADS * RET_DK
RET_W = RET_HEADS * RET_DV
RET_CHUNK = 128
RET_GN_EPS = 1e-5
MLA_HEADS = 6
MLA_Q_RANK = 192
MLA_KV_RANK = 128
MLA_D_NOPE = 64
MLA_D_ROPE = 32
MLA_D_V = 64
MLA_SCALE = (MLA_D_NOPE + MLA_D_ROPE) ** -0.5
ROPE_DIM = 32
ROPE_PAIRS = ROPE_DIM // 4
ROPE_BASE = 10000.0
N_EXPERTS = 16
N_GROUPS = 4
EXPERTS_PER_GROUP = N_EXPERTS // N_GROUPS
D_EXPERT = 256
EPS = 1e-6

LANES = 128
HEAD_PAD = LANES
ATT_W = MLA_HEADS * HEAD_PAD
LOG2E = math.log2(math.e)

C_CONV = 0
C_RQ = 512
C_RQS = 768
C_RK = 1024
C_RKS = 1280
C_RV = 1536
C_RGF = 1920
C_RGB = 2304
C_CQ = 2688
C_CKV = 2944
C_KR = 3072
C_KRS = 3200
IN_EXT = 3328

ROW_TILE = 256
MOE_TILE = 512
ATT_TQ = 1024
ATT_TK = 1280
VMEM_LIMIT = 48 * 1024 * 1024


def _cparams(sem):
    return pltpu.CompilerParams(dimension_semantics=sem, vmem_limit_bytes=VMEM_LIMIT)


def _silu(x):
    return x * jax.nn.sigmoid(x)


def _ada_kernel(c_ref, w_ref, b_ref, o_ref):
    s = _silu(c_ref[...])
    o_ref[...] = jnp.dot(s, w_ref[...], preferred_element_type=F32,
                         precision=lax.Precision.HIGHEST) + b_ref[...]


def _ada_call(cond, ada_w, ada_b):
    depth, d, n = ada_w.shape
    tn = 1536
    return pl.pallas_call(
        _ada_kernel,
        grid=(depth, n // tn),
        in_specs=[pl.BlockSpec((8, d), lambda l, j: (0, 0)),
                  pl.BlockSpec((None, d, tn), lambda l, j: (l, 0, j)),
                  pl.BlockSpec((None, 1, tn), lambda l, j: (l, 0, j))],
        out_specs=pl.BlockSpec((None, 8, tn), lambda l, j: (l, 0, j)),
        out_shape=jax.ShapeDtypeStruct((depth, 8, n), F32),
        compiler_params=_cparams(("arbitrary", "arbitrary")),
        name="adaln",
    )(cond, ada_w, ada_b.reshape(depth, 1, n))


def _inproj_kernel(x_ref, xc_ref, nw_ref, mod_ref, w_ref, cs_ref, sn_ref, qnw_ref, wuq_ref, kvnw_ref, wukv_ref,
                   u_ref, rq_ref, rk_ref, rv_ref, gf_ref, gb_ref, q_ref, k_ref, v_ref, *, n_lat_tiles):
    x = jnp.where(pl.program_id(0) < n_lat_tiles, x_ref[...], xc_ref[...])
    y = x * lax.rsqrt(jnp.mean(x * x, axis=-1, keepdims=True) + EPS) * nw_ref[...]
    h = (y * (1.0 + mod_ref[1:2, :]) + mod_ref[0:1, :]).astype(BF16)

    def proj(c0, width):
        return jnp.dot(h, w_ref[:, c0:c0 + width], preferred_element_type=F32)

    cs = cs_ref[...]
    sn = sn_ref[...]
    lane = lax.broadcasted_iota(jnp.int32, (1, LANES), 1)
    rope_lanes = (lane >= MLA_D_NOPE) & (lane < MLA_D_NOPE + MLA_D_ROPE)
    csq = jnp.where(rope_lanes, cs, 1.0)
    snq = jnp.where(rope_lanes, sn, 0.0)

    ag = proj(C_CONV, 2 * CONV_W)
    u_ref[...] = ag[:, :CONV_W] * jax.nn.sigmoid(ag[:, CONV_W:])

    cs192 = jnp.concatenate([cs, cs[:, :RET_QK_W - LANES]], axis=1)
    sn192 = jnp.concatenate([sn, sn[:, :RET_QK_W - LANES]], axis=1)
    rq = proj(C_RQ, 256)[:, :RET_QK_W] * cs192 + proj(C_RQS, 256)[:, :RET_QK_W] * sn192
    rq_ref[...] = rq.astype(BF16)
    rk = proj(C_RK, 256)[:, :RET_QK_W] * cs192 + proj(C_RKS, 256)[:, :RET_QK_W] * sn192
    rk_ref[...] = (rk * (RET_DK ** -0.5)).astype(BF16)
    rv_ref[...] = proj(C_RV, RET_W).astype(BF16)
    gf_ref[...] = _silu(proj(C_RGF, RET_W))
    gb_ref[...] = _silu(proj(C_RGB, RET_W))

    cq = proj(C_CQ, 256)[:, :MLA_Q_RANK]
    cqn = (cq * lax.rsqrt(jnp.mean(cq * cq, axis=-1, keepdims=True) + EPS) * qnw_ref[...]).astype(BF16)
    qa = jnp.dot(cqn, wuq_ref[:, :ATT_W], preferred_element_type=F32)
    qb = jnp.dot(cqn, wuq_ref[:, ATT_W:], preferred_element_type=F32)
    ckv = proj(C_CKV, MLA_KV_RANK)
    ckvn = (ckv * lax.rsqrt(jnp.mean(ckv * ckv, axis=-1, keepdims=True) + EPS) * kvnw_ref[...]).astype(BF16)
    ka = jnp.dot(ckvn, wukv_ref[:, :ATT_W], preferred_element_type=F32)
    va = jnp.dot(ckvn, wukv_ref[:, ATT_W:], preferred_element_type=F32)
    kr = proj(C_KR, LANES) * csq + proj(C_KRS, LANES) * snq
    ones_lane = (lane == MLA_D_V).astype(F32)
    for hd in range(MLA_HEADS):
        sl = slice(hd * HEAD_PAD, (hd + 1) * HEAD_PAD)
        q_ref[hd] = ((qa[:, sl] * csq + qb[:, sl] * snq) * (MLA_SCALE * LOG2E)).astype(BF16)
        k_ref[hd] = (ka[:, sl] + kr).astype(BF16)
        v_ref[hd] = (va[:, sl] + ones_lane).astype(BF16)


def _inproj_call(x_lat, x_ctx, ctx_block0, t_all, nw, mod, w_ext, cs, sn, qnw, wuq, kvnw, wukv, n_lat_tiles):
    d = x_lat.shape[1]
    tm = ROW_TILE
    row = lambda i: (i, 0)
    const2 = lambda i: (0, 0)
    head = lambda i: (0, i, 0)
    cls = lambda i: (jnp.where(i < n_lat_tiles, 1, 0), 0, 0)
    out_shape = (
        jax.ShapeDtypeStruct((t_all, CONV_W), F32),
        jax.ShapeDtypeStruct((t_all, RET_QK_W), BF16),
        jax.ShapeDtypeStruct((t_all, RET_QK_W), BF16),
        jax.ShapeDtypeStruct((t_all, RET_W), BF16),
        jax.ShapeDtypeStruct((t_all, RET_W), F32),
        jax.ShapeDtypeStruct((t_all, RET_W), F32),
        jax.ShapeDtypeStruct((MLA_HEADS, t_all, HEAD_PAD), BF16),
        jax.ShapeDtypeStruct((MLA_HEADS, t_all, HEAD_PAD), BF16),
        jax.ShapeDtypeStruct((MLA_HEADS, t_all, HEAD_PAD), BF16),
    )
    out_specs = (
        pl.BlockSpec((tm, CONV_W), row),
        pl.BlockSpec((tm, RET_QK_W), row),
        pl.BlockSpec((tm, RET_QK_W), row),
        pl.BlockSpec((tm, RET_W), row),
        pl.BlockSpec((tm, RET_W), row),
        pl.BlockSpec((tm, RET_W), row),
        pl.BlockSpec((MLA_HEADS, tm, HEAD_PAD), head),
        pl.BlockSpec((MLA_HEADS, tm, HEAD_PAD), head),
        pl.BlockSpec((MLA_HEADS, tm, HEAD_PAD), head),
    )
    return pl.pallas_call(
        functools.partial(_inproj_kernel, n_lat_tiles=n_lat_tiles),
        grid=(t_all // tm,),
        in_specs=[pl.BlockSpec((tm, d), lambda i: (jnp.minimum(i, n_lat_tiles - 1), 0)),
                  pl.BlockSpec((tm, d), lambda i: (ctx_block0 + jnp.maximum(i - n_lat_tiles, 0), 0)),
                  pl.BlockSpec((1, d), const2),
                  pl.BlockSpec((None, 2, d), cls),
                  pl.BlockSpec((d, IN_EXT), const2),
                  pl.BlockSpec((tm, LANES), row),
                  pl.BlockSpec((tm, LANES), row),
                  pl.BlockSpec((1, MLA_Q_RANK), const2),
                  pl.BlockSpec((MLA_Q_RANK, 2 * ATT_W), const2),
                  pl.BlockSpec((1, MLA_KV_RANK), const2),
                  pl.BlockSpec((MLA_KV_RANK, 2 * ATT_W), const2)],
        out_specs=out_specs,
        out_shape=out_shape,
        compiler_params=_cparams(("arbitrary",)),
        name="inproj",
    )(x_lat, x_ctx, nw, mod, w_ext, cs, sn, qnw, wuq, kvnw, wukv)


CONV_HALO = 16
CONV_SUB = 64
SUBLANES = 8


def _conv_kernel(prev_ref, cur_ref, next_ref, dw_ref, b_ref, lnw_ref, lnb_ref, o_ref, ext_ref, sh_ref, *,
                 n_lat_tiles):
    i = pl.program_id(0)
    n = pl.num_programs(0)
    tm = cur_ref.shape[0]
    seq_start = (i == 0) | (i == n_lat_tiles)
    seq_end = (i == n_lat_tiles - 1) | (i == n - 1)
    ext_ref[0:CONV_HALO, :] = jnp.where(seq_start, 0.0, prev_ref[...])
    ext_ref[CONV_HALO:CONV_HALO + tm, :] = cur_ref[...]
    ext_ref[CONV_HALO + tm:, :] = jnp.where(seq_end, 0.0, next_ref[...])
    span = sh_ref.shape[1]
    for ph in range(SUBLANES):
        sh_ref[ph] = ext_ref[pl.ds(ph, span), :]
    base = CONV_HALO - CONV_K // 2
    for r in range(tm // CONV_SUB):
        acc = jnp.zeros((CONV_SUB, CONV_W), F32) + b_ref[...]
        for k in range(CONV_K):
            off = base + k
            acc = acc + (sh_ref[off % SUBLANES, pl.ds(r * CONV_SUB + off - off % SUBLANES, CONV_SUB), :]
                         * dw_ref[k:k + 1, :])
        mu = jnp.mean(acc, axis=-1, keepdims=True)
        dlt = acc - mu
        var = jnp.mean(dlt * dlt, axis=-1, keepdims=True)
        y = dlt * lax.rsqrt(var + EPS) * lnw_ref[...] + lnb_ref[...]
        o_ref[r * CONV_SUB:(r + 1) * CONV_SUB, :] = _silu(y).astype(BF16)


def _conv_call(u, dw, b, lnw, lnb, n_lat_tiles):
    t_all = u.shape[0]
    tm = ROW_TILE
    hpt = tm // CONV_HALO
    n_halo = t_all // CONV_HALO
    const2 = lambda i: (0, 0)
    return pl.pallas_call(
        functools.partial(_conv_kernel, n_lat_tiles=n_lat_tiles),
        grid=(t_all // tm,),
        in_specs=[pl.BlockSpec((CONV_HALO, CONV_W), lambda i: (jnp.maximum(i * hpt - 1, 0), 0)),
                  pl.BlockSpec((tm, CONV_W), lambda i: (i, 0)),
                  pl.BlockSpec((CONV_HALO, CONV_W), lambda i: (jnp.minimum((i + 1) * hpt, n_halo - 1), 0)),
                  pl.BlockSpec((CONV_K, CONV_W), const2),
                  pl.BlockSpec((1, CONV_W), const2),
                  pl.BlockSpec((1, CONV_W), const2),
                  pl.BlockSpec((1, CONV_W), const2)],
        out_specs=pl.BlockSpec((tm, CONV_W), lambda i: (i, 0)),
        out_shape=jax.ShapeDtypeStruct((t_all, CONV_W), BF16),
        scratch_shapes=[pltpu.VMEM((tm + 2 * CONV_HALO, CONV_W), F32),
                        pltpu.VMEM((SUBLANES, tm + 2 * CONV_HALO - SUBLANES, CONV_W), F32)],
        compiler_params=_cparams(("arbitrary",)),
        name="conv",
    )(u, u, u, dw, b, lnw, lnb)


def _split_dot(x, a):
    hi = x.astype(BF16)
    lo = (x - hi.astype(F32)).astype(BF16)
    return (jnp.dot(hi, a, preferred_element_type=F32) + jnp.dot(lo, a, preferred_element_type=F32))


def _ret_direction(q, k, v, gate, r_ref, dmask, xi, zeta, gchunk, bdmask, avg, o_ref):
    lane_k = lax.broadcasted_iota(jnp.int32, (1, RET_QK_W), 1) // RET_DK
    lane_v = lax.broadcasted_iota(jnp.int32, (1, RET_W), 1) // RET_DV
    zero_k = jnp.zeros_like(k)
    zero_v = jnp.zeros_like(v)
    k_bd = jnp.concatenate([jnp.where(lane_k == hd, k, zero_k) for hd in range(RET_HEADS)], axis=0)
    v_bd = jnp.concatenate([jnp.where(lane_v == hd, v, zero_v) for hd in range(RET_HEADS)], axis=0)
    s = lax.dot_general(q, k_bd, (((1,), (1,)), ((), ())), preferred_element_type=F32)
    s = (s * dmask).astype(BF16)
    inner = jnp.dot(s, v_bd, preferred_element_type=F32)
    r = r_ref[...]
    cross = jnp.dot(q, r.astype(BF16), preferred_element_type=F32) * xi
    o = inner + cross
    kz = (k.astype(F32) * zeta).astype(BF16)
    ds = lax.dot_general(kz, v, (((0,), (0,)), ((), ())), preferred_element_type=F32)
    r_ref[...] = gchunk * r + ds * bdmask
    mu = jnp.dot(o.astype(BF16), avg, preferred_element_type=F32)
    dlt = o - mu
    var = jnp.dot((dlt * dlt).astype(BF16), avg, preferred_element_type=F32)
    o_ref[...] = (gate * (dlt * lax.rsqrt(var + RET_GN_EPS))).astype(BF16)


def _ret_kernel(qf_ref, kf_ref, vf_ref, gf_ref, qb_ref, kb_ref, vb_ref, gb_ref,
                dmf_ref, dmb_ref, xif_ref, xib_ref, ztf_ref, ztb_ref, gcf_ref, gcb_ref, bdm_ref, avg_ref,
                of_ref, ob_ref, rf_ref, rb_ref):
    @pl.when(pl.program_id(0) == 0)
    def _():
        rf_ref[...] = jnp.zeros_like(rf_ref)
        rb_ref[...] = jnp.zeros_like(rb_ref)

    bdm = bdm_ref[...]
    avg = avg_ref[...]
    _ret_direction(qf_ref[...], kf_ref[...], vf_ref[...], gf_ref[...], rf_ref, dmf_ref[...], xif_ref[...],
                   ztf_ref[...], gcf_ref[...], bdm, avg, of_ref)
    _ret_direction(qb_ref[...], kb_ref[...], vb_ref[...], gb_ref[...], rb_ref, dmb_ref[...], xib_ref[...],
                   ztb_ref[...], gcb_ref[...], bdm, avg, ob_ref)


def _ret_tables():
    c = RET_CHUNK
    f32 = np.float32
    gamma_f = (1.0 - 2.0 ** (-5.0 - np.arange(RET_HEADS, dtype=f32))).astype(f32)
    gamma_b = gamma_f[::-1]
    idx = np.arange(c, dtype=f32)
    diff = idx[:, None] - idx[None, :]

    def tables(gamma, reverse):
        lg = np.log(gamma).astype(f32)
        d = -diff if reverse else diff
        dm = np.where(d[None] >= 0, np.exp(np.maximum(d, 0.0)[None] * lg[:, None, None]), 0.0)
        dm = np.transpose(dm, (1, 0, 2)).reshape(c, RET_HEADS * c)
        xi_e = (c - idx) if reverse else (idx + 1.0)
        zt_e = idx if reverse else (c - 1.0 - idx)
        xi = np.repeat(np.exp(xi_e[:, None] * lg[None, :]), RET_DV, axis=1)
        zt = np.repeat(np.exp(zt_e[:, None] * lg[None, :]), RET_DK, axis=1)
        gc = np.repeat(np.exp(c * lg), RET_DV)[None, :]
        return [t.astype(f32) for t in (dm, xi, zt, gc)]

    dmf, xif, ztf, gcf = tables(gamma_f, False)
    dmb, xib, ztb, gcb = tables(gamma_b, True)
    hk = np.arange(RET_QK_W) // RET_DK
    hv = np.arange(RET_W) // RET_DV
    bdm = (hk[:, None] == hv[None, :]).astype(f32)
    avg = jnp.asarray((hv[:, None] == hv[None, :]).astype(f32) / RET_DV, dtype=BF16)
    return tuple(jnp.asarray(t) for t in (dmf, dmb, xif, xib, ztf, ztb, gcf, gcb, bdm)) + (avg,)


def _ret_call(rq, rk, rv, gf, gb, tabs, n_lat_chunks):
    t_all = rq.shape[0]
    c = RET_CHUNK
    n = t_all // c
    n_ctx_chunks = n - n_lat_chunks

    def fwd(i):
        return (jnp.where(i < n_ctx_chunks, n_lat_chunks + i, i - n_ctx_chunks), 0)

    def bwd(i):
        return (n - 1 - i, 0)

    const2 = lambda i: (0, 0)
    tab_specs = [pl.BlockSpec(t.shape, const2) for t in tabs]
    return pl.pallas_call(
        _ret_kernel,
        grid=(n,),
        in_specs=[pl.BlockSpec((c, RET_QK_W), fwd), pl.BlockSpec((c, RET_QK_W), fwd),
                  pl.BlockSpec((c, RET_W), fwd), pl.BlockSpec((c, RET_W), fwd),
                  pl.BlockSpec((c, RET_QK_W), bwd), pl.BlockSpec((c, RET_QK_W), bwd),
                  pl.BlockSpec((c, RET_W), bwd), pl.BlockSpec((c, RET_W), bwd)] + tab_specs,
        out_specs=(pl.BlockSpec((c, RET_W), fwd), pl.BlockSpec((c, RET_W), bwd)),
        out_shape=(jax.ShapeDtypeStruct((t_all, RET_W), BF16), jax.ShapeDtypeStruct((t_all, RET_W), BF16)),
        scratch_shapes=[pltpu.VMEM((RET_QK_W, RET_W), F32), pltpu.VMEM((RET_QK_W, RET_W), F32)],
        compiler_params=_cparams(("arbitrary",)),
        name="retention",
    )(rq, rk, rv, gf, rq, rk, rv, gb, *tabs)


ATT_SUB = 128
VT_ROWS = HEAD_PAD
PV_KEYS = 256


def _attn_kernel(q_ref, k_ref, vt_ref, o_ref, s0_scr, s1_scr, p0_scr, p1_scr, acc_scr):
    q = q_ref[...]
    tq = q.shape[0]
    nkc, tk, _ = k_ref.shape
    nsub = tk // ATT_SUB

    s_bufs = (s0_scr, s1_scr)
    p_bufs = (p0_scr, p1_scr)

    def pv_part(c, par, t, alpha):
        keys = pl.ds(t * PV_KEYS, PV_KEYS)
        part = jnp.dot(vt_ref[c, :, keys], p_bufs[par][keys, :], preferred_element_type=F32)
        if t == 0:
            acc_scr[...] = alpha * acc_scr[...] + part
        else:
            acc_scr[...] += part

    def pv(c, par, alpha):
        for t in range(tk // PV_KEYS):
            pv_part(c, par, t, alpha)

    def step(c, par, m_old, m_blk, alpha_prev, with_scores, with_pv):
        m_new = jnp.maximum(m_old, m_blk)
        alpha = jnp.exp2(m_old - m_new)
        mx = jnp.full((8, tq), -jnp.inf, F32)
        per_part = PV_KEYS // ATT_SUB
        for j in range(nsub):
            rows = pl.ds(j * ATT_SUB, ATT_SUB)
            p_bufs[par][rows, :] = jnp.exp2(s_bufs[par][rows, :] - m_new).astype(BF16)
            if with_scores:
                mx = jnp.maximum(mx, score_rows(c + 1, 1 - par, rows))
            if with_pv and j % per_part == per_part - 1:
                pv_part(c - 1, 1 - par, j // per_part, alpha_prev)
        return m_new, jnp.max(mx, axis=0, keepdims=True), alpha

    def score_rows(c, par, rows):
        s = jnp.dot(k_ref[c, rows, :], qt, preferred_element_type=F32)
        s_bufs[par][rows, :] = s
        return jnp.max(s.reshape(ATT_SUB // 8, 8, tq), axis=0)

    def scores(c, par):
        mx = jnp.full((8, tq), -jnp.inf, F32)
        for j in range(nsub):
            mx = jnp.maximum(mx, score_rows(c, par, pl.ds(j * ATT_SUB, ATT_SUB)))
        return jnp.max(mx, axis=0, keepdims=True)

    acc_scr[...] = jnp.zeros_like(acc_scr)
    qt = q.astype(F32).T.astype(BF16)
    m = jnp.full((1, tq), -jnp.inf, F32)
    m_blk = scores(0, 0)
    alpha = jnp.ones((1, tq), F32)
    if nkc > 1:
        m, m_blk, alpha = step(0, 0, m, m_blk, alpha, True, False)
        def body(c, carry):
            return lax.cond(c % 2 == 1,
                            lambda cr: step(c, 1, *cr, True, True),
                            lambda cr: step(c, 0, *cr, True, True), carry)

        m, m_blk, alpha = lax.fori_loop(1, nkc - 1, body, (m, m_blk, alpha))
    last = nkc - 1
    m, _, alpha = step(last, last % 2, m, m_blk, alpha, False, nkc > 1)
    pv(last, last % 2, alpha)
    acc = acc_scr[...]
    out_t = acc / acc[MLA_D_V:MLA_D_V + 1, :]
    if VT_ROWS < HEAD_PAD:
        out_t = jnp.concatenate([out_t, jnp.zeros((HEAD_PAD - VT_ROWS, tq), F32)], axis=0)
    o_ref[...] = out_t.T.astype(BF16)


def _attn_call(q, k, v, tq, tk, n_q_blocks):
    nh = q.shape[0]
    t_k = k.shape[1]
    nkc = t_k // tk
    k4 = k.reshape(nh, nkc, tk, HEAD_PAD)
    vt4 = jnp.swapaxes(v[:, :, :VT_ROWS].reshape(nh, nkc, tk, VT_ROWS), 2, 3)
    return pl.pallas_call(
        _attn_kernel,
        grid=(nh, n_q_blocks),
        in_specs=[pl.BlockSpec((None, tq, HEAD_PAD), lambda h, j: (h, j, 0)),
                  pl.BlockSpec((None, nkc, tk, HEAD_PAD), lambda h, j: (h, 0, 0, 0)),
                  pl.BlockSpec((None, nkc, VT_ROWS, tk), lambda h, j: (h, 0, 0, 0))],
        out_specs=pl.BlockSpec((tq, HEAD_PAD), lambda h, j: (j, h)),
        out_shape=jax.ShapeDtypeStruct((n_q_blocks * tq, nh * HEAD_PAD), BF16),
        scratch_shapes=[pltpu.VMEM((tk, tq), F32), pltpu.VMEM((tk, tq), F32),
                        pltpu.VMEM((tk, tq), BF16), pltpu.VMEM((tk, tq), BF16), pltpu.VMEM((VT_ROWS, tq), F32)],
        compiler_params=_cparams(("arbitrary", "arbitrary")),
        name="attention",
    )(q, k4, vt4)


def _top2_sum(a, b, c, d):
    hi1, lo1 = jnp.maximum(a, b), jnp.minimum(a, b)
    hi2, lo2 = jnp.maximum(c, d), jnp.minimum(c, d)
    return jnp.maximum(hi1, hi2) + jnp.maximum(jnp.minimum(hi1, hi2), jnp.maximum(lo1, lo2))


def _gates_t(aff, sel, gt_ref, grp_ref):
    rows = [sel[e:e + 1, :] for e in range(N_EXPERTS)]
    g_score = [_top2_sum(*rows[g * EXPERTS_PER_GROUP:(g + 1) * EXPERTS_PER_GROUP]) for g in range(N_GROUPS)]
    best = g_score[0]
    best_g = jnp.zeros_like(best, dtype=jnp.int32)
    for g in range(1, N_GROUPS):
        better = g_score[g] > best
        best = jnp.where(better, g_score[g], best)
        best_g = jnp.where(better, g, best_g)
    picked = []
    for e in range(N_EXPERTS):
        g = e // EXPERTS_PER_GROUP
        rank = jnp.zeros_like(best_g)
        for o in range(g * EXPERTS_PER_GROUP, (g + 1) * EXPERTS_PER_GROUP):
            if o == e:
                continue
            ahead = (rows[o] >= rows[e]) if o < e else (rows[o] > rows[e])
            rank = rank + jnp.where(ahead, 1, 0)
        picked.append(jnp.where(best_g == g, rank, 2) < 2)
    w = [jnp.where(picked[e], aff[e:e + 1, :], 0.0) for e in range(N_EXPERTS)]
    total = w[0]
    for e in range(1, N_EXPERTS):
        total = total + w[e]
    for e in range(N_EXPERTS):
        gt_ref[e:e + 1, :] = w[e] / total
    grp_ref[...] = best_g


OUT_HALF = D_MODEL // 2


def _outproj_kernel(x_ref, xc_ref, conv_ref, of_ref, ob_ref, att_ref, attc_ref, w_ref, mod_ref, nw_ref, rw_ref,
                    rb_ref, x1_ref, h2_ref, gt_ref, grp_ref, *, n_lat_tiles):
    is_lat = pl.program_id(0) < n_lat_tiles
    ret = (of_ref[...].astype(F32) + ob_ref[...].astype(F32)).astype(BF16)
    att = jnp.where(is_lat, att_ref[...], attc_ref[...])
    mix = jnp.concatenate([conv_ref[...], ret, att], axis=1)
    halves = []
    ssq = 0.0
    for hf in range(2):
        cols = slice(hf * OUT_HALF, (hf + 1) * OUT_HALF)
        o = jnp.dot(mix, w_ref[:, cols], preferred_element_type=F32)
        x1 = jnp.where(is_lat, x_ref[:, cols], xc_ref[:, cols]) + mod_ref[2:3, cols] * o
        x1_ref[:, cols] = x1
        ssq = ssq + jnp.sum(x1 * x1, axis=-1, keepdims=True)
        halves.append(x1)
    inv = lax.rsqrt(ssq * (1.0 / D_MODEL) + EPS)
    logit_parts = 0.0
    for hf in range(2):
        cols = slice(hf * OUT_HALF, (hf + 1) * OUT_HALF)
        h2 = (halves[hf] * inv * nw_ref[:, cols]) * (1.0 + mod_ref[4:5, cols]) + mod_ref[3:4, cols]
        hi = h2.astype(BF16)
        h2_ref[:, cols] = hi
        lo = (h2 - hi.astype(F32)).astype(BF16)
        logit_parts = (logit_parts + jnp.dot(hi, rw_ref[cols, :], preferred_element_type=F32)
                       + jnp.dot(lo, rw_ref[cols, :], preferred_element_type=F32))
    lt = logit_parts.T
    logits = lt[0:N_EXPERTS, :] + lt[N_EXPERTS:2 * N_EXPERTS, :]
    aff = jax.nn.sigmoid(logits)
    _gates_t(aff, aff + rb_ref[...], gt_ref, grp_ref)


def _outproj_call(x_lat, x_ctx, ctx_block0, conv, of, ob, att, att_c, w_ext, mod, nw, rw3, rb, n_lat_tiles,
                  n_tiles):
    d = x_lat.shape[1]
    tm = ROW_TILE
    row = lambda i: (i, 0)
    const2 = lambda i: (0, 0)
    lat = lambda i: (jnp.minimum(i, n_lat_tiles - 1), 0)
    cls = lambda i: (jnp.where(i < n_lat_tiles, 1, 0), 0, 0)
    return pl.pallas_call(
        functools.partial(_outproj_kernel, n_lat_tiles=n_lat_tiles),
        grid=(n_tiles,),
        in_specs=[pl.BlockSpec((tm, d), lat),
                  pl.BlockSpec((tm, d), lambda i: (ctx_block0 + jnp.maximum(i - n_lat_tiles, 0), 0)),
                  pl.BlockSpec((tm, CONV_W), row),
                  pl.BlockSpec((tm, RET_W), row),
                  pl.BlockSpec((tm, RET_W), row),
                  pl.BlockSpec((tm, ATT_W), lat),
                  pl.BlockSpec((tm, ATT_W), lambda i: (jnp.maximum(i - n_lat_tiles, 0), 0)),
                  pl.BlockSpec(w_ext.shape, const2),
                  pl.BlockSpec((None, 8, d), cls),
                  pl.BlockSpec((1, d), const2),
                  pl.BlockSpec((d, LANES), const2),
                  pl.BlockSpec((N_EXPERTS, 1), const2)],
        out_specs=(pl.BlockSpec((tm, d), row), pl.BlockSpec((tm, d), row),
                   pl.BlockSpec((N_EXPERTS, tm), lambda i: (0, i)), pl.BlockSpec((1, tm), lambda i: (0, i))),
        out_shape=(jax.ShapeDtypeStruct((n_tiles * tm, d), F32), jax.ShapeDtypeStruct((n_tiles * tm, d), BF16),
                   jax.ShapeDtypeStruct((N_EXPERTS, n_tiles * tm), F32),
                   jax.ShapeDtypeStruct((1, n_tiles * tm), jnp.int32)),
        compiler_params=_cparams(("arbitrary",)),
        name="outproj",
    )(x_lat, x_ctx, conv, of, ob, att, att_c, w_ext, mod, nw, rw3, rb)


MOE_DENSE_ROWS = 256


def _group_mlp(xb, gate4, wgu_ref, wd_ref):
    parts = []
    for k in range(EXPERTS_PER_GROUP):
        gu = jnp.dot(xb, wgu_ref[k], preferred_element_type=F32)
        parts.append((_silu(gu[:, :D_EXPERT]) * gu[:, D_EXPERT:] * gate4[:, k:k + 1]).astype(BF16))
    return jnp.dot(jnp.concatenate(parts, axis=1), wd_ref[...], preferred_element_type=F32)


def _moe_kernel(h_ref, gate8_ref, grp_ref, x1_ref, g2_ref, tri_ref, wgu_ref, wd_ref, *rest, caps, final_norm):
    rest = list(rest)
    fw_ref = rest.pop(0) if final_norm else None
    o_ref, cnt_scr = rest
    tm = h_ref.shape[0]
    grp = grp_ref[...]
    o_ref[...] = jnp.zeros_like(o_ref)
    rows8 = lax.broadcasted_iota(jnp.int32, (SUBLANES, tm), 0)
    member8 = jnp.where(rows8 == grp, 1.0, 0.0).astype(BF16)
    cnt_scr[...] = jnp.dot(member8, tri_ref[...], preferred_element_type=F32)

    def gates_of(g8):
        return g8[:, :EXPERTS_PER_GROUP] + g8[:, EXPERTS_PER_GROUP:]

    def group(g, carry):
        incl = cnt_scr[pl.ds(g, 1), :]
        count = jnp.max(incl)
        gate8_g, wgu_g, wd_g = gate8_ref.at[g], wgu_ref.at[g], wd_ref.at[g]

        def compacted(cap):
            pos = jnp.where(grp == g, incl.astype(jnp.int32) - 1, -1)
            slot = lax.broadcasted_iota(jnp.int32, (cap, tm), 0)
            onehot = jnp.where(slot == pos, 1.0, 0.0).astype(BF16)
            xg = jnp.dot(onehot, h_ref[...], preferred_element_type=F32).astype(BF16)
            g8 = jnp.dot(onehot, gate8_g[...], preferred_element_type=F32)
            y = _group_mlp(xg, gates_of(g8), wgu_g, wd_g)
            o_ref[...] += lax.dot_general(onehot, y.astype(BF16), (((0,), (0,)), ((), ())),
                                          preferred_element_type=F32)

        lower = 0
        for cap in caps:
            pl.when((count > lower) & (count <= cap))(functools.partial(compacted, cap))
            lower = cap

        @pl.when(count > caps[-1])
        def _():
            def chunk(ci, inner):
                rows = pl.ds(pl.multiple_of(ci * MOE_DENSE_ROWS, MOE_DENSE_ROWS), MOE_DENSE_ROWS)
                o_ref[rows, :] += _group_mlp(h_ref[rows, :], gates_of(gate8_g[rows, :].astype(F32)), wgu_g, wd_g)
                return inner

            lax.fori_loop(0, tm // MOE_DENSE_ROWS, chunk, 0)

        return carry

    lax.fori_loop(0, N_GROUPS, group, 0)
    x2 = x1_ref[...] + g2_ref[...] * o_ref[...]
    if final_norm:
        x2 = x2 * lax.rsqrt(jnp.mean(x2 * x2, axis=-1, keepdims=True) + EPS) * fw_ref[...]
    o_ref[...] = x2


def _moe_call(h2, gate8, grp, x1, g2, wgu, wd, tm, block0, n_tiles, final_w=None):
    out_rows, d = x1.shape
    caps = (tm // 4, 3 * tm // 8, tm // 2)
    assert all(cap % 16 == 0 for cap in caps) and tm % MOE_DENSE_ROWS == 0
    tri = jnp.asarray(np.triu(np.ones((tm, tm), np.float32)), dtype=BF16)
    rows = lambda i: (block0 + i, 0)
    const2 = lambda i: (0, 0)
    const3 = lambda i: (0, 0, 0)
    once = pl.Buffered(1)
    in_specs = [pl.BlockSpec((tm, d), rows),
                pl.BlockSpec((N_GROUPS, tm, 2 * EXPERTS_PER_GROUP), lambda i: (0, block0 + i, 0)),
                pl.BlockSpec((1, tm), lambda i: (0, block0 + i)),
                pl.BlockSpec((tm, d), rows),
                pl.BlockSpec((1, d), const2),
                pl.BlockSpec((tm, tm), const2, pipeline_mode=once),
                pl.BlockSpec(wgu.shape, lambda i: (0, 0, 0, 0), pipeline_mode=once),
                pl.BlockSpec(wd.shape, const3, pipeline_mode=once)]
    args = [h2, gate8, grp, x1, g2, tri, wgu, wd]
    if final_w is not None:
        in_specs.append(pl.BlockSpec((1, d), const2))
        args.append(final_w)
    return pl.pallas_call(
        functools.partial(_moe_kernel, caps=caps, final_norm=final_w is not None),
        grid=(n_tiles,),
        in_specs=in_specs,
        out_specs=pl.BlockSpec((tm, d), rows),
        out_shape=jax.ShapeDtypeStruct((out_rows, d), F32),
        scratch_shapes=[pltpu.VMEM((SUBLANES, tm), F32)],
        input_output_aliases={3: 0},
        compiler_params=_cparams(("arbitrary",)),
        name="experts",
    )(*args)


_SWAP32 = np.concatenate([np.arange(8, 16), np.arange(0, 8), np.arange(24, 32), np.arange(16, 24)])


def _pad_cols(w, width):
    return jnp.pad(w, ((0, 0), (0, width - w.shape[1])))


def _in_weight(w_in):
    sizes = (2 * CONV_W, RET_QK_W, RET_QK_W, RET_W, RET_W, RET_W, MLA_Q_RANK, MLA_KV_RANK, MLA_D_ROPE)
    offs = np.concatenate([[0], np.cumsum(sizes)])
    conv, rq, rk, rv, gf, gb, cq, ckv, kr = [w_in[:, offs[i]:offs[i + 1]] for i in range(len(sizes))]
    swap192 = np.concatenate([h * ROPE_DIM + _SWAP32 for h in range(RET_HEADS)])
    d = w_in.shape[0]

    def place_rope(w):
        return jnp.concatenate([jnp.zeros((d, MLA_D_NOPE), w.dtype), w,
                                jnp.zeros((d, LANES - MLA_D_NOPE - MLA_D_ROPE), w.dtype)], axis=1)

    ext = jnp.concatenate([
        conv, _pad_cols(rq, 256), _pad_cols(rq[:, swap192], 256), _pad_cols(rk, 256), _pad_cols(rk[:, swap192], 256),
        rv, gf, gb, _pad_cols(cq, 256), ckv, place_rope(kr), place_rope(kr[:, _SWAP32])], axis=1)
    assert ext.shape[1] == IN_EXT
    return ext.astype(BF16)


def _uq_weight(w_uq):
    r = w_uq.shape[0]
    w = w_uq.reshape(r, MLA_HEADS, MLA_D_NOPE + MLA_D_ROPE)
    nope, rope = w[..., :MLA_D_NOPE], w[..., MLA_D_NOPE:]
    zpad = jnp.zeros((r, MLA_HEADS, HEAD_PAD - MLA_D_NOPE - MLA_D_ROPE), w.dtype)
    main = jnp.concatenate([nope, rope, zpad], axis=-1).reshape(r, ATT_W)
    part = jnp.concatenate([jnp.zeros_like(nope), rope[..., _SWAP32], zpad], axis=-1).reshape(r, ATT_W)
    return jnp.concatenate([main, part], axis=1).astype(BF16)


def _ukv_weight(w_ukv):
    r = w_ukv.shape[0]
    w = w_ukv.reshape(r, MLA_HEADS, MLA_D_NOPE + MLA_D_V)
    zpad = jnp.zeros((r, MLA_HEADS, HEAD_PAD - MLA_D_NOPE), w.dtype)
    kpart = jnp.concatenate([w[..., :MLA_D_NOPE], zpad], axis=-1).reshape(r, ATT_W)
    vpart = jnp.concatenate([w[..., MLA_D_NOPE:], zpad], axis=-1).reshape(r, ATT_W)
    return jnp.concatenate([kpart, vpart], axis=1).astype(BF16)


def _out_weight(w_out):
    d = w_out.shape[1]
    conv, ret = w_out[:CONV_W], w_out[CONV_W:CONV_W + RET_W]
    att = w_out[CONV_W + RET_W:].reshape(MLA_HEADS, MLA_D_V, d)
    att = jnp.pad(att, ((0, 0), (0, HEAD_PAD - MLA_D_V), (0, 0))).reshape(ATT_W, d)
    return jnp.concatenate([conv, ret, att], axis=0).astype(BF16)


def _rope_tables(n_ctx, seq):
    f32 = np.float32
    inv = (f32(ROPE_BASE) ** (-np.arange(ROPE_PAIRS, dtype=f32) / f32(ROPE_PAIRS))).astype(f32)
    ar = (np.arange(seq // GRID_W, dtype=f32)[:, None] * inv).astype(f32)
    ac = (np.arange(GRID_W, dtype=f32)[:, None] * inv).astype(f32)
    row_cs = np.concatenate([np.cos(ar), np.cos(ar)], axis=1).astype(f32)
    row_sn = np.concatenate([-np.sin(ar), np.sin(ar)], axis=1).astype(f32)
    col_cs = np.concatenate([np.cos(ac), np.cos(ac)], axis=1).astype(f32)
    col_sn = np.concatenate([-np.sin(ac), np.sin(ac)], axis=1).astype(f32)

    def expand(row_t, col_t, ctx_value):
        rows = seq // GRID_W
        lat = jnp.concatenate([jnp.broadcast_to(jnp.asarray(row_t)[:, None, :], (rows, GRID_W, 16)),
                               jnp.broadcast_to(jnp.asarray(col_t)[None, :, :], (rows, GRID_W, 16))],
                              axis=-1).reshape(seq, ROPE_DIM)
        full = jnp.concatenate([lat, jnp.full((n_ctx, ROPE_DIM), ctx_value, F32)], axis=0)
        return jnp.tile(full, (1, LANES // ROPE_DIM))

    return expand(row_cs, col_cs, 1.0), expand(row_sn, col_sn, 0.0)


def _router_weight(router_w):
    hi = router_w.astype(BF16)
    lo = (router_w - hi.astype(F32)).astype(BF16)
    return jnp.pad(jnp.concatenate([hi, lo], axis=1), ((0, 0), (0, LANES - 2 * N_EXPERTS)))


def _group_gates(gates_t):
    t = gates_t.shape[1]
    g4 = jnp.transpose(gates_t.reshape(N_GROUPS, EXPERTS_PER_GROUP, t), (0, 2, 1))
    hi = g4.astype(BF16)
    lo = (g4 - hi.astype(F32)).astype(BF16)
    return jnp.concatenate([hi, lo], axis=-1)


def _group_weights(w_gate, w_up, w_down):
    e, d, f = w_gate.shape
    gu = jnp.concatenate([w_gate, w_up], axis=-1).astype(BF16)
    gu = gu.reshape(N_GROUPS, EXPERTS_PER_GROUP, d, 2 * f)
    wd = w_down.astype(BF16).reshape(N_GROUPS, EXPERTS_PER_GROUP * f, d)
    return gu, wd


def kernel(x, c, ctx, c_ctx, ada_w, ada_b, norm1_w, norm2_w, w_in, conv_dw, conv_b, conv_ln_w, conv_ln_b,
           mla_q_norm_w, mla_w_uq, mla_kv_norm_w, mla_w_ukv, w_out, router_w, router_bias,
           moe_w_gate, moe_w_up, moe_w_down, final_norm_w):
    batch, seq, d = x.shape
    n_ctx = ctx.shape[1]
    depth = ada_w.shape[0]
    t_all = seq + n_ctx
    assert batch == 1 and d == D_MODEL
    assert n_ctx % ROW_TILE == 0 and seq % ROW_TILE == 0 and n_ctx <= MOE_TILE and seq % MOE_TILE == 0
    assert seq % ATT_TQ == 0 and t_all % ATT_TK == 0 and n_ctx % RET_CHUNK == 0 and seq % n_ctx == 0
    n_lat_tiles = seq // ROW_TILE
    n_all_tiles = t_all // ROW_TILE

    cond = jnp.zeros((8, d), F32).at[0].set(c_ctx).at[1].set(c[0])
    mod = _ada_call(cond, ada_w, ada_b)[:, :2, :].reshape(depth, 2, 6, d)
    mod8 = jnp.pad(mod, ((0, 0), (0, 0), (0, 2), (0, 0)))
    cs, sn = _rope_tables(n_ctx, seq)
    ret_tabs = _ret_tables()
    rw3 = _router_weight(router_w)
    rb = router_bias.reshape(N_EXPERTS, 1)

    x_lat, x_ctx, ctx_block0 = x[0], ctx[0], 0
    for l in range(depth):
        last = l == depth - 1
        u, rq, rk, rv, gf, gb, q_att, k_att, v_att = _inproj_call(
            x_lat, x_ctx, ctx_block0, t_all, norm1_w[l][None, :], mod8[l][:, :2, :], _in_weight(w_in[l]), cs, sn,
            mla_q_norm_w[l][None, :], _uq_weight(mla_w_uq[l]), mla_kv_norm_w[l][None, :], _ukv_weight(mla_w_ukv[l]),
            n_lat_tiles)
        conv = _conv_call(u, conv_dw[l], conv_b[l][None, :], conv_ln_w[l][None, :], conv_ln_b[l][None, :],
                          n_lat_tiles)
        of, ob = _ret_call(rq, rk, rv, gf, gb, ret_tabs, seq // RET_CHUNK)
        att = _attn_call(q_att, k_att, v_att, ATT_TQ, ATT_TK, seq // ATT_TQ)
        att_c = att
        if not last:
            tqc = min(ATT_TQ, n_ctx)
            att_c = _attn_call(q_att[:, seq:], k_att[:, seq:], v_att[:, seq:], tqc, n_ctx, n_ctx // tqc)
        n_tiles = n_lat_tiles if last else n_all_tiles
        x1, h2, gates_t, grp = _outproj_call(x_lat, x_ctx, ctx_block0, conv, of, ob, att, att_c,
                                             _out_weight(w_out[l]), mod8[l], norm2_w[l][None, :], rw3, rb,
                                             n_lat_tiles, n_tiles)
        gate8 = _group_gates(gates_t)
        wgu, wd = _group_weights(moe_w_gate[l], moe_w_up[l], moe_w_down[l])
        x_all = _moe_call(h2, gate8, grp, x1, mod[l, 1, 5][None, :], wgu, wd, MOE_TILE, 0, seq // MOE_TILE,
                          final_w=final_norm_w[None, :] if last else None)
        if not last:
            x_all = _moe_call(h2, gate8, grp, x_all, mod[l, 0, 5][None, :], wgu, wd, n_ctx, seq // n_ctx, 1)
        x_lat, x_ctx, ctx_block0 = x_all, x_all, n_lat_tiles
    return x_all[None]
```

```python
import functools
import math

import numpy as np
import jax
import jax.numpy as jnp
from jax import lax
from jax.experimental import pallas as pl
from jax.experimental.pallas import tpu as pltpu

F32 = jnp.float32
BF16 = jnp.bfloat16

D_MODEL = 1024
GRID_W = 64
CONV_W = 256
CONV_K = 31
RET_HEADS = 6
RET_DK = 32
RET_DV = 64
RET_QK_W = RET_HEADS * RET_DK
RET_W = RET_HEADS * RET_DV
RET_CHUNK = 128
RET_PER_STEP = 2
RET_GN_EPS = 1e-5
MLA_HEADS = 6
MLA_Q_RANK = 192
MLA_KV_RANK = 128
MLA_D_NOPE = 64
MLA_D_ROPE = 32
MLA_D_V = 64
MLA_SCALE = (MLA_D_NOPE + MLA_D_ROPE) ** -0.5
ROPE_DIM = 32
ROPE_PAIRS = ROPE_DIM // 4
ROPE_BASE = 10000.0
N_EXPERTS = 16
N_GROUPS = 4
EXPERTS_PER_GROUP = N_EXPERTS // N_GROUPS
D_EXPERT = 256
EPS = 1e-6

LANES = 128
HEAD_PAD = LANES
ATT_W = MLA_HEADS * HEAD_PAD
LOG2E = math.log2(math.e)

C_CONV = 0
C_RQ = 512
C_RQS = 768
C_RK = 1024
C_RKS = 1280
C_RV = 1536
C_RGF = 1920
C_RGB = 2304
C_CQ = 2688
C_CKV = 2944
C_KR = 3072
C_KRS = 3200
IN_EXT = 3328

ROW_TILE = 256
MOE_TILE = 512
ATT_TQ = 1024
ATT_TK = 1280
VMEM_LIMIT = 48 * 1024 * 1024


def _cparams(sem):
    return pltpu.CompilerParams(dimension_semantics=sem, vmem_limit_bytes=VMEM_LIMIT)


def _silu(x):
    return x * jax.nn.sigmoid(x)


def _ada_kernel(c_ref, w_ref, b_ref, o_ref):
    s = _silu(c_ref[...])
    o_ref[...] = jnp.dot(s, w_ref[...], preferred_element_type=F32,
                         precision=lax.Precision.HIGHEST) + b_ref[...]


def _ada_call(cond, ada_w, ada_b):
    depth, d, n = ada_w.shape
    tn = 1536
    return pl.pallas_call(
        _ada_kernel,
        grid=(depth, n // tn),
        in_specs=[pl.BlockSpec((8, d), lambda l, j: (0, 0)),
                  pl.BlockSpec((None, d, tn), lambda l, j: (l, 0, j)),
                  pl.BlockSpec((None, 1, tn), lambda l, j: (l, 0, j))],
        out_specs=pl.BlockSpec((None, 8, tn), lambda l, j: (l, 0, j)),
        out_shape=jax.ShapeDtypeStruct((depth, 8, n), F32),
        compiler_params=_cparams(("arbitrary", "arbitrary")),
        name="adaln",
    )(cond, ada_w, ada_b.reshape(depth, 1, n))


def _inproj_kernel(x_ref, xc_ref, nw_ref, mod_ref, w_ref, cs_ref, sn_ref, qnw_ref, wuq_ref, kvnw_ref, wukv_ref,
                   u_ref, rq_ref, rk_ref, rv_ref, gf_ref, gb_ref, q_ref, k_ref, v_ref, *, n_lat_tiles):
    x = jnp.where(pl.program_id(0) < n_lat_tiles, x_ref[...], xc_ref[...])
    y = x * lax.rsqrt(jnp.mean(x * x, axis=-1, keepdims=True) + EPS) * nw_ref[...]
    h = (y * (1.0 + mod_ref[1:2, :]) + mod_ref[0:1, :]).astype(BF16)

    def proj(c0, width):
        return jnp.dot(h, w_ref[:, c0:c0 + width], preferred_element_type=F32)

    cs = cs_ref[...]
    sn = sn_ref[...]
    lane = lax.broadcasted_iota(jnp.int32, (1, LANES), 1)
    rope_lanes = (lane >= MLA_D_NOPE) & (lane < MLA_D_NOPE + MLA_D_ROPE)
    csq = jnp.where(rope_lanes, cs, 1.0)
    snq = jnp.where(rope_lanes, sn, 0.0)

    ag = proj(C_CONV, 2 * CONV_W)
    u_ref[...] = ag[:, :CONV_W] * jax.nn.sigmoid(ag[:, CONV_W:])

    cs192 = jnp.concatenate([cs, cs[:, :RET_QK_W - LANES]], axis=1)
    sn192 = jnp.concatenate([sn, sn[:, :RET_QK_W - LANES]], axis=1)
    rq = proj(C_RQ, 256)[:, :RET_QK_W] * cs192 + proj(C_RQS, 256)[:, :RET_QK_W] * sn192
    rq_ref[...] = rq.astype(BF16)
    rk = proj(C_RK, 256)[:, :RET_QK_W] * cs192 + proj(C_RKS, 256)[:, :RET_QK_W] * sn192
    rk_ref[...] = (rk * (RET_DK ** -0.5)).astype(BF16)
    rv_ref[...] = proj(C_RV, RET_W).astype(BF16)
    gf_ref[...] = _silu(proj(C_RGF, RET_W))
    gb_ref[...] = _silu(proj(C_RGB, RET_W))

    cq = proj(C_CQ, 256)[:, :MLA_Q_RANK]
    cqn = (cq * lax.rsqrt(jnp.mean(cq * cq, axis=-1, keepdims=True) + EPS) * qnw_ref[...]).astype(BF16)
    qa = jnp.dot(cqn, wuq_ref[:, :ATT_W], preferred_element_type=F32)
    qb = jnp.dot(cqn, wuq_ref[:, ATT_W:], preferred_element_type=F32)
    ckv = proj(C_CKV, MLA_KV_RANK)
    ckvn = (ckv * lax.rsqrt(jnp.mean(ckv * ckv, axis=-1, keepdims=True) + EPS) * kvnw_ref[...]).astype(BF16)
    ka = jnp.dot(ckvn, wukv_ref[:, :ATT_W], preferred_element_type=F32)
    va = jnp.dot(ckvn, wukv_ref[:, ATT_W:], preferred_element_type=F32)
    kr = proj(C_KR, LANES) * csq + proj(C_KRS, LANES) * snq
    ones_lane = (lane == MLA_D_V).astype(F32)
    for hd in range(MLA_HEADS):
        sl = slice(hd * HEAD_PAD, (hd + 1) * HEAD_PAD)
        q_ref[hd] = ((qa[:, sl] * csq + qb[:, sl] * snq) * (MLA_SCALE * LOG2E)).astype(BF16)
        k_ref[hd] = (ka[:, sl] + kr).astype(BF16)
        v_ref[hd] = (va[:, sl] + ones_lane).astype(BF16)


def _inproj_call(x_lat, x_ctx, ctx_block0, t_all, nw, mod, w_ext, cs, sn, qnw, wuq, kvnw, wukv, n_lat_tiles):
    d = x_lat.shape[1]
    tm = ROW_TILE
    row = lambda i: (i, 0)
    const2 = lambda i: (0, 0)
    head = lambda i: (0, i, 0)
    cls = lambda i: (jnp.where(i < n_lat_tiles, 1, 0), 0, 0)
    out_shape = (
        jax.ShapeDtypeStruct((t_all, CONV_W), F32),
        jax.ShapeDtypeStruct((t_all, RET_QK_W), BF16),
        jax.ShapeDtypeStruct((t_all, RET_QK_W), BF16),
        jax.ShapeDtypeStruct((t_all, RET_W), BF16),
        jax.ShapeDtypeStruct((t_all, RET_W), F32),
        jax.ShapeDtypeStruct((t_all, RET_W), F32),
        jax.ShapeDtypeStruct((MLA_HEADS, t_all, HEAD_PAD), BF16),
        jax.ShapeDtypeStruct((MLA_HEADS, t_all, HEAD_PAD), BF16),
        jax.ShapeDtypeStruct((MLA_HEADS, t_all, HEAD_PAD), BF16),
    )
    out_specs = (
        pl.BlockSpec((tm, CONV_W), row),
        pl.BlockSpec((tm, RET_QK_W), row),
        pl.BlockSpec((tm, RET_QK_W), row),
        pl.BlockSpec((tm, RET_W), row),
        pl.BlockSpec((tm, RET_W), row),
        pl.BlockSpec((tm, RET_W), row),
        pl.BlockSpec((MLA_HEADS, tm, HEAD_PAD), head),
        pl.BlockSpec((MLA_HEADS, tm, HEAD_PAD), head),
        pl.BlockSpec((MLA_HEADS, tm, HEAD_PAD), head),
    )
    return pl.pallas_call(
        functools.partial(_inproj_kernel, n_lat_tiles=n_lat_tiles),
        grid=(t_all // tm,),
        in_specs=[pl.BlockSpec((tm, d), lambda i: (jnp.minimum(i, n_lat_tiles - 1), 0)),
                  pl.BlockSpec((tm, d), lambda i: (ctx_block0 + jnp.maximum(i - n_lat_tiles, 0), 0)),
                  pl.BlockSpec((1, d), const2),
                  pl.BlockSpec((None, 2, d), cls),
                  pl.BlockSpec((d, IN_EXT), const2),
                  pl.BlockSpec((tm, LANES), row),
                  pl.BlockSpec((tm, LANES), row),
                  pl.BlockSpec((1, MLA_Q_RANK), const2),
                  pl.BlockSpec((MLA_Q_RANK, 2 * ATT_W), const2),
                  pl.BlockSpec((1, MLA_KV_RANK), const2),
                  pl.BlockSpec((MLA_KV_RANK, 2 * ATT_W), const2)],
        out_specs=out_specs,
        out_shape=out_shape,
        compiler_params=_cparams(("arbitrary",)),
        name="inproj",
    )(x_lat, x_ctx, nw, mod, w_ext, cs, sn, qnw, wuq, kvnw, wukv)


CONV_HALO = 16
CONV_SUB = 64
SUBLANES = 8


def _conv_kernel(prev_ref, cur_ref, next_ref, dw_ref, b_ref, lnw_ref, lnb_ref, o_ref, ext_ref, sh_ref, *,
                 n_lat_tiles):
    i = pl.program_id(0)
    n = pl.num_programs(0)
    tm = cur_ref.shape[0]
    seq_start = (i == 0) | (i == n_lat_tiles)
    seq_end = (i == n_lat_tiles - 1) | (i == n - 1)
    ext_ref[0:CONV_HALO, :] = jnp.where(seq_start, 0.0, prev_ref[...])
    ext_ref[CONV_HALO:CONV_HALO + tm, :] = cur_ref[...]
    ext_ref[CONV_HALO + tm:, :] = jnp.where(seq_end, 0.0, next_ref[...])
    span = sh_ref.shape[1]
    for ph in range(SUBLANES):
        sh_ref[ph] = ext_ref[pl.ds(ph, span), :]
    base = CONV_HALO - CONV_K // 2
    for r in range(tm // CONV_SUB):
        acc = jnp.zeros((CONV_SUB, CONV_W), F32) + b_ref[...]
        for k in range(CONV_K):
            off = base + k
            acc = acc + (sh_ref[off % SUBLANES, pl.ds(r * CONV_SUB + off - off % SUBLANES, CONV_SUB), :]
                         * dw_ref[k:k + 1, :])
        mu = jnp.mean(acc, axis=-1, keepdims=True)
        dlt = acc - mu
        var = jnp.mean(dlt * dlt, axis=-1, keepdims=True)
        y = dlt * lax.rsqrt(var + EPS) * lnw_ref[...] + lnb_ref[...]
        o_ref[r * CONV_SUB:(r + 1) * CONV_SUB, :] = _silu(y).astype(BF16)


def _conv_call(u, dw, b, lnw, lnb, n_lat_tiles):
    t_all = u.shape[0]
    tm = ROW_TILE
    hpt = tm // CONV_HALO
    n_halo = t_all // CONV_HALO
    const2 = lambda i: (0, 0)
    return pl.pallas_call(
        functools.partial(_conv_kernel, n_lat_tiles=n_lat_tiles),
        grid=(t_all // tm,),
        in_specs=[pl.BlockSpec((CONV_HALO, CONV_W), lambda i: (jnp.maximum(i * hpt - 1, 0), 0)),
                  pl.BlockSpec((tm, CONV_W), lambda i: (i, 0)),
                  pl.BlockSpec((CONV_HALO, CONV_W), lambda i: (jnp.minimum((i + 1) * hpt, n_halo - 1), 0)),
                  pl.BlockSpec((CONV_K, CONV_W), const2),
                  pl.BlockSpec((1, CONV_W), const2),
                  pl.BlockSpec((1, CONV_W), const2),
                  pl.BlockSpec((1, CONV_W), const2)],
        out_specs=pl.BlockSpec((tm, CONV_W), lambda i: (i, 0)),
        out_shape=jax.ShapeDtypeStruct((t_all, CONV_W), BF16),
        scratch_shapes=[pltpu.VMEM((tm + 2 * CONV_HALO, CONV_W), F32),
                        pltpu.VMEM((SUBLANES, tm + 2 * CONV_HALO - SUBLANES, CONV_W), F32)],
        compiler_params=_cparams(("arbitrary",)),
        name="conv",
    )(u, u, u, dw, b, lnw, lnb)


def _split_dot(x, a):
    hi = x.astype(BF16)
    lo = (x - hi.astype(F32)).astype(BF16)
    return (jnp.dot(hi, a, preferred_element_type=F32) + jnp.dot(lo, a, preferred_element_type=F32))


def _ret_direction(q, k, v, gate, r_ref, dmask, xi, zeta, gchunk, bdmask, avg, o_ref):
    lane_k = lax.broadcasted_iota(jnp.int32, (1, RET_QK_W), 1) // RET_DK
    lane_v = lax.broadcasted_iota(jnp.int32, (1, RET_W), 1) // RET_DV
    zero_k = jnp.zeros_like(k)
    zero_v = jnp.zeros_like(v)
    k_bd = jnp.concatenate([jnp.where(lane_k == hd, k, zero_k) for hd in range(RET_HEADS)], axis=0)
    v_bd = jnp.concatenate([jnp.where(lane_v == hd, v, zero_v) for hd in range(RET_HEADS)], axis=0)
    s = lax.dot_general(q, k_bd, (((1,), (1,)), ((), ())), preferred_element_type=F32)
    s = (s * dmask).astype(BF16)
    inner = jnp.dot(s, v_bd, preferred_element_type=F32)
    r = r_ref[...]
    cross = jnp.dot(q, r.astype(BF16), preferred_element_type=F32) * xi
    o = inner + cross
    kz = (k.astype(F32) * zeta).astype(BF16)
    ds = lax.dot_general(kz, v, (((0,), (0,)), ((), ())), preferred_element_type=F32)
    r_ref[...] = gchunk * r + ds * bdmask
    mu = jnp.dot(o.astype(BF16), avg, preferred_element_type=F32)
    dlt = o - mu
    var = jnp.dot((dlt * dlt).astype(BF16), avg, preferred_element_type=F32)
    o_ref[...] = (gate * (dlt * lax.rsqrt(var + RET_GN_EPS))).astype(BF16)


def _ret_kernel(qf_ref, kf_ref, vf_ref, gf_ref, qb_ref, kb_ref, vb_ref, gb_ref,
                dmf_ref, dmb_ref, xif_ref, xib_ref, ztf_ref, ztb_ref, gcf_ref, gcb_ref, bdm_ref, avg_ref,
                of_ref, ob_ref, rf_ref, rb_ref):
    @pl.when(pl.program_id(0) == 0)
    def _():
        rf_ref[...] = jnp.zeros_like(rf_ref)
        rb_ref[...] = jnp.zeros_like(rb_ref)

    bdm = bdm_ref[...]
    avg = avg_ref[...]
    c = RET_CHUNK
    for s in range(RET_PER_STEP):
        fr = pl.ds(s * c, c)
        br = pl.ds((RET_PER_STEP - 1 - s) * c, c)
        _ret_direction(qf_ref[fr, :], kf_ref[fr, :], vf_ref[fr, :], gf_ref[fr, :], rf_ref, dmf_ref[...],
                       xif_ref[...], ztf_ref[...], gcf_ref[...], bdm, avg, of_ref.at[fr, :])
        _ret_direction(qb_ref[br, :], kb_ref[br, :], vb_ref[br, :], gb_ref[br, :], rb_ref, dmb_ref[...],
                       xib_ref[...], ztb_ref[...], gcb_ref[...], bdm, avg, ob_ref.at[br, :])


def _ret_tables():
    c = RET_CHUNK
    f32 = np.float32
    gamma_f = (1.0 - 2.0 ** (-5.0 - np.arange(RET_HEADS, dtype=f32))).astype(f32)
    gamma_b = gamma_f[::-1]
    idx = np.arange(c, dtype=f32)
    diff = idx[:, None] - idx[None, :]

    def tables(gamma, reverse):
        lg = np.log(gamma).astype(f32)
        d = -diff if reverse else diff
        dm = np.where(d[None] >= 0, np.exp(np.maximum(d, 0.0)[None] * lg[:, None, None]), 0.0)
        dm = np.transpose(dm, (1, 0, 2)).reshape(c, RET_HEADS * c)
        xi_e = (c - idx) if reverse else (idx + 1.0)
        zt_e = idx if reverse else (c - 1.0 - idx)
        xi = np.repeat(np.exp(xi_e[:, None] * lg[None, :]), RET_DV, axis=1)
        zt = np.repeat(np.exp(zt_e[:, None] * lg[None, :]), RET_DK, axis=1)
        gc = np.repeat(np.exp(c * lg), RET_DV)[None, :]
        return [t.astype(f32) for t in (dm, xi, zt, gc)]

    dmf, xif, ztf, gcf = tables(gamma_f, False)
    dmb, xib, ztb, gcb = tables(gamma_b, True)
    hk = np.arange(RET_QK_W) // RET_DK
    hv = np.arange(RET_W) // RET_DV
    bdm = (hk[:, None] == hv[None, :]).astype(f32)
    avg = jnp.asarray((hv[:, None] == hv[None, :]).astype(f32) / RET_DV, dtype=BF16)
    return tuple(jnp.asarray(t) for t in (dmf, dmb, xif, xib, ztf, ztb, gcf, gcb, bdm)) + (avg,)


def _ret_call(rq, rk, rv, gf, gb, tabs, n_lat_rows):
    t_all = rq.shape[0]
    c = RET_CHUNK * RET_PER_STEP
    assert n_lat_rows % c == 0 and t_all % c == 0
    n = t_all // c
    n_lat_chunks = n_lat_rows // c
    n_ctx_chunks = n - n_lat_chunks

    def fwd(i):
        return (jnp.where(i < n_ctx_chunks, n_lat_chunks + i, i - n_ctx_chunks), 0)

    def bwd(i):
        return (n - 1 - i, 0)

    const2 = lambda i: (0, 0)
    tab_specs = [pl.BlockSpec(t.shape, const2) for t in tabs]
    return pl.pallas_call(
        _ret_kernel,
        grid=(n,),
        in_specs=[pl.BlockSpec((c, RET_QK_W), fwd), pl.BlockSpec((c, RET_QK_W), fwd),
                  pl.BlockSpec((c, RET_W), fwd), pl.BlockSpec((c, RET_W), fwd),
                  pl.BlockSpec((c, RET_QK_W), bwd), pl.BlockSpec((c, RET_QK_W), bwd),
                  pl.BlockSpec((c, RET_W), bwd), pl.BlockSpec((c, RET_W), bwd)] + tab_specs,
        out_specs=(pl.BlockSpec((c, RET_W), fwd), pl.BlockSpec((c, RET_W), bwd)),
        out_shape=(jax.ShapeDtypeStruct((t_all, RET_W), BF16), jax.ShapeDtypeStruct((t_all, RET_W), BF16)),
        scratch_shapes=[pltpu.VMEM((RET_QK_W, RET_W), F32), pltpu.VMEM((RET_QK_W, RET_W), F32)],
        compiler_params=_cparams(("arbitrary",)),
        name="retention",
    )(rq, rk, rv, gf, rq, rk, rv, gb, *tabs)


ATT_SUB = 128
VT_ROWS = HEAD_PAD


def _attn_kernel(q_ref, k_ref, vt_ref, o_ref, s0_scr, s1_scr, p0_scr, p1_scr, acc_scr):
    q = q_ref[...]
    tq = q.shape[0]
    nkc, tk, _ = k_ref.shape
    nsub = tk // ATT_SUB

    s_bufs = (s0_scr, s1_scr)
    p_bufs = (p0_scr, p1_scr)

    def pv(c, par, alpha):
        part = jnp.dot(vt_ref[c], p_bufs[par][...], preferred_element_type=F32)
        acc_scr[...] = alpha * acc_scr[...] + part

    def step(c, par, m_old, m_blk, alpha_prev, with_scores, with_pv):
        if with_pv:
            pv(c - 1, 1 - par, alpha_prev)
        m_new = jnp.maximum(m_old, m_blk)
        alpha = jnp.exp2(m_old - m_new)
        mx = jnp.full((8, tq), -jnp.inf, F32)
        for j in range(nsub):
            rows = pl.ds(j * ATT_SUB, ATT_SUB)
            p_bufs[par][rows, :] = jnp.exp2(s_bufs[par][rows, :] - m_new).astype(BF16)
            if with_scores:
                mx = jnp.maximum(mx, score_rows(c + 1, 1 - par, rows))
        return m_new, jnp.max(mx, axis=0, keepdims=True), alpha

    def score_rows(c, par, rows):
        s = jnp.dot(k_ref[c, rows, :], qt, preferred_element_type=F32)
        s_bufs[par][rows, :] = s
        return jnp.max(s.reshape(ATT_SUB // 8, 8, tq), axis=0)

    def scores(c, par):
        mx = jnp.full((8, tq), -jnp.inf, F32)
        for j in range(nsub):
            mx = jnp.maximum(mx, score_rows(c, par, pl.ds(j * ATT_SUB, ATT_SUB)))
        return jnp.max(mx, axis=0, keepdims=True)

    acc_scr[...] = jnp.zeros_like(acc_scr)
    qt = q.astype(F32).T.astype(BF16)
    m = jnp.full((1, tq), -jnp.inf, F32)
    m_blk = scores(0, 0)
    alpha = jnp.ones((1, tq), F32)
    if nkc > 1:
        m, m_blk, alpha = step(0, 0, m, m_blk, alpha, True, False)
        def body(c, carry):
            return lax.cond(c % 2 == 1,
                            lambda cr: step(c, 1, *cr, True, True),
                            lambda cr: step(c, 0, *cr, True, True), carry)

        m, m_blk, alpha = lax.fori_loop(1, nkc - 1, body, (m, m_blk, alpha))
    last = nkc - 1
    m, _, alpha = step(last, last % 2, m, m_blk, alpha, False, nkc > 1)
    pv(last, last % 2, alpha)
    acc = acc_scr[...]
    out_t = acc / acc[MLA_D_V:MLA_D_V + 1, :]
    if VT_ROWS < HEAD_PAD:
        out_t = jnp.concatenate([out_t, jnp.zeros((HEAD_PAD - VT_ROWS, tq), F32)], axis=0)
    o_ref[...] = out_t.T.astype(BF16)


def _attn_call(q, k, v, tq, tk, n_q_blocks):
    nh = q.shape[0]
    t_k = k.shape[1]
    nkc = t_k // tk
    k4 = k.reshape(nh, nkc, tk, HEAD_PAD)
    vt4 = jnp.swapaxes(v[:, :, :VT_ROWS].reshape(nh, nkc, tk, VT_ROWS), 2, 3)
    return pl.pallas_call(
        _attn_kernel,
        grid=(nh, n_q_blocks),
        in_specs=[pl.BlockSpec((None, tq, HEAD_PAD), lambda h, j: (h, j, 0)),
                  pl.BlockSpec((None, nkc, tk, HEAD_PAD), lambda h, j: (h, 0, 0, 0)),
                  pl.BlockSpec((None, nkc, VT_ROWS, tk), lambda h, j: (h, 0, 0, 0))],
        out_specs=pl.BlockSpec((tq, HEAD_PAD), lambda h, j: (j, h)),
        out_shape=jax.ShapeDtypeStruct((n_q_blocks * tq, nh * HEAD_PAD), BF16),
        scratch_shapes=[pltpu.VMEM((tk, tq), F32), pltpu.VMEM((tk, tq), F32),
                        pltpu.VMEM((tk, tq), BF16), pltpu.VMEM((tk, tq), BF16), pltpu.VMEM((VT_ROWS, tq), F32)],
        compiler_params=_cparams(("arbitrary", "arbitrary")),
        name="attention",
    )(q, k4, vt4)


def _top2_sum(a, b, c, d):
    hi1, lo1 = jnp.maximum(a, b), jnp.minimum(a, b)
    hi2, lo2 = jnp.maximum(c, d), jnp.minimum(c, d)
    return jnp.maximum(hi1, hi2) + jnp.maximum(jnp.minimum(hi1, hi2), jnp.maximum(lo1, lo2))


def _gates_t(aff, sel, gt_ref, grp_ref):
    rows = [sel[e:e + 1, :] for e in range(N_EXPERTS)]
    g_score = [_top2_sum(*rows[g * EXPERTS_PER_GROUP:(g + 1) * EXPERTS_PER_GROUP]) for g in range(N_GROUPS)]
    best = g_score[0]
    best_g = jnp.zeros_like(best, dtype=jnp.int32)
    for g in range(1, N_GROUPS):
        better = g_score[g] > best
        best = jnp.where(better, g_score[g], best)
        best_g = jnp.where(better, g, best_g)
    picked = []
    for e in range(N_EXPERTS):
        g = e // EXPERTS_PER_GROUP
        rank = jnp.zeros_like(best_g)
        for o in range(g * EXPERTS_PER_GROUP, (g + 1) * EXPERTS_PER_GROUP):
            if o == e:
                continue
            ahead = (rows[o] >= rows[e]) if o < e else (rows[o] > rows[e])
            rank = rank + jnp.where(ahead, 1, 0)
        picked.append(jnp.where(best_g == g, rank, 2) < 2)
    w = [jnp.where(picked[e], aff[e:e + 1, :], 0.0) for e in range(N_EXPERTS)]
    total = w[0]
    for e in range(1, N_EXPERTS):
        total = total + w[e]
    for e in range(N_EXPERTS):
        gt_ref[e:e + 1, :] = w[e] / total
    grp_ref[...] = best_g


OUT_HALF = D_MODEL // 2


def _outproj_kernel(x_ref, xc_ref, conv_ref, of_ref, ob_ref, att_ref, attc_ref, w_ref, mod_ref, nw_ref, rw_ref,
                    rb_ref, x1_ref, h2_ref, gt_ref, grp_ref, *, n_lat_tiles):
    is_lat = pl.program_id(0) < n_lat_tiles
    ret = (of_ref[...].astype(F32) + ob_ref[...].astype(F32)).astype(BF16)
    att = jnp.where(is_lat, att_ref[...], attc_ref[...])
    mix = jnp.concatenate([conv_ref[...], ret, att], axis=1)
    halves = []
    ssq = 0.0
    for hf in range(2):
        cols = slice(hf * OUT_HALF, (hf + 1) * OUT_HALF)
        o = jnp.dot(mix, w_ref[:, cols], preferred_element_type=F32)
        x1 = jnp.where(is_lat, x_ref[:, cols], xc_ref[:, cols]) + mod_ref[2:3, cols] * o
        x1_ref[:, cols] = x1
        ssq = ssq + jnp.sum(x1 * x1, axis=-1, keepdims=True)
        halves.append(x1)
    inv = lax.rsqrt(ssq * (1.0 / D_MODEL) + EPS)
    logit_parts = 0.0
    for hf in range(2):
        cols = slice(hf * OUT_HALF, (hf + 1) * OUT_HALF)
        h2 = (halves[hf] * inv * nw_ref[:, cols]) * (1.0 + mod_ref[4:5, cols]) + mod_ref[3:4, cols]
        hi = h2.astype(BF16)
        h2_ref[:, cols] = hi
        lo = (h2 - hi.astype(F32)).astype(BF16)
        logit_parts = (logit_parts + jnp.dot(hi, rw_ref[cols, :], preferred_element_type=F32)
                       + jnp.dot(lo, rw_ref[cols, :], preferred_element_type=F32))
    lt = logit_parts.T
    logits = lt[0:N_EXPERTS, :] + lt[N_EXPERTS:2 * N_EXPERTS, :]
    aff = jax.nn.sigmoid(logits)
    _gates_t(aff, aff + rb_ref[...], gt_ref, grp_ref)


def _outproj_call(x_lat, x_ctx, ctx_block0, conv, of, ob, att, att_c, w_ext, mod, nw, rw3, rb, n_lat_tiles,
                  n_tiles):
    d = x_lat.shape[1]
    tm = ROW_TILE
    row = lambda i: (i, 0)
    const2 = lambda i: (0, 0)
    lat = lambda i: (jnp.minimum(i, n_lat_tiles - 1), 0)
    cls = lambda i: (jnp.where(i < n_lat_tiles, 1, 0), 0, 0)
    return pl.pallas_call(
        functools.partial(_outproj_kernel, n_lat_tiles=n_lat_tiles),
        grid=(n_tiles,),
        in_specs=[pl.BlockSpec((tm, d), lat),
                  pl.BlockSpec((tm, d), lambda i: (ctx_block0 + jnp.maximum(i - n_lat_tiles, 0), 0)),
                  pl.BlockSpec((tm, CONV_W), row),
                  pl.BlockSpec((tm, RET_W), row),
                  pl.BlockSpec((tm, RET_W), row),
                  pl.BlockSpec((tm, ATT_W), lat),
                  pl.BlockSpec((tm, ATT_W), lambda i: (jnp.maximum(i - n_lat_tiles, 0), 0)),
                  pl.BlockSpec(w_ext.shape, const2),
                  pl.BlockSpec((None, 8, d), cls),
                  pl.BlockSpec((1, d), const2),
                  pl.BlockSpec((d, LANES), const2),
                  pl.BlockSpec((N_EXPERTS, 1), const2)],
        out_specs=(pl.BlockSpec((tm, d), row), pl.BlockSpec((tm, d), row),
                   pl.BlockSpec((N_EXPERTS, tm), lambda i: (0, i)), pl.BlockSpec((1, tm), lambda i: (0, i))),
        out_shape=(jax.ShapeDtypeStruct((n_tiles * tm, d), F32), jax.ShapeDtypeStruct((n_tiles * tm, d), BF16),
                   jax.ShapeDtypeStruct((N_EXPERTS, n_tiles * tm), F32),
                   jax.ShapeDtypeStruct((1, n_tiles * tm), jnp.int32)),
        compiler_params=_cparams(("arbitrary",)),
        name="outproj",
    )(x_lat, x_ctx, conv, of, ob, att, att_c, w_ext, mod, nw, rw3, rb)


MOE_DENSE_ROWS = 256


def _group_mlp(xb, gate4, wgu_ref, wd_ref):
    parts = []
    for k in range(EXPERTS_PER_GROUP):
        gu = jnp.dot(xb, wgu_ref[k], preferred_element_type=F32)
        parts.append((_silu(gu[:, :D_EXPERT]) * gu[:, D_EXPERT:] * gate4[:, k:k + 1]).astype(BF16))
    return jnp.dot(jnp.concatenate(parts, axis=1), wd_ref[...], preferred_element_type=F32)


def _moe_kernel(h_ref, gate8_ref, grp_ref, x1_ref, g2_ref, tri_ref, wgu_ref, wd_ref, *rest, caps, final_norm):
    rest = list(rest)
    fw_ref = rest.pop(0) if final_norm else None
    o_ref, cnt_scr = rest
    tm = h_ref.shape[0]
    grp = grp_ref[...]
    o_ref[...] = jnp.zeros_like(o_ref)
    rows8 = lax.broadcasted_iota(jnp.int32, (SUBLANES, tm), 0)
    member8 = jnp.where(rows8 == grp, 1.0, 0.0).astype(BF16)
    cnt_scr[...] = jnp.dot(member8, tri_ref[...], preferred_element_type=F32)

    def gates_of(g8):
        return g8[:, :EXPERTS_PER_GROUP] + g8[:, EXPERTS_PER_GROUP:]

    def group(g, carry):
        incl = cnt_scr[pl.ds(g, 1), :]
        count = jnp.max(incl)
        gate8_g, wgu_g, wd_g = gate8_ref.at[g], wgu_ref.at[g], wd_ref.at[g]

        def compacted(cap):
            pos = jnp.where(grp == g, incl.astype(jnp.int32) - 1, -1)
            slot = lax.broadcasted_iota(jnp.int32, (cap, tm), 0)
            onehot = jnp.where(slot == pos, 1.0, 0.0).astype(BF16)
            xg = jnp.dot(onehot, h_ref[...], preferred_element_type=F32).astype(BF16)
            g8 = jnp.dot(onehot, gate8_g[...], preferred_element_type=F32)
            y = _group_mlp(xg, gates_of(g8), wgu_g, wd_g)
            o_ref[...] += lax.dot_general(onehot, y.astype(BF16), (((0,), (0,)), ((), ())),
                                          preferred_element_type=F32)

        lower = 0
        for cap in caps:
            pl.when((count > lower) & (count <= cap))(functools.partial(compacted, cap))
            lower = cap

        @pl.when(count > caps[-1])
        def _():
            def chunk(ci, inner):
                rows = pl.ds(pl.multiple_of(ci * MOE_DENSE_ROWS, MOE_DENSE_ROWS), MOE_DENSE_ROWS)
                o_ref[rows, :] += _group_mlp(h_ref[rows, :], gates_of(gate8_g[rows, :].astype(F32)), wgu_g, wd_g)
                return inner

            lax.fori_loop(0, tm // MOE_DENSE_ROWS, chunk, 0)

        return carry

    lax.fori_loop(0, N_GROUPS, group, 0)
    x2 = x1_ref[...] + g2_ref[...] * o_ref[...]
    if final_norm:
        x2 = x2 * lax.rsqrt(jnp.mean(x2 * x2, axis=-1, keepdims=True) + EPS) * fw_ref[...]
    o_ref[...] = x2


def _moe_call(h2, gate8, grp, x1, g2, wgu, wd, tm, block0, n_tiles, final_w=None):
    out_rows, d = x1.shape
    caps = (tm // 4, 3 * tm // 8, tm // 2)
    assert all(cap % 16 == 0 for cap in caps) and tm % MOE_DENSE_ROWS == 0
    tri = jnp.asarray(np.triu(np.ones((tm, tm), np.float32)), dtype=BF16)
    rows = lambda i: (block0 + i, 0)
    const2 = lambda i: (0, 0)
    const3 = lambda i: (0, 0, 0)
    once = pl.Buffered(1)
    in_specs = [pl.BlockSpec((tm, d), rows),
                pl.BlockSpec((N_GROUPS, tm, 2 * EXPERTS_PER_GROUP), lambda i: (0, block0 + i, 0)),
                pl.BlockSpec((1, tm), lambda i: (0, block0 + i)),
                pl.BlockSpec((tm, d), rows),
                pl.BlockSpec((1, d), const2),
                pl.BlockSpec((tm, tm), const2, pipeline_mode=once),
                pl.BlockSpec(wgu.shape, lambda i: (0, 0, 0, 0), pipeline_mode=once),
                pl.BlockSpec(wd.shape, const3, pipeline_mode=once)]
    args = [h2, gate8, grp, x1, g2, tri, wgu, wd]
    if final_w is not None:
        in_specs.append(pl.BlockSpec((1, d), const2))
        args.append(final_w)
    return pl.pallas_call(
        functools.partial(_moe_kernel, caps=caps, final_norm=final_w is not None),
        grid=(n_tiles,),
        in_specs=in_specs,
        out_specs=pl.BlockSpec((tm, d), rows),
        out_shape=jax.ShapeDtypeStruct((out_rows, d), F32),
        scratch_shapes=[pltpu.VMEM((SUBLANES, tm), F32)],
        input_output_aliases={3: 0},
        compiler_params=_cparams(("arbitrary",)),
        name="experts",
    )(*args)


_SWAP32 = np.concatenate([np.arange(8, 16), np.arange(0, 8), np.arange(24, 32), np.arange(16, 24)])


def _pad_cols(w, width):
    return jnp.pad(w, ((0, 0), (0, width - w.shape[1])))


def _in_weight(w_in):
    sizes = (2 * CONV_W, RET_QK_W, RET_QK_W, RET_W, RET_W, RET_W, MLA_Q_RANK, MLA_KV_RANK, MLA_D_ROPE)
    offs = np.concatenate([[0], np.cumsum(sizes)])
    conv, rq, rk, rv, gf, gb, cq, ckv, kr = [w_in[:, offs[i]:offs[i + 1]] for i in range(len(sizes))]
    swap192 = np.concatenate([h * ROPE_DIM + _SWAP32 for h in range(RET_HEADS)])
    d = w_in.shape[0]

    def place_rope(w):
        return jnp.concatenate([jnp.zeros((d, MLA_D_NOPE), w.dtype), w,
                                jnp.zeros((d, LANES - MLA_D_NOPE - MLA_D_ROPE), w.dtype)], axis=1)

    ext = jnp.concatenate([
        conv, _pad_cols(rq, 256), _pad_cols(rq[:, swap192], 256), _pad_cols(rk, 256), _pad_cols(rk[:, swap192], 256),
        rv, gf, gb, _pad_cols(cq, 256), ckv, place_rope(kr), place_rope(kr[:, _SWAP32])], axis=1)
    assert ext.shape[1] == IN_EXT
    return ext.astype(BF16)


def _uq_weight(w_uq):
    r = w_uq.shape[0]
    w = w_uq.reshape(r, MLA_HEADS, MLA_D_NOPE + MLA_D_ROPE)
    nope, rope = w[..., :MLA_D_NOPE], w[..., MLA_D_NOPE:]
    zpad = jnp.zeros((r, MLA_HEADS, HEAD_PAD - MLA_D_NOPE - MLA_D_ROPE), w.dtype)
    main = jnp.concatenate([nope, rope, zpad], axis=-1).reshape(r, ATT_W)
    part = jnp.concatenate([jnp.zeros_like(nope), rope[..., _SWAP32], zpad], axis=-1).reshape(r, ATT_W)
    return jnp.concatenate([main, part], axis=1).astype(BF16)


def _ukv_weight(w_ukv):
    r = w_ukv.shape[0]
    w = w_ukv.reshape(r, MLA_HEADS, MLA_D_NOPE + MLA_D_V)
    zpad = jnp.zeros((r, MLA_HEADS, HEAD_PAD - MLA_D_NOPE), w.dtype)
    kpart = jnp.concatenate([w[..., :MLA_D_NOPE], zpad], axis=-1).reshape(r, ATT_W)
    vpart = jnp.concatenate([w[..., MLA_D_NOPE:], zpad], axis=-1).reshape(r, ATT_W)
    return jnp.concatenate([kpart, vpart], axis=1).astype(BF16)


def _out_weight(w_out):
    d = w_out.shape[1]
    conv, ret = w_out[:CONV_W], w_out[CONV_W:CONV_W + RET_W]
    att = w_out[CONV_W + RET_W:].reshape(MLA_HEADS, MLA_D_V, d)
    att = jnp.pad(att, ((0, 0), (0, HEAD_PAD - MLA_D_V), (0, 0))).reshape(ATT_W, d)
    return jnp.concatenate([conv, ret, att], axis=0).astype(BF16)


def _rope_tables(n_ctx, seq):
    f32 = np.float32
    inv = (f32(ROPE_BASE) ** (-np.arange(ROPE_PAIRS, dtype=f32) / f32(ROPE_PAIRS))).astype(f32)
    ar = (np.arange(seq // GRID_W, dtype=f32)[:, None] * inv).astype(f32)
    ac = (np.arange(GRID_W, dtype=f32)[:, None] * inv).astype(f32)
    row_cs = np.concatenate([np.cos(ar), np.cos(ar)], axis=1).astype(f32)
    row_sn = np.concatenate([-np.sin(ar), np.sin(ar)], axis=1).astype(f32)
    col_cs = np.concatenate([np.cos(ac), np.cos(ac)], axis=1).astype(f32)
    col_sn = np.concatenate([-np.sin(ac), np.sin(ac)], axis=1).astype(f32)

    def expand(row_t, col_t, ctx_value):
        rows = seq // GRID_W
        lat = jnp.concatenate([jnp.broadcast_to(jnp.asarray(row_t)[:, None, :], (rows, GRID_W, 16)),
                               jnp.broadcast_to(jnp.asarray(col_t)[None, :, :], (rows, GRID_W, 16))],
                              axis=-1).reshape(seq, ROPE_DIM)
        full = jnp.concatenate([lat, jnp.full((n_ctx, ROPE_DIM), ctx_value, F32)], axis=0)
        return jnp.tile(full, (1, LANES // ROPE_DIM))

    return expand(row_cs, col_cs, 1.0), expand(row_sn, col_sn, 0.0)


def _router_weight(router_w):
    hi = router_w.astype(BF16)
    lo = (router_w - hi.astype(F32)).astype(BF16)
    return jnp.pad(jnp.concatenate([hi, lo], axis=1), ((0, 0), (0, LANES - 2 * N_EXPERTS)))


def _group_gates(gates_t):
    t = gates_t.shape[1]
    g4 = jnp.transpose(gates_t.reshape(N_GROUPS, EXPERTS_PER_GROUP, t), (0, 2, 1))
    hi = g4.astype(BF16)
    lo = (g4 - hi.astype(F32)).astype(BF16)
    return jnp.concatenate([hi, lo], axis=-1)


def _group_weights(w_gate, w_up, w_down):
    e, d, f = w_gate.shape
    gu = jnp.concatenate([w_gate, w_up], axis=-1).astype(BF16)
    gu = gu.reshape(N_GROUPS, EXPERTS_PER_GROUP, d, 2 * f)
    wd = w_down.astype(BF16).reshape(N_GROUPS, EXPERTS_PER_GROUP * f, d)
    return gu, wd


def kernel(x, c, ctx, c_ctx, ada_w, ada_b, norm1_w, norm2_w, w_in, conv_dw, conv_b, conv_ln_w, conv_ln_b,
           mla_q_norm_w, mla_w_uq, mla_kv_norm_w, mla_w_ukv, w_out, router_w, router_bias,
           moe_w_gate, moe_w_up, moe_w_down, final_norm_w):
    batch, seq, d = x.shape
    n_ctx = ctx.shape[1]
    depth = ada_w.shape[0]
    t_all = seq + n_ctx
    assert batch == 1 and d == D_MODEL
    assert n_ctx % ROW_TILE == 0 and seq % ROW_TILE == 0 and n_ctx <= MOE_TILE and seq % MOE_TILE == 0
    assert seq % ATT_TQ == 0 and t_all % ATT_TK == 0 and n_ctx % RET_CHUNK == 0 and seq % n_ctx == 0
    n_lat_tiles = seq // ROW_TILE
    n_all_tiles = t_all // ROW_TILE

    cond = jnp.zeros((8, d), F32).at[0].set(c_ctx).at[1].set(c[0])
    mod = _ada_call(cond, ada_w, ada_b)[:, :2, :].reshape(depth, 2, 6, d)
    mod8 = jnp.pad(mod, ((0, 0), (0, 0), (0, 2), (0, 0)))
    cs, sn = _rope_tables(n_ctx, seq)
    ret_tabs = _ret_tables()
    rw3 = _router_weight(router_w)
    rb = router_bias.reshape(N_EXPERTS, 1)

    x_lat, x_ctx, ctx_block0 = x[0], ctx[0], 0
    for l in range(depth):
        last = l == depth - 1
        u, rq, rk, rv, gf, gb, q_att, k_att, v_att = _inproj_call(
            x_lat, x_ctx, ctx_block0, t_all, norm1_w[l][None, :], mod8[l][:, :2, :], _in_weight(w_in[l]), cs, sn,
            mla_q_norm_w[l][None, :], _uq_weight(mla_w_uq[l]), mla_kv_norm_w[l][None, :], _ukv_weight(mla_w_ukv[l]),
            n_lat_tiles)
        conv = _conv_call(u, conv_dw[l], conv_b[l][None, :], conv_ln_w[l][None, :], conv_ln_b[l][None, :],
                          n_lat_tiles)
        of, ob = _ret_call(rq, rk, rv, gf, gb, ret_tabs, seq)
        att = _attn_call(q_att, k_att, v_att, ATT_TQ, ATT_TK, seq // ATT_TQ)
        att_c = att
        if not last:
            tqc = min(ATT_TQ, n_ctx)
            att_c = _attn_call(q_att[:, seq:], k_att[:, seq:], v_att[:, seq:], tqc, n_ctx, n_ctx // tqc)
        n_tiles = n_lat_tiles if last else n_all_tiles
        x1, h2, gates_t, grp = _outproj_call(x_lat, x_ctx, ctx_block0, conv, of, ob, att, att_c,
                                             _out_weight(w_out[l]), mod8[l], norm2_w[l][None, :], rw3, rb,
                                             n_lat_tiles, n_tiles)
        gate8 = _group_gates(gates_t)
        wgu, wd = _group_weights(moe_w_gate[l], moe_w_up[l], moe_w_down[l])
        x_all = _moe_call(h2, gate8, grp, x1, mod[l, 1, 5][None, :], wgu, wd, MOE_TILE, 0, seq // MOE_TILE,
                          final_w=final_norm_w[None, :] if last else None)
        if not last:
            x_all = _moe_call(h2, gate8, grp, x_all, mod[l, 0, 5][None, :], wgu, wd, n_ctx, seq // n_ctx, 1)
        x_lat, x_ctx, ctx_block0 = x_all, x_all, n_lat_tiles
    return x_all[None]
```

```python
import functools
import math

import numpy as np
import jax
import jax.numpy as jnp
from jax import lax
from jax.experimental import pallas as pl
from jax.experimental.pallas import tpu as pltpu

F32 = jnp.float32
BF16 = jnp.bfloat16

D_MODEL = 1024
GRID_W = 64
CONV_W = 256
CONV_K = 31
RET_HEADS = 6
RET_DK = 32
RET_DV = 64
RET_QK_W = RET_HEADS * RET_DK
RET_W = RET_HEADS * RET_DV
RET_CHUNK = 128
RET_PER_STEP = 2
RET_GN_EPS = 1e-5
MLA_HEADS = 6
MLA_Q_RANK = 192
MLA_KV_RANK = 128
MLA_D_NOPE = 64
MLA_D_ROPE = 32
MLA_D_V = 64
MLA_SCALE = (MLA_D_NOPE + MLA_D_ROPE) ** -0.5
ROPE_DIM = 32
ROPE_PAIRS = ROPE_DIM // 4
ROPE_BASE = 10000.0
N_EXPERTS = 16
N_GROUPS = 4
EXPERTS_PER_GROUP = N_EXPERTS // N_GROUPS
D_EXPERT = 256
EPS = 1e-6

LANES = 128
HEAD_PAD = LANES
ATT_W = MLA_HEADS * HEAD_PAD
LOG2E = math.log2(math.e)

C_CONV = 0
C_RQ = 512
C_RK = 768
C_RV = 1024
C_RGF = 1408
C_RGB = 1792
C_CQ = 2176
C_CKV = 2432
C_KR = 2560
IN_EXT = 2688

ROW_TILE = 256
MOE_TILE = 512
ATT_TQ = 1024
ATT_TK = 1280
VMEM_LIMIT = 48 * 1024 * 1024


def _cparams(sem):
    return pltpu.CompilerParams(dimension_semantics=sem, vmem_limit_bytes=VMEM_LIMIT)


def _silu(x):
    return x * jax.nn.sigmoid(x)


def _ada_kernel(c_ref, w_ref, b_ref, o_ref):
    s = _silu(c_ref[...])
    o_ref[...] = jnp.dot(s, w_ref[...], preferred_element_type=F32,
                         precision=lax.Precision.HIGHEST) + b_ref[...]


def _ada_call(cond, ada_w, ada_b):
    depth, d, n = ada_w.shape
    tn = 1536
    return pl.pallas_call(
        _ada_kernel,
        grid=(depth, n // tn),
        in_specs=[pl.BlockSpec((8, d), lambda l, j: (0, 0)),
                  pl.BlockSpec((None, d, tn), lambda l, j: (l, 0, j)),
                  pl.BlockSpec((None, 1, tn), lambda l, j: (l, 0, j))],
        out_specs=pl.BlockSpec((None, 8, tn), lambda l, j: (l, 0, j)),
        out_shape=jax.ShapeDtypeStruct((depth, 8, n), F32),
        compiler_params=_cparams(("arbitrary", "arbitrary")),
        name="adaln",
    )(cond, ada_w, ada_b.reshape(depth, 1, n))


def _rot_partner(x):
    n = x.shape[-1]
    lane = lax.broadcasted_iota(jnp.int32, (1, n), 1)
    return jnp.where(lane % (2 * ROPE_PAIRS) < ROPE_PAIRS, pltpu.roll(x, n - ROPE_PAIRS, axis=1),
                     pltpu.roll(x, ROPE_PAIRS, axis=1))


def _inproj_kernel(x_ref, xc_ref, nw_ref, mod_ref, w_ref, cs_ref, sn_ref, qnw_ref, wuq_ref, kvnw_ref, wukv_ref,
                   u_ref, rq_ref, rk_ref, rv_ref, gf_ref, gb_ref, q_ref, k_ref, v_ref, *, n_lat_tiles):
    x = jnp.where(pl.program_id(0) < n_lat_tiles, x_ref[...], xc_ref[...])
    y = x * lax.rsqrt(jnp.mean(x * x, axis=-1, keepdims=True) + EPS) * nw_ref[...]
    h = (y * (1.0 + mod_ref[1:2, :]) + mod_ref[0:1, :]).astype(BF16)

    def proj(c0, width):
        return jnp.dot(h, w_ref[:, c0:c0 + width], preferred_element_type=F32)

    cs = cs_ref[...]
    sn = sn_ref[...]
    lane = lax.broadcasted_iota(jnp.int32, (1, LANES), 1)
    rope_lanes = (lane >= MLA_D_NOPE) & (lane < MLA_D_NOPE + MLA_D_ROPE)
    csq = jnp.where(rope_lanes, cs, 1.0)
    snq = jnp.where(rope_lanes, sn, 0.0)

    ag = proj(C_CONV, 2 * CONV_W)
    u_ref[...] = ag[:, :CONV_W] * jax.nn.sigmoid(ag[:, CONV_W:])

    cs192 = jnp.concatenate([cs, cs[:, :RET_QK_W - LANES]], axis=1)
    sn192 = jnp.concatenate([sn, sn[:, :RET_QK_W - LANES]], axis=1)
    rq = proj(C_RQ, 256)
    rq = rq[:, :RET_QK_W] * cs192 + _rot_partner(rq)[:, :RET_QK_W] * sn192
    rq_ref[...] = rq.astype(BF16)
    rk = proj(C_RK, 256)
    rk = rk[:, :RET_QK_W] * cs192 + _rot_partner(rk)[:, :RET_QK_W] * sn192
    rk_ref[...] = (rk * (RET_DK ** -0.5)).astype(BF16)
    rv_ref[...] = proj(C_RV, RET_W).astype(BF16)
    gf_ref[...] = _silu(proj(C_RGF, RET_W))
    gb_ref[...] = _silu(proj(C_RGB, RET_W))

    cq = proj(C_CQ, 256)[:, :MLA_Q_RANK]
    cqn = (cq * lax.rsqrt(jnp.mean(cq * cq, axis=-1, keepdims=True) + EPS) * qnw_ref[...]).astype(BF16)
    qa = jnp.dot(cqn, wuq_ref[:, :ATT_W], preferred_element_type=F32)
    qb = jnp.dot(cqn, wuq_ref[:, ATT_W:], preferred_element_type=F32)
    ckv = proj(C_CKV, MLA_KV_RANK)
    ckvn = (ckv * lax.rsqrt(jnp.mean(ckv * ckv, axis=-1, keepdims=True) + EPS) * kvnw_ref[...]).astype(BF16)
    ka = jnp.dot(ckvn, wukv_ref[:, :ATT_W], preferred_element_type=F32)
    va = jnp.dot(ckvn, wukv_ref[:, ATT_W:], preferred_element_type=F32)
    kr = proj(C_KR, LANES)
    kr = kr * csq + _rot_partner(kr) * snq
    ones_lane = (lane == MLA_D_V).astype(F32)
    for hd in range(MLA_HEADS):
        sl = slice(hd * HEAD_PAD, (hd + 1) * HEAD_PAD)
        q_ref[hd] = ((qa[:, sl] * csq + qb[:, sl] * snq) * (MLA_SCALE * LOG2E)).astype(BF16)
        k_ref[hd] = (ka[:, sl] + kr).astype(BF16)
        v_ref[hd] = (va[:, sl] + ones_lane).astype(BF16)


def _inproj_call(x_lat, x_ctx, ctx_block0, t_all, nw, mod, w_ext, cs, sn, qnw, wuq, kvnw, wukv, n_lat_tiles):
    d = x_lat.shape[1]
    tm = ROW_TILE
    row = lambda i: (i, 0)
    const2 = lambda i: (0, 0)
    head = lambda i: (0, i, 0)
    cls = lambda i: (jnp.where(i < n_lat_tiles, 1, 0), 0, 0)
    out_shape = (
        jax.ShapeDtypeStruct((t_all, CONV_W), F32),
        jax.ShapeDtypeStruct((t_all, RET_QK_W), BF16),
        jax.ShapeDtypeStruct((t_all, RET_QK_W), BF16),
        jax.ShapeDtypeStruct((t_all, RET_W), BF16),
        jax.ShapeDtypeStruct((t_all, RET_W), F32),
        jax.ShapeDtypeStruct((t_all, RET_W), F32),
        jax.ShapeDtypeStruct((MLA_HEADS, t_all, HEAD_PAD), BF16),
        jax.ShapeDtypeStruct((MLA_HEADS, t_all, HEAD_PAD), BF16),
        jax.ShapeDtypeStruct((MLA_HEADS, t_all, HEAD_PAD), BF16),
    )
    out_specs = (
        pl.BlockSpec((tm, CONV_W), row),
        pl.BlockSpec((tm, RET_QK_W), row),
        pl.BlockSpec((tm, RET_QK_W), row),
        pl.BlockSpec((tm, RET_W), row),
        pl.BlockSpec((tm, RET_W), row),
        pl.BlockSpec((tm, RET_W), row),
        pl.BlockSpec((MLA_HEADS, tm, HEAD_PAD), head),
        pl.BlockSpec((MLA_HEADS, tm, HEAD_PAD), head),
        pl.BlockSpec((MLA_HEADS, tm, HEAD_PAD), head),
    )
    return pl.pallas_call(
        functools.partial(_inproj_kernel, n_lat_tiles=n_lat_tiles),
        grid=(t_all // tm,),
        in_specs=[pl.BlockSpec((tm, d), lambda i: (jnp.minimum(i, n_lat_tiles - 1), 0)),
                  pl.BlockSpec((tm, d), lambda i: (ctx_block0 + jnp.maximum(i - n_lat_tiles, 0), 0)),
                  pl.BlockSpec((1, d), const2),
                  pl.BlockSpec((None, 2, d), cls),
                  pl.BlockSpec((d, IN_EXT), const2),
                  pl.BlockSpec((tm, LANES), row),
                  pl.BlockSpec((tm, LANES), row),
                  pl.BlockSpec((1, MLA_Q_RANK), const2),
                  pl.BlockSpec((MLA_Q_RANK, 2 * ATT_W), const2),
                  pl.BlockSpec((1, MLA_KV_RANK), const2),
                  pl.BlockSpec((MLA_KV_RANK, 2 * ATT_W), const2)],
        out_specs=out_specs,
        out_shape=out_shape,
        compiler_params=_cparams(("arbitrary",)),
        name="inproj",
    )(x_lat, x_ctx, nw, mod, w_ext, cs, sn, qnw, wuq, kvnw, wukv)


CONV_HALO = 16
CONV_SUB = 64
SUBLANES = 8


def _conv_kernel(prev_ref, cur_ref, next_ref, dw_ref, b_ref, lnw_ref, lnb_ref, o_ref, ext_ref, sh_ref, *,
                 n_lat_tiles):
    i = pl.program_id(0)
    n = pl.num_programs(0)
    tm = cur_ref.shape[0]
    seq_start = (i == 0) | (i == n_lat_tiles)
    seq_end = (i == n_lat_tiles - 1) | (i == n - 1)
    ext_ref[0:CONV_HALO, :] = jnp.where(seq_start, 0.0, prev_ref[...])
    ext_ref[CONV_HALO:CONV_HALO + tm, :] = cur_ref[...]
    ext_ref[CONV_HALO + tm:, :] = jnp.where(seq_end, 0.0, next_ref[...])
    span = sh_ref.shape[1]
    for ph in range(SUBLANES):
        sh_ref[ph] = ext_ref[pl.ds(ph, span), :]
    base = CONV_HALO - CONV_K // 2
    for r in range(tm // CONV_SUB):
        acc = jnp.zeros((CONV_SUB, CONV_W), F32) + b_ref[...]
        for k in range(CONV_K):
            off = base + k
            acc = acc + (sh_ref[off % SUBLANES, pl.ds(r * CONV_SUB + off - off % SUBLANES, CONV_SUB), :]
                         * dw_ref[k:k + 1, :])
        mu = jnp.mean(acc, axis=-1, keepdims=True)
        dlt = acc - mu
        var = jnp.mean(dlt * dlt, axis=-1, keepdims=True)
        y = dlt * lax.rsqrt(var + EPS) * lnw_ref[...] + lnb_ref[...]
        o_ref[r * CONV_SUB:(r + 1) * CONV_SUB, :] = _silu(y).astype(BF16)


def _conv_call(u, dw, b, lnw, lnb, n_lat_tiles):
    t_all = u.shape[0]
    tm = ROW_TILE
    hpt = tm // CONV_HALO
    n_halo = t_all // CONV_HALO
    const2 = lambda i: (0, 0)
    return pl.pallas_call(
        functools.partial(_conv_kernel, n_lat_tiles=n_lat_tiles),
        grid=(t_all // tm,),
        in_specs=[pl.BlockSpec((CONV_HALO, CONV_W), lambda i: (jnp.maximum(i * hpt - 1, 0), 0)),
                  pl.BlockSpec((tm, CONV_W), lambda i: (i, 0)),
                  pl.BlockSpec((CONV_HALO, CONV_W), lambda i: (jnp.minimum((i + 1) * hpt, n_halo - 1), 0)),
                  pl.BlockSpec((CONV_K, CONV_W), const2),
                  pl.BlockSpec((1, CONV_W), const2),
                  pl.BlockSpec((1, CONV_W), const2),
                  pl.BlockSpec((1, CONV_W), const2)],
        out_specs=pl.BlockSpec((tm, CONV_W), lambda i: (i, 0)),
        out_shape=jax.ShapeDtypeStruct((t_all, CONV_W), BF16),
        scratch_shapes=[pltpu.VMEM((tm + 2 * CONV_HALO, CONV_W), F32),
                        pltpu.VMEM((SUBLANES, tm + 2 * CONV_HALO - SUBLANES, CONV_W), F32)],
        compiler_params=_cparams(("arbitrary",)),
        name="conv",
    )(u, u, u, dw, b, lnw, lnb)


def _split_dot(x, a):
    hi = x.astype(BF16)
    lo = (x - hi.astype(F32)).astype(BF16)
    return (jnp.dot(hi, a, preferred_element_type=F32) + jnp.dot(lo, a, preferred_element_type=F32))


def _ret_direction(q, k, v, gate, r_ref, dmask, xi, zeta, gchunk, bdmask, avg, o_ref):
    lane_k = lax.broadcasted_iota(jnp.int32, (1, RET_QK_W), 1) // RET_DK
    lane_v = lax.broadcasted_iota(jnp.int32, (1, RET_W), 1) // RET_DV
    zero_k = jnp.zeros_like(k)
    zero_v = jnp.zeros_like(v)
    k_bd = jnp.concatenate([jnp.where(lane_k == hd, k, zero_k) for hd in range(RET_HEADS)], axis=0)
    v_bd = jnp.concatenate([jnp.where(lane_v == hd, v, zero_v) for hd in range(RET_HEADS)], axis=0)
    s = lax.dot_general(q, k_bd, (((1,), (1,)), ((), ())), preferred_element_type=F32)
    s = (s * dmask).astype(BF16)
    inner = jnp.dot(s, v_bd, preferred_element_type=F32)
    r = r_ref[...]
    cross = jnp.dot(q, r.astype(BF16), preferred_element_type=F32) * xi
    o = inner + cross
    kz = (k.astype(F32) * zeta).astype(BF16)
    ds = lax.dot_general(kz, v, (((0,), (0,)), ((), ())), preferred_element_type=F32)
    r_ref[...] = gchunk * r + ds * bdmask
    mu = jnp.dot(o.astype(BF16), avg, preferred_element_type=F32)
    dlt = o - mu
    var = jnp.dot((dlt * dlt).astype(BF16), avg, preferred_element_type=F32)
    o_ref[...] = (gate * (dlt * lax.rsqrt(var + RET_GN_EPS))).astype(BF16)


def _ret_kernel(qf_ref, kf_ref, vf_ref, gf_ref, qb_ref, kb_ref, vb_ref, gb_ref,
                dmf_ref, dmb_ref, xif_ref, xib_ref, ztf_ref, ztb_ref, gcf_ref, gcb_ref, bdm_ref, avg_ref,
                of_ref, ob_ref, rf_ref, rb_ref):
    @pl.when(pl.program_id(0) == 0)
    def _():
        rf_ref[...] = jnp.zeros_like(rf_ref)
        rb_ref[...] = jnp.zeros_like(rb_ref)

    bdm = bdm_ref[...]
    avg = avg_ref[...]
    c = RET_CHUNK
    for s in range(RET_PER_STEP):
        fr = pl.ds(s * c, c)
        br = pl.ds((RET_PER_STEP - 1 - s) * c, c)
        _ret_direction(qf_ref[fr, :], kf_ref[fr, :], vf_ref[fr, :], gf_ref[fr, :], rf_ref, dmf_ref[...],
                       xif_ref[...], ztf_ref[...], gcf_ref[...], bdm, avg, of_ref.at[fr, :])
        _ret_direction(qb_ref[br, :], kb_ref[br, :], vb_ref[br, :], gb_ref[br, :], rb_ref, dmb_ref[...],
                       xib_ref[...], ztb_ref[...], gcb_ref[...], bdm, avg, ob_ref.at[br, :])


def _ret_tables():
    c = RET_CHUNK
    f32 = np.float32
    gamma_f = (1.0 - 2.0 ** (-5.0 - np.arange(RET_HEADS, dtype=f32))).astype(f32)
    gamma_b = gamma_f[::-1]
    idx = np.arange(c, dtype=f32)
    diff = idx[:, None] - idx[None, :]

    def tables(gamma, reverse):
        lg = np.log(gamma).astype(f32)
        d = -diff if reverse else diff
        dm = np.where(d[None] >= 0, np.exp(np.maximum(d, 0.0)[None] * lg[:, None, None]), 0.0)
        dm = np.transpose(dm, (1, 0, 2)).reshape(c, RET_HEADS * c)
        xi_e = (c - idx) if reverse else (idx + 1.0)
        zt_e = idx if reverse else (c - 1.0 - idx)
        xi = np.repeat(np.exp(xi_e[:, None] * lg[None, :]), RET_DV, axis=1)
        zt = np.repeat(np.exp(zt_e[:, None] * lg[None, :]), RET_DK, axis=1)
        gc = np.repeat(np.exp(c * lg), RET_DV)[None, :]
        return [t.astype(f32) for t in (dm, xi, zt, gc)]

    dmf, xif, ztf, gcf = tables(gamma_f, False)
    dmb, xib, ztb, gcb = tables(gamma_b, True)
    hk = np.arange(RET_QK_W) // RET_DK
    hv = np.arange(RET_W) // RET_DV
    bdm = (hk[:, None] == hv[None, :]).astype(f32)
    avg = jnp.asarray((hv[:, None] == hv[None, :]).astype(f32) / RET_DV, dtype=BF16)
    return tuple(jnp.asarray(t) for t in (dmf, dmb, xif, xib, ztf, ztb, gcf, gcb, bdm)) + (avg,)


def _ret_call(rq, rk, rv, gf, gb, tabs, n_lat_rows):
    t_all = rq.shape[0]
    c = RET_CHUNK * RET_PER_STEP
    assert n_lat_rows % c == 0 and t_all % c == 0
    n = t_all // c
    n_lat_chunks = n_lat_rows // c
    n_ctx_chunks = n - n_lat_chunks

    def fwd(i):
        return (jnp.where(i < n_ctx_chunks, n_lat_chunks + i, i - n_ctx_chunks), 0)

    def bwd(i):
        return (n - 1 - i, 0)

    const2 = lambda i: (0, 0)
    tab_specs = [pl.BlockSpec(t.shape, const2) for t in tabs]
    return pl.pallas_call(
        _ret_kernel,
        grid=(n,),
        in_specs=[pl.BlockSpec((c, RET_QK_W), fwd), pl.BlockSpec((c, RET_QK_W), fwd),
                  pl.BlockSpec((c, RET_W), fwd), pl.BlockSpec((c, RET_W), fwd),
                  pl.BlockSpec((c, RET_QK_W), bwd), pl.BlockSpec((c, RET_QK_W), bwd),
                  pl.BlockSpec((c, RET_W), bwd), pl.BlockSpec((c, RET_W), bwd)] + tab_specs,
        out_specs=(pl.BlockSpec((c, RET_W), fwd), pl.BlockSpec((c, RET_W), bwd)),
        out_shape=(jax.ShapeDtypeStruct((t_all, RET_W), BF16), jax.ShapeDtypeStruct((t_all, RET_W), BF16)),
        scratch_shapes=[pltpu.VMEM((RET_QK_W, RET_W), F32), pltpu.VMEM((RET_QK_W, RET_W), F32)],
        compiler_params=_cparams(("arbitrary",)),
        name="retention",
    )(rq, rk, rv, gf, rq, rk, rv, gb, *tabs)


ATT_SUB = 128
VT_ROWS = HEAD_PAD


def _attn_kernel(q_ref, k_ref, vt_ref, o_ref, s0_scr, s1_scr, p0_scr, p1_scr, acc_scr):
    q = q_ref[...]
    tq = q.shape[0]
    nkc, tk, _ = k_ref.shape
    nsub = tk // ATT_SUB

    s_bufs = (s0_scr, s1_scr)
    p_bufs = (p0_scr, p1_scr)

    def pv(c, par, alpha):
        part = jnp.dot(vt_ref[c], p_bufs[par][...], preferred_element_type=F32)
        acc_scr[...] = alpha * acc_scr[...] + part

    def step(c, par, m_old, m_blk, alpha_prev, with_scores, with_pv):
        if with_pv:
            pv(c - 1, 1 - par, alpha_prev)
        m_new = jnp.maximum(m_old, m_blk)
        alpha = jnp.exp2(m_old - m_new)
        mx = jnp.full((8, tq), -jnp.inf, F32)
        for j in range(nsub):
            rows = pl.ds(j * ATT_SUB, ATT_SUB)
            p_bufs[par][rows, :] = jnp.exp2(s_bufs[par][rows, :] - m_new).astype(BF16)
            if with_scores:
                mx = jnp.maximum(mx, score_rows(c + 1, 1 - par, rows))
        return m_new, jnp.max(mx, axis=0, keepdims=True), alpha

    def score_rows(c, par, rows):
        s = jnp.dot(k_ref[c, rows, :], qt, preferred_element_type=F32)
        s_bufs[par][rows, :] = s
        return jnp.max(s.reshape(ATT_SUB // 8, 8, tq), axis=0)

    def scores(c, par):
        mx = jnp.full((8, tq), -jnp.inf, F32)
        for j in range(nsub):
            mx = jnp.maximum(mx, score_rows(c, par, pl.ds(j * ATT_SUB, ATT_SUB)))
        return jnp.max(mx, axis=0, keepdims=True)

    acc_scr[...] = jnp.zeros_like(acc_scr)
    qt = q.astype(F32).T.astype(BF16)
    m = jnp.full((1, tq), -jnp.inf, F32)
    m_blk = scores(0, 0)
    alpha = jnp.ones((1, tq), F32)
    if nkc > 1:
        m, m_blk, alpha = step(0, 0, m, m_blk, alpha, True, False)
        def body(c, carry):
            return lax.cond(c % 2 == 1,
                            lambda cr: step(c, 1, *cr, True, True),
                            lambda cr: step(c, 0, *cr, True, True), carry)

        m, m_blk, alpha = lax.fori_loop(1, nkc - 1, body, (m, m_blk, alpha))
    last = nkc - 1
    m, _, alpha = step(last, last % 2, m, m_blk, alpha, False, nkc > 1)
    pv(last, last % 2, alpha)
    acc = acc_scr[...]
    out_t = acc / acc[MLA_D_V:MLA_D_V + 1, :]
    if VT_ROWS < HEAD_PAD:
        out_t = jnp.concatenate([out_t, jnp.zeros((HEAD_PAD - VT_ROWS, tq), F32)], axis=0)
    o_ref[...] = out_t.T.astype(BF16)


def _attn_call(q, k, v, tq, tk, n_q_blocks):
    nh = q.shape[0]
    t_k = k.shape[1]
    nkc = t_k // tk
    k4 = k.reshape(nh, nkc, tk, HEAD_PAD)
    vt4 = jnp.swapaxes(v[:, :, :VT_ROWS].reshape(nh, nkc, tk, VT_ROWS), 2, 3)
    return pl.pallas_call(
        _attn_kernel,
        grid=(nh, n_q_blocks),
        in_specs=[pl.BlockSpec((None, tq, HEAD_PAD), lambda h, j: (h, j, 0)),
                  pl.BlockSpec((None, nkc, tk, HEAD_PAD), lambda h, j: (h, 0, 0, 0)),
                  pl.BlockSpec((None, nkc, VT_ROWS, tk), lambda h, j: (h, 0, 0, 0))],
        out_specs=pl.BlockSpec((tq, HEAD_PAD), lambda h, j: (j, h)),
        out_shape=jax.ShapeDtypeStruct((n_q_blocks * tq, nh * HEAD_PAD), BF16),
        scratch_shapes=[pltpu.VMEM((tk, tq), F32), pltpu.VMEM((tk, tq), F32),
                        pltpu.VMEM((tk, tq), BF16), pltpu.VMEM((tk, tq), BF16), pltpu.VMEM((VT_ROWS, tq), F32)],
        compiler_params=_cparams(("arbitrary", "arbitrary")),
        name="attention",
    )(q, k4, vt4)


def _top2_sum(a, b, c, d):
    hi1, lo1 = jnp.maximum(a, b), jnp.minimum(a, b)
    hi2, lo2 = jnp.maximum(c, d), jnp.minimum(c, d)
    return jnp.maximum(hi1, hi2) + jnp.maximum(jnp.minimum(hi1, hi2), jnp.maximum(lo1, lo2))


def _gates_t(aff, sel, gt_ref, grp_ref):
    rows = [sel[e:e + 1, :] for e in range(N_EXPERTS)]
    g_score = [_top2_sum(*rows[g * EXPERTS_PER_GROUP:(g + 1) * EXPERTS_PER_GROUP]) for g in range(N_GROUPS)]
    best = g_score[0]
    best_g = jnp.zeros_like(best, dtype=jnp.int32)
    for g in range(1, N_GROUPS):
        better = g_score[g] > best
        best = jnp.where(better, g_score[g], best)
        best_g = jnp.where(better, g, best_g)
    picked = []
    for e in range(N_EXPERTS):
        g = e // EXPERTS_PER_GROUP
        rank = jnp.zeros_like(best_g)
        for o in range(g * EXPERTS_PER_GROUP, (g + 1) * EXPERTS_PER_GROUP):
            if o == e:
                continue
            ahead = (rows[o] >= rows[e]) if o < e else (rows[o] > rows[e])
            rank = rank + jnp.where(ahead, 1, 0)
        picked.append(jnp.where(best_g == g, rank, 2) < 2)
    w = [jnp.where(picked[e], aff[e:e + 1, :], 0.0) for e in range(N_EXPERTS)]
    total = w[0]
    for e in range(1, N_EXPERTS):
        total = total + w[e]
    for e in range(N_EXPERTS):
        gt_ref[e:e + 1, :] = w[e] / total
    grp_ref[...] = best_g


OUT_HALF = D_MODEL // 2


def _outproj_kernel(x_ref, xc_ref, conv_ref, of_ref, ob_ref, att_ref, attc_ref, w_ref, mod_ref, nw_ref, rw_ref,
                    rb_ref, x1_ref, h2_ref, gt_ref, grp_ref, *, n_lat_tiles):
    is_lat = pl.program_id(0) < n_lat_tiles
    ret = (of_ref[...].astype(F32) + ob_ref[...].astype(F32)).astype(BF16)
    att = jnp.where(is_lat, att_ref[...], attc_ref[...])
    mix = jnp.concatenate([conv_ref[...], ret, att], axis=1)
    halves = []
    ssq = 0.0
    for hf in range(2):
        cols = slice(hf * OUT_HALF, (hf + 1) * OUT_HALF)
        o = jnp.dot(mix, w_ref[:, cols], preferred_element_type=F32)
        x1 = jnp.where(is_lat, x_ref[:, cols], xc_ref[:, cols]) + mod_ref[2:3, cols] * o
        x1_ref[:, cols] = x1
        ssq = ssq + jnp.sum(x1 * x1, axis=-1, keepdims=True)
        halves.append(x1)
    inv = lax.rsqrt(ssq * (1.0 / D_MODEL) + EPS)
    logit_parts = 0.0
    for hf in range(2):
        cols = slice(hf * OUT_HALF, (hf + 1) * OUT_HALF)
        h2 = (halves[hf] * inv * nw_ref[:, cols]) * (1.0 + mod_ref[4:5, cols]) + mod_ref[3:4, cols]
        hi = h2.astype(BF16)
        h2_ref[:, cols] = hi
        lo = (h2 - hi.astype(F32)).astype(BF16)
        logit_parts = (logit_parts + jnp.dot(hi, rw_ref[cols, :], preferred_element_type=F32)
                       + jnp.dot(lo, rw_ref[cols, :], preferred_element_type=F32))
    lt = logit_parts.T
    logits = lt[0:N_EXPERTS, :] + lt[N_EXPERTS:2 * N_EXPERTS, :]
    aff = jax.nn.sigmoid(logits)
    _gates_t(aff, aff + rb_ref[...], gt_ref, grp_ref)


def _outproj_call(x_lat, x_ctx, ctx_block0, conv, of, ob, att, att_c, w_ext, mod, nw, rw3, rb, n_lat_tiles,
                  n_tiles):
    d = x_lat.shape[1]
    tm = ROW_TILE
    row = lambda i: (i, 0)
    const2 = lambda i: (0, 0)
    lat = lambda i: (jnp.minimum(i, n_lat_tiles - 1), 0)
    cls = lambda i: (jnp.where(i < n_lat_tiles, 1, 0), 0, 0)
    return pl.pallas_call(
        functools.partial(_outproj_kernel, n_lat_tiles=n_lat_tiles),
        grid=(n_tiles,),
        in_specs=[pl.BlockSpec((tm, d), lat),
                  pl.BlockSpec((tm, d), lambda i: (ctx_block0 + jnp.maximum(i - n_lat_tiles, 0), 0)),
                  pl.BlockSpec((tm, CONV_W), row),
                  pl.BlockSpec((tm, RET_W), row),
                  pl.BlockSpec((tm, RET_W), row),
                  pl.BlockSpec((tm, ATT_W), lat),
                  pl.BlockSpec((tm, ATT_W), lambda i: (jnp.maximum(i - n_lat_tiles, 0), 0)),
                  pl.BlockSpec(w_ext.shape, const2),
                  pl.BlockSpec((None, 8, d), cls),
                  pl.BlockSpec((1, d), const2),
                  pl.BlockSpec((d, LANES), const2),
                  pl.BlockSpec((N_EXPERTS, 1), const2)],
        out_specs=(pl.BlockSpec((tm, d), row), pl.BlockSpec((tm, d), row),
                   pl.BlockSpec((N_EXPERTS, tm), lambda i: (0, i)), pl.BlockSpec((1, tm), lambda i: (0, i))),
        out_shape=(jax.ShapeDtypeStruct((n_tiles * tm, d), F32), jax.ShapeDtypeStruct((n_tiles * tm, d), BF16),
                   jax.ShapeDtypeStruct((N_EXPERTS, n_tiles * tm), F32),
                   jax.ShapeDtypeStruct((1, n_tiles * tm), jnp.int32)),
        compiler_params=_cparams(("arbitrary",)),
        name="outproj",
    )(x_lat, x_ctx, conv, of, ob, att, att_c, w_ext, mod, nw, rw3, rb)


MOE_DENSE_ROWS = 256


def _group_mlp(xb, gate4, wgu_ref, wd_ref):
    parts = []
    for k in range(EXPERTS_PER_GROUP):
        gu = jnp.dot(xb, wgu_ref[k], preferred_element_type=F32)
        parts.append((_silu(gu[:, :D_EXPERT]) * gu[:, D_EXPERT:] * gate4[:, k:k + 1]).astype(BF16))
    return jnp.dot(jnp.concatenate(parts, axis=1), wd_ref[...], preferred_element_type=F32)


def _moe_kernel(h_ref, gate8_ref, grp_ref, x1_ref, g2_ref, tri_ref, wgu_ref, wd_ref, *rest, caps, final_norm):
    rest = list(rest)
    fw_ref = rest.pop(0) if final_norm else None
    o_ref, cnt_scr = rest
    tm = h_ref.shape[0]
    grp = grp_ref[...]
    o_ref[...] = jnp.zeros_like(o_ref)
    rows8 = lax.broadcasted_iota(jnp.int32, (SUBLANES, tm), 0)
    member8 = jnp.where(rows8 == grp, 1.0, 0.0).astype(BF16)
    cnt_scr[...] = jnp.dot(member8, tri_ref[...], preferred_element_type=F32)

    def gates_of(g8):
        return g8[:, :EXPERTS_PER_GROUP] + g8[:, EXPERTS_PER_GROUP:]

    def group(g, carry):
        incl = cnt_scr[pl.ds(g, 1), :]
        count = jnp.max(incl)
        gate8_g, wgu_g, wd_g = gate8_ref.at[g], wgu_ref.at[g], wd_ref.at[g]

        def compacted(cap):
            pos = jnp.where(grp == g, incl.astype(jnp.int32) - 1, -1)
            slot = lax.broadcasted_iota(jnp.int32, (cap, tm), 0)
            onehot = jnp.where(slot == pos, 1.0, 0.0).astype(BF16)
            xg = jnp.dot(onehot, h_ref[...], preferred_element_type=F32).astype(BF16)
            g8 = jnp.dot(onehot, gate8_g[...], preferred_element_type=F32)
            y = _group_mlp(xg, gates_of(g8), wgu_g, wd_g)
            o_ref[...] += lax.dot_general(onehot, y.astype(BF16), (((0,), (0,)), ((), ())),
                                          preferred_element_type=F32)

        lower = 0
        for cap in caps:
            pl.when((count > lower) & (count <= cap))(functools.partial(compacted, cap))
            lower = cap

        @pl.when(count > caps[-1])
        def _():
            def chunk(ci, inner):
                rows = pl.ds(pl.multiple_of(ci * MOE_DENSE_ROWS, MOE_DENSE_ROWS), MOE_DENSE_ROWS)
                o_ref[rows, :] += _group_mlp(h_ref[rows, :], gates_of(gate8_g[rows, :].astype(F32)), wgu_g, wd_g)
                return inner

            lax.fori_loop(0, tm // MOE_DENSE_ROWS, chunk, 0)

        return carry

    lax.fori_loop(0, N_GROUPS, group, 0)
    x2 = x1_ref[...] + g2_ref[...] * o_ref[...]
    if final_norm:
        x2 = x2 * lax.rsqrt(jnp.mean(x2 * x2, axis=-1, keepdims=True) + EPS) * fw_ref[...]
    o_ref[...] = x2


def _moe_call(h2, gate8, grp, x1, g2, wgu, wd, tm, block0, n_tiles, final_w=None):
    out_rows, d = x1.shape
    caps = (tm // 4, 3 * tm // 8, tm // 2)
    assert all(cap % 16 == 0 for cap in caps) and tm % MOE_DENSE_ROWS == 0
    tri = jnp.asarray(np.triu(np.ones((tm, tm), np.float32)), dtype=BF16)
    rows = lambda i: (block0 + i, 0)
    const2 = lambda i: (0, 0)
    const3 = lambda i: (0, 0, 0)
    once = pl.Buffered(1)
    in_specs = [pl.BlockSpec((tm, d), rows),
                pl.BlockSpec((N_GROUPS, tm, 2 * EXPERTS_PER_GROUP), lambda i: (0, block0 + i, 0)),
                pl.BlockSpec((1, tm), lambda i: (0, block0 + i)),
                pl.BlockSpec((tm, d), rows),
                pl.BlockSpec((1, d), const2),
                pl.BlockSpec((tm, tm), const2, pipeline_mode=once),
                pl.BlockSpec(wgu.shape, lambda i: (0, 0, 0, 0), pipeline_mode=once),
                pl.BlockSpec(wd.shape, const3, pipeline_mode=once)]
    args = [h2, gate8, grp, x1, g2, tri, wgu, wd]
    if final_w is not None:
        in_specs.append(pl.BlockSpec((1, d), const2))
        args.append(final_w)
    return pl.pallas_call(
        functools.partial(_moe_kernel, caps=caps, final_norm=final_w is not None),
        grid=(n_tiles,),
        in_specs=in_specs,
        out_specs=pl.BlockSpec((tm, d), rows),
        out_shape=jax.ShapeDtypeStruct((out_rows, d), F32),
        scratch_shapes=[pltpu.VMEM((SUBLANES, tm), F32)],
        input_output_aliases={3: 0},
        compiler_params=_cparams(("arbitrary",)),
        name="experts",
    )(*args)


_SWAP32 = np.concatenate([np.arange(8, 16), np.arange(0, 8), np.arange(24, 32), np.arange(16, 24)])


def _pad_cols(w, width):
    return jnp.pad(w, ((0, 0), (0, width - w.shape[1])))


def _in_weight(w_in):
    sizes = (2 * CONV_W, RET_QK_W, RET_QK_W, RET_W, RET_W, RET_W, MLA_Q_RANK, MLA_KV_RANK, MLA_D_ROPE)
    offs = np.concatenate([[0], np.cumsum(sizes)])
    conv, rq, rk, rv, gf, gb, cq, ckv, kr = [w_in[:, offs[i]:offs[i + 1]] for i in range(len(sizes))]
    d = w_in.shape[0]

    def place_rope(w):
        return jnp.concatenate([jnp.zeros((d, MLA_D_NOPE), w.dtype), w,
                                jnp.zeros((d, LANES - MLA_D_NOPE - MLA_D_ROPE), w.dtype)], axis=1)

    ext = jnp.concatenate([conv, _pad_cols(rq, 256), _pad_cols(rk, 256), rv, gf, gb, _pad_cols(cq, 256), ckv,
                           place_rope(kr)], axis=1)
    assert ext.shape[1] == IN_EXT
    return ext.astype(BF16)


def _uq_weight(w_uq):
    r = w_uq.shape[0]
    w = w_uq.reshape(r, MLA_HEADS, MLA_D_NOPE + MLA_D_ROPE)
    nope, rope = w[..., :MLA_D_NOPE], w[..., MLA_D_NOPE:]
    zpad = jnp.zeros((r, MLA_HEADS, HEAD_PAD - MLA_D_NOPE - MLA_D_ROPE), w.dtype)
    main = jnp.concatenate([nope, rope, zpad], axis=-1).reshape(r, ATT_W)
    part = jnp.concatenate([jnp.zeros_like(nope), rope[..., _SWAP32], zpad], axis=-1).reshape(r, ATT_W)
    return jnp.concatenate([main, part], axis=1).astype(BF16)


def _ukv_weight(w_ukv):
    r = w_ukv.shape[0]
    w = w_ukv.reshape(r, MLA_HEADS, MLA_D_NOPE + MLA_D_V)
    zpad = jnp.zeros((r, MLA_HEADS, HEAD_PAD - MLA_D_NOPE), w.dtype)
    kpart = jnp.concatenate([w[..., :MLA_D_NOPE], zpad], axis=-1).reshape(r, ATT_W)
    vpart = jnp.concatenate([w[..., MLA_D_NOPE:], zpad], axis=-1).reshape(r, ATT_W)
    return jnp.concatenate([kpart, vpart], axis=1).astype(BF16)


def _out_weight(w_out):
    d = w_out.shape[1]
    conv, ret = w_out[:CONV_W], w_out[CONV_W:CONV_W + RET_W]
    att = w_out[CONV_W + RET_W:].reshape(MLA_HEADS, MLA_D_V, d)
    att = jnp.pad(att, ((0, 0), (0, HEAD_PAD - MLA_D_V), (0, 0))).reshape(ATT_W, d)
    return jnp.concatenate([conv, ret, att], axis=0).astype(BF16)


def _rope_tables(n_ctx, seq):
    f32 = np.float32
    inv = (f32(ROPE_BASE) ** (-np.arange(ROPE_PAIRS, dtype=f32) / f32(ROPE_PAIRS))).astype(f32)
    ar = (np.arange(seq // GRID_W, dtype=f32)[:, None] * inv).astype(f32)
    ac = (np.arange(GRID_W, dtype=f32)[:, None] * inv).astype(f32)
    row_cs = np.concatenate([np.cos(ar), np.cos(ar)], axis=1).astype(f32)
    row_sn = np.concatenate([-np.sin(ar), np.sin(ar)], axis=1).astype(f32)
    col_cs = np.concatenate([np.cos(ac), np.cos(ac)], axis=1).astype(f32)
    col_sn = np.concatenate([-np.sin(ac), np.sin(ac)], axis=1).astype(f32)

    def expand(row_t, col_t, ctx_value):
        rows = seq // GRID_W
        lat = jnp.concatenate([jnp.broadcast_to(jnp.asarray(row_t)[:, None, :], (rows, GRID_W, 16)),
                               jnp.broadcast_to(jnp.asarray(col_t)[None, :, :], (rows, GRID_W, 16))],
                              axis=-1).reshape(seq, ROPE_DIM)
        full = jnp.concatenate([lat, jnp.full((n_ctx, ROPE_DIM), ctx_value, F32)], axis=0)
        return jnp.tile(full, (1, LANES // ROPE_DIM))

    return expand(row_cs, col_cs, 1.0), expand(row_sn, col_sn, 0.0)


def _router_weight(router_w):
    hi = router_w.astype(BF16)
    lo = (router_w - hi.astype(F32)).astype(BF16)
    return jnp.pad(jnp.concatenate([hi, lo], axis=1), ((0, 0), (0, LANES - 2 * N_EXPERTS)))


def _group_gates(gates_t):
    t = gates_t.shape[1]
    g4 = jnp.transpose(gates_t.reshape(N_GROUPS, EXPERTS_PER_GROUP, t), (0, 2, 1))
    hi = g4.astype(BF16)
    lo = (g4 - hi.astype(F32)).astype(BF16)
    return jnp.concatenate([hi, lo], axis=-1)


def _group_weights(w_gate, w_up, w_down):
    e, d, f = w_gate.shape
    gu = jnp.concatenate([w_gate, w_up], axis=-1).astype(BF16)
    gu = gu.reshape(N_GROUPS, EXPERTS_PER_GROUP, d, 2 * f)
    wd = w_down.astype(BF16).reshape(N_GROUPS, EXPERTS_PER_GROUP * f, d)
    return gu, wd


def kernel(x, c, ctx, c_ctx, ada_w, ada_b, norm1_w, norm2_w, w_in, conv_dw, conv_b, conv_ln_w, conv_ln_b,
           mla_q_norm_w, mla_w_uq, mla_kv_norm_w, mla_w_ukv, w_out, router_w, router_bias,
           moe_w_gate, moe_w_up, moe_w_down, final_norm_w):
    batch, seq, d = x.shape
    n_ctx = ctx.shape[1]
    depth = ada_w.shape[0]
    t_all = seq + n_ctx
    assert batch == 1 and d == D_MODEL
    assert n_ctx % ROW_TILE == 0 and seq % ROW_TILE == 0 and n_ctx <= MOE_TILE and seq % MOE_TILE == 0
    assert seq % ATT_TQ == 0 and t_all % ATT_TK == 0 and n_ctx % RET_CHUNK == 0 and seq % n_ctx == 0
    n_lat_tiles = seq // ROW_TILE
    n_all_tiles = t_all // ROW_TILE

    cond = jnp.zeros((8, d), F32).at[0].set(c_ctx).at[1].set(c[0])
    mod = _ada_call(cond, ada_w, ada_b)[:, :2, :].reshape(depth, 2, 6, d)
    mod8 = jnp.pad(mod, ((0, 0), (0, 0), (0, 2), (0, 0)))
    cs, sn = _rope_tables(n_ctx, seq)
    ret_tabs = _ret_tables()
    rw3 = _router_weight(router_w)
    rb = router_bias.reshape(N_EXPERTS, 1)

    x_lat, x_ctx, ctx_block0 = x[0], ctx[0], 0
    for l in range(depth):
        last = l == depth - 1
        u, rq, rk, rv, gf, gb, q_att, k_att, v_att = _inproj_call(
            x_lat, x_ctx, ctx_block0, t_all, norm1_w[l][None, :], mod8[l][:, :2, :], _in_weight(w_in[l]), cs, sn,
            mla_q_norm_w[l][None, :], _uq_weight(mla_w_uq[l]), mla_kv_norm_w[l][None, :], _ukv_weight(mla_w_ukv[l]),
            n_lat_tiles)
        conv = _conv_call(u, conv_dw[l], conv_b[l][None, :], conv_ln_w[l][None, :], conv_ln_b[l][None, :],
                          n_lat_tiles)
        of, ob = _ret_call(rq, rk, rv, gf, gb, ret_tabs, seq)
        att = _attn_call(q_att, k_att, v_att, ATT_TQ, ATT_TK, seq // ATT_TQ)
        att_c = att
        if not last:
            tqc = min(ATT_TQ, n_ctx)
            att_c = _attn_call(q_att[:, seq:], k_att[:, seq:], v_att[:, seq:], tqc, n_ctx, n_ctx // tqc)
        n_tiles = n_lat_tiles if last else n_all_tiles
        x1, h2, gates_t, grp = _outproj_call(x_lat, x_ctx, ctx_block0, conv, of, ob, att, att_c,
                                             _out_weight(w_out[l]), mod8[l], norm2_w[l][None, :], rw3, rb,
                                             n_lat_tiles, n_tiles)
        gate8 = _group_gates(gates_t)
        wgu, wd = _group_weights(moe_w_gate[l], moe_w_up[l], moe_w_down[l])
        x_all = _moe_call(h2, gate8, grp, x1, mod[l, 1, 5][None, :], wgu, wd, MOE_TILE, 0, seq // MOE_TILE,
                          final_w=final_norm_w[None, :] if last else None)
        if not last:
            x_all = _moe_call(h2, gate8, grp, x_all, mod[l, 0, 5][None, :], wgu, wd, n_ctx, seq // n_ctx, 1)
        x_lat, x_ctx, ctx_block0 = x_all, x_all, n_lat_tiles
    return x_all[None]
```
